```python
import functools
import jax, jax.numpy as jnp
from jax import lax
import numpy as np

D_MODEL = 1024
BATCH = 4
SEQ = 4096
DEPTH = 2
DEC_BATCH = 16
DEC_SEQ = 32
PAST_LEN = 1024

CHUNK = 64
D_MIX = D_MODEL
RG_WIDTH = D_MIX // 2
RG_BLOCKS = 8
RG_BLOCK = RG_WIDTH // RG_BLOCKS
CONV_W = 4
RG_C = 8.0
RET_HEADS = 4
RET_DK = 64
RET_DV = 64
RET_QK = RET_HEADS * RET_DK
RET_WIDTH = RET_HEADS * RET_DV
HG_HEADS = 4
HG_DK = 64
HG_DV = 64
HG_QK = HG_HEADS * HG_DK
HG_WIDTH = HG_HEADS * HG_DV
ROPE_BASE = 10000.0
LN_EPS = 1e-5
F_EPS = 1e-6
DN_ALPHA = (2 * DEPTH) ** 0.25
DN_BETA = (8 * DEPTH) ** -0.25
SPLITS = (RG_WIDTH, RG_WIDTH, RET_QK, RET_QK, RET_WIDTH, RET_WIDTH, HG_QK, HG_QK, HG_WIDTH, HG_WIDTH)
D_IN = sum(SPLITS)
SPLIT_IDX = tuple(int(v) for v in np.cumsum(SPLITS)[:-1])

kernel_name = "hymba_rglru_retention_hgrn2_deepnorm_stream"


def _layer_norm(x, g, b):
    xf = x.astype(jnp.float32)
    mu = xf.mean(-1, keepdims=True)
    var = jnp.mean(jnp.square(xf - mu), -1, keepdims=True)
    return ((xf - mu) * lax.rsqrt(var + LN_EPS) * g + b).astype(x.dtype)


def _head_layernorm(o, g):
    mu = o.mean(-1, keepdims=True)
    var = jnp.mean(jnp.square(o - mu), -1, keepdims=True)
    y = (o - mu) * lax.rsqrt(var + LN_EPS)
    return y.reshape(o.shape[0], o.shape[1], -1) * g


def _head_rmsnorm(o, g):
    y = o * lax.rsqrt(jnp.mean(jnp.square(o), -1, keepdims=True) + LN_EPS)
    return y.reshape(o.shape[0], o.shape[1], -1) * g


def _rotary(x, pos):
    half = x.shape[-1] // 2
    inv = ROPE_BASE ** (-jnp.arange(half, dtype=jnp.float32) / half)
    ang = pos.astype(jnp.float32)[:, None] * inv[None, :]
    cos = jnp.cos(ang)[None, :, None, :]
    sin = jnp.sin(ang)[None, :, None, :]
    x1, x2 = x[..., :half], x[..., half:]
    return jnp.concatenate([x1 * cos - x2 * sin, x2 * cos + x1 * sin], axis=-1)


def _causal_conv(u, buf, w, b):
    L = u.shape[1]
    full = jnp.concatenate([buf, u], axis=1)
    y = b
    for j in range(CONV_W):
        y = y + full[:, j:j + L] * w[j]
    return y, full[:, -(CONV_W - 1):]


def _lin_combine(e1, e2):
    a1, b1 = e1
    a2, b2 = e2
    return a1 * a2, a2 * b1 + b2


def _rglru(u, h0, wa, ba, wx, bx, lam):
    bsz, L, _ = u.shape
    ub = u.reshape(bsz, L, RG_BLOCKS, RG_BLOCK)
    r = jax.nn.sigmoid(jnp.einsum('blgi,gij->blgj', ub, wa).reshape(bsz, L, RG_WIDTH) + ba)
    gi = jax.nn.sigmoid(jnp.einsum('blgi,gij->blgj', ub, wx).reshape(bsz, L, RG_WIDTH) + bx)
    log_a = RG_C * r * jax.nn.log_sigmoid(lam.astype(jnp.float32))
    a = jnp.exp(log_a)
    b = jnp.sqrt(-jnp.expm1(2.0 * log_a)) * (gi * u)
    a_cum, b_cum = lax.associative_scan(_lin_combine, (a, b), axis=1)
    h = a_cum * h0[:, None, :] + b_cum
    return h, h[:, -1]


def _retention_chunk(S, q, k, v, log_g):
    L = q.shape[1]
    j = jnp.arange(L, dtype=jnp.float32)
    rel = j[:, None] - j[None, :]
    dmat = jnp.where(rel[None] >= 0, jnp.exp(jnp.maximum(rel, 0.0)[None] * log_g[:, None, None]), 0.0)
    scores = jnp.einsum('blhd,bshd->bhls', q, k) * dmat[None]
    intra = jnp.einsum('bhls,bshv->blhv', scores, v)
    cross = jnp.einsum('blhd,bhdv->blhv', q, S) * jnp.exp((j[:, None] + 1.0) * log_g[None, :])[None, :, :, None]
    k_dec = k * jnp.exp((L - 1.0 - j)[:, None] * log_g[None, :])[None, :, :, None]
    S_new = jnp.exp(L * log_g)[None, :, None, None] * S + jnp.einsum('bshd,bshv->bhdv', k_dec, v)
    return S_new, intra + cross


def _hgrn2_chunk(S, q, k, v, log_f):
    L = q.shape[1]
    cum = jnp.cumsum(log_f, axis=1)
    tri = jnp.tril(jnp.ones((L, L), dtype=bool))[None, :, :, None, None]
    diff = cum[:, :, None] - cum[:, None, :]
    decay = jnp.where(tri, jnp.exp(jnp.minimum(diff, 0.0)), 0.0)
    scores = jnp.einsum('bthd,btshd,bshd->bhts', q, decay, k)
    intra = jnp.einsum('bhts,bshv->bthv', scores, v)
    cross = jnp.einsum('bthd,bhdv->bthv', q * jnp.exp(cum), S)
    last = cum[:, -1]
    k_dec = k * jnp.exp(last[:, None] - cum)
    S_new = jnp.exp(last)[..., None] * S + jnp.einsum('bshd,bshv->bhdv', k_dec, v)
    return S_new, intra + cross


def _chunkwise(step, S0, seqs):
    L = seqs[0].shape[1]
    if L <= CHUNK:
        return step(S0, *seqs)
    n = L // CHUNK

    def to_chunks(a):
        return jnp.moveaxis(a.reshape(a.shape[0], n, CHUNK, *a.shape[2:]), 1, 0)

    S, o = lax.scan(lambda S, xs: step(S, *xs), S0, tuple(to_chunks(a) for a in seqs))
    o = jnp.moveaxis(o, 0, 1)
    return S, o.reshape(o.shape[0], L, *o.shape[3:])


def _layer(x, conv_buf, h_rg, s_ret, s_hg, pos, lb, w_in, conv_w, conv_b, rg_wa, rg_ba, rg_wx, rg_bx,
           rg_lam, ret_gn_g, hg_bf, hg_norm_g, w_out, ln_g, ln_b):
    f32 = jnp.float32
    bsz, L, _ = x.shape
    proj = jnp.einsum('bld,de->ble', x, w_in)
    rg_x, rg_gate, r_q, r_k, r_v, r_gate, h_q, h_f, h_i, h_gate = jnp.split(proj, SPLIT_IDX, axis=-1)
    u, conv_new = _causal_conv(rg_x, conv_buf, conv_w, conv_b)
    h, h_last = _rglru(u.astype(f32), h_rg.astype(f32), rg_wa, rg_ba, rg_wx, rg_bx, rg_lam)
    y_a = h * jax.nn.silu(rg_gate.astype(f32))
    q = _rotary(r_q.astype(f32).reshape(bsz, L, RET_HEADS, RET_DK), pos) * RET_DK ** -0.5
    k = _rotary(r_k.astype(f32).reshape(bsz, L, RET_HEADS, RET_DK), pos)
    v = r_v.astype(f32).reshape(bsz, L, RET_HEADS, RET_DV)
    log_g = jnp.log1p(-jnp.exp2(-5.0 - jnp.arange(RET_HEADS, dtype=f32)))
    s_ret_new, o_b = _chunkwise(functools.partial(_retention_chunk, log_g=log_g), s_ret.astype(f32), (q, k, v))
    y_b = _head_layernorm(o_b, ret_gn_g) * jax.nn.silu(r_gate.astype(f32))
    z = h_f.astype(f32) + hg_bf
    f = lb + (1.0 - lb) * jax.nn.sigmoid(z)
    log_f = jnp.log(jnp.maximum(f, F_EPS))
    k_c = (1.0 - lb) * jax.nn.sigmoid(-z)
    shp = (bsz, L, HG_HEADS, HG_DK)
    s_hg_new, o_c = _chunkwise(_hgrn2_chunk, s_hg.astype(f32),
                               (h_q.astype(f32).reshape(shp), k_c.reshape(shp),
                                h_i.astype(f32).reshape(bsz, L, HG_HEADS, HG_DV), log_f.reshape(shp)))
    y_c = _head_rmsnorm(o_c, hg_norm_g) * jax.nn.silu(h_gate.astype(f32))
    y = jnp.einsum('ble,ed->bld', jnp.concatenate([y_a, y_b, y_c], axis=-1).astype(x.dtype), w_out)
    x_new = _layer_norm(DN_ALPHA * x + y, ln_g, ln_b)
    return x_new, conv_new, h_last, s_ret_new, s_hg_new


def setup_inputs(seed: int = 0) -> dict:
    key = jax.random.key(seed)
    ks = jax.random.split(key, 24)
    f32 = jnp.float32
    nrm = lambda k, shp, s: jax.random.normal(k, shp, f32) * s
    u = jax.random.uniform(ks[10], (DEPTH, RG_WIDTH), f32, 0.9, 0.999)
    s = u ** (1.0 / RG_C)
    return {
        'x_prompt': nrm(ks[0], (BATCH, SEQ, D_MODEL), 1.0),
        'x_sample': nrm(ks[1], (DEC_BATCH, DEC_SEQ, D_MODEL), 1.0),
        'cache_conv': nrm(ks[2], (DEPTH, DEC_BATCH, CONV_W - 1, RG_WIDTH), 1.0),
        'state_rglru': nrm(ks[3], (DEPTH, DEC_BATCH, RG_WIDTH), 1.0),
        'state_ret': nrm(ks[4], (DEPTH, DEC_BATCH, RET_HEADS, RET_DK, RET_DV), 4.0),
        'state_hgrn': nrm(ks[5], (DEPTH, DEC_BATCH, HG_HEADS, HG_DK, HG_DV), 1.0),
        'w_in': nrm(ks[6], (DEPTH, D_MODEL, D_IN), D_MODEL ** -0.5),
        'conv_w': nrm(ks[7], (DEPTH, CONV_W, RG_WIDTH), CONV_W ** -0.5),
        'conv_b': nrm(ks[8], (DEPTH, RG_WIDTH), 0.02),
        'rg_wa': nrm(ks[9], (DEPTH, RG_BLOCKS, RG_BLOCK, RG_BLOCK), RG_BLOCK ** -0.5),
        'rg_ba': nrm(ks[11], (DEPTH, RG_WIDTH), 0.02),
        'rg_wx': nrm(ks[12], (DEPTH, RG_BLOCKS, RG_BLOCK, RG_BLOCK), RG_BLOCK ** -0.5),
        'rg_bx': nrm(ks[13], (DEPTH, RG_WIDTH), 0.02),
        'rg_lambda': jnp.log(s) - jnp.log1p(-s),
        'ret_gn_g': 1.0 + nrm(ks[14], (DEPTH, RET_WIDTH), 0.02),
        'hg_bf': nrm(ks[15], (DEPTH, HG_QK), 0.1),
        'hg_lb_logits': 1.0 + nrm(ks[16], (DEPTH, HG_QK), 0.1),
        'hg_norm_g': 1.0 + nrm(ks[17], (DEPTH, HG_WIDTH), 0.02),
        'w_out': nrm(ks[18], (DEPTH, D_MIX, D_MODEL), DN_BETA * D_MIX ** -0.5),
        'ln_g': 1.0 + nrm(ks[19], (DEPTH, D_MODEL), 0.02),
        'ln_b': nrm(ks[20], (DEPTH, D_MODEL), 0.02),
    }


def reference(x_prompt, x_sample, cache_conv, state_rglru, state_ret, state_hgrn, w_in, conv_w, conv_b,
              rg_wa, rg_ba, rg_wx, rg_bx, rg_lambda, ret_gn_g, hg_bf, hg_lb_logits, hg_norm_g, w_out,
              ln_g, ln_b):
    f32 = jnp.float32
    lb_p = jax.nn.softmax(hg_lb_logits.astype(f32), axis=0)
    lb_all = jnp.cumsum(lb_p, axis=0) - lb_p[0]
    bp, lp = x_prompt.shape[0], x_prompt.shape[1]
    ls = x_sample.shape[1]
    pos_p = jnp.arange(lp, dtype=jnp.int32)
    pos_s = PAST_LEN + jnp.arange(ls, dtype=jnp.int32)
    xp, xs = x_prompt, x_sample
    conv_p, rg_p, ret_p, hg_p = [], [], [], []
    conv_s, rg_s, ret_s, hg_s = [], [], [], []
    for l in range(DEPTH):
        params = (w_in[l], conv_w[l], conv_b[l], rg_wa[l], rg_ba[l], rg_wx[l], rg_bx[l], rg_lambda[l],
                  ret_gn_g[l], hg_bf[l], hg_norm_g[l], w_out[l], ln_g[l], ln_b[l])
        xp, c1, r1, s1, g1 = _layer(
            xp, jnp.zeros((bp, CONV_W - 1, RG_WIDTH), xp.dtype), jnp.zeros((bp, RG_WIDTH), f32),
            jnp.zeros((bp, RET_HEADS, RET_DK, RET_DV), f32), jnp.zeros((bp, HG_HEADS, HG_DK, HG_DV), f32),
            pos_p, lb_all[l], *params)
        xs, c2, r2, s2, g2 = _layer(
            xs, cache_conv[l], state_rglru[l], state_ret[l], state_hgrn[l], pos_s, lb_all[l], *params)
        conv_p.append(c1); rg_p.append(r1); ret_p.append(s1); hg_p.append(g1)
        conv_s.append(c2); rg_s.append(r2); ret_s.append(s2); hg_s.append(g2)
    return (xp, xs, jnp.stack(conv_p), jnp.stack(rg_p), jnp.stack(ret_p), jnp.stack(hg_p),
            jnp.stack(conv_s), jnp.stack(rg_s), jnp.stack(ret_s), jnp.stack(hg_s))
```

```python
import functools
import math

import jax
import jax.numpy as jnp
import numpy as np
from jax import lax
from jax.experimental import pallas as pl
from jax.experimental.pallas import tpu as pltpu

F32 = jnp.float32
BF16 = jnp.bfloat16

D_MODEL = 1024
RG_WIDTH = 512
RG_BLOCKS = 8
RG_BLOCK = RG_WIDTH // RG_BLOCKS
CONV_W = 4
RG_C = 8.0
HEADS = 4
HEAD_DIM = 64
HW = HEADS * HEAD_DIM
ROPE_BASE = 10000.0
LN_EPS = 1e-5
F_EPS = 1e-6
PAST_LEN = 1024
SPLITS = (RG_WIDTH, RG_WIDTH, HW, HW, HW, HW, HW, HW, HW, HW)
D_IN = sum(SPLITS)
SEG = tuple(int(v) for v in np.cumsum((0,) + SPLITS))

SUBLANES = 8
LANES = 128
MXU_DIM = 256
VMEM_LIMIT_BYTES = 56 * 1024 * 1024

MAX_CHUNK = 64
LEVEL0_ROWS = SUBLANES


def _mixer_tables(chunk):
    c = chunk
    t = np.arange(c)
    hid_k = np.tile(np.repeat(np.arange(HEADS), HEAD_DIM // 2), 2)
    hid_n = np.repeat(np.arange(HEADS), HEAD_DIM)
    row_h = np.repeat(np.arange(HEADS), c)

    log_g = np.log1p(-np.exp2(-5.0 - np.arange(HEADS)))
    rel = t[:, None] - t[None, :]
    dmat = np.where(rel >= 0, np.exp(np.maximum(rel, 0)[None] * log_g[:, None, None]), 0.0)
    tabs = {
        "dall": dmat.transpose(1, 0, 2).reshape(c, HEADS * c),
        "gq": np.repeat(np.exp((t[:, None] + 1.0) * log_g[None, :]), HEAD_DIM, 1),
        "gk": np.exp((c - 1.0 - t)[:, None] * log_g[None, :])[:, hid_k],
        "gs": np.exp(c * log_g)[hid_k][:, None] * np.ones((1, HW)),
        "bdk": (hid_k[:, None] == hid_n[None, :]).astype(np.float64),
        "bdn": (hid_n[:, None] == hid_n[None, :]).astype(np.float64),
        "kbm": (row_h[:, None] == hid_k[None, :]).astype(np.float64),
        "vbm": (row_h[:, None] == hid_n[None, :]).astype(np.float64),
    }

    n_lv = int(round(math.log2(c // LEVEL0_ROWS)))
    j = t[None, :]
    tt = t[:, None]
    w_cross = (j <= tt).astype(np.float64)
    w_state = (j > tt).astype(np.float64)
    r0 = (tt // LEVEL0_ROWS) * LEVEL0_ROWS + LEVEL0_ROWS // 2 - 1
    w_l0 = ((j > r0) & (j <= tt)).astype(np.float64) - ((j > tt) & (j <= r0)).astype(np.float64)
    blocks = [w_cross, w_state, w_l0]
    lvl = np.full((c, c), -1.0)
    causal = t[None, :] <= t[:, None]
    same0 = (t[:, None] // LEVEL0_ROWS) == (t[None, :] // LEVEL0_ROWS)
    lvl[causal & same0] = 0.0
    assigned = same0.copy()
    for lv in range(1, n_lv + 1):
        g = LEVEL0_ROWS * 2 ** lv
        r = (tt // g) * g + g // 2 - 1
        up = ((j > r) & (j <= tt)).astype(np.float64)
        lo = ((j > tt) & (j <= r)).astype(np.float64)
        blocks.append(up + lo)
        same = (t[:, None] // g) == (t[None, :] // g)
        lvl[causal & same & ~assigned] = float(lv)
        assigned |= same
    w = np.concatenate(blocks, axis=0)
    tabs["w3"] = np.concatenate([w, w, w], axis=1)
    tabs["lvl"] = np.tile(lvl, (1, HEADS))
    tabs["e64"] = tabs["bdn"] / HEAD_DIM
    return tabs, n_lv


def _rope_tables(pos):
    half = HEAD_DIM // 2
    inv = ROPE_BASE ** (-np.arange(half, dtype=np.float64) / half)
    ang = pos.astype(np.float64)[:, None] * inv[None, :]
    cos = np.tile(np.cos(ang), (1, HEADS))
    sin = np.tile(np.sin(ang), (1, HEADS))
    scale = HEAD_DIM ** -0.5
    return cos * scale, sin * scale, cos, sin


def _sigmoid(x):
    return 1.0 / (1.0 + jnp.exp(-x))


def _silu(x):
    return x * _sigmoid(x)


def _dot(a, b):
    return jnp.dot(a, b, preferred_element_type=F32)


def _dot_t(a, b):
    return lax.dot_general(a, b, (((1,), (1,)), ((), ())), preferred_element_type=F32)


def _tdot(a, b):
    return lax.dot_general(a, b, (((0,), (0,)), ((), ())), preferred_element_type=F32)


def _seg_mean(x, e64):
    hi = x.astype(BF16)
    lo = (x - hi.astype(F32)).astype(BF16)
    return _dot(hi, e64) + _dot(lo, e64)


def _stack_heads(x, mask):
    return (jnp.concatenate([x] * HEADS, axis=0) * mask).astype(BF16)


def _rg_scan(a, b, h_prev):
    c, w = a.shape
    g = c // SUBLANES
    a3 = a.reshape(g, SUBLANES, w)
    b3 = b.reshape(g, SUBLANES, w)
    sub = lax.broadcasted_iota(jnp.int32, a3.shape, 1)
    shift = 1
    while shift < SUBLANES:
        keep = sub >= shift
        a_sh = pltpu.roll(a3, shift, axis=1)
        b_sh = pltpu.roll(b3, shift, axis=1)
        b3 = jnp.where(keep, a3 * b_sh + b3, b3)
        a3 = jnp.where(keep, a3 * a_sh, a3)
        shift *= 2
    outs = []
    hb = h_prev
    for gi in range(g):
        hg = a3[gi] * hb + b3[gi]
        outs.append(hg)
        hb = jnp.broadcast_to(hg[SUBLANES - 1:SUBLANES, :], (SUBLANES, w))
    return jnp.concatenate(outs, axis=0), hb


def _layer_kernel(layer, depth, bt, tile, chunk, n_lv,
                  x_ref, cq_ref, sq_ref, ck_ref, sk_ref,
                  conv_in, h_in, sret_in, shg_in,
                  win_ref, wa_ref, wx_ref, wout_ref, p512_ref, p256_ref, p1024_ref,
                  dall_ref, gq_ref, gk_ref, gs_ref, bdk_ref, bdn_ref, kbm_ref, vbm_ref, w3_ref, lvl_ref, e64_ref,
                  xo_ref, conv_out, h_out, sret_out, shg_out,
                  xpad_s, rgg_s, q_s, k_s, v_s, rgate_s, hq_s, hf_s, hi_s, hgate_s,
                  u_s, rpre_s, ipre_s, y_s):
    n = bt * tile
    c = chunk
    n_chunks = tile // c
    dn_alpha = (2 * depth) ** 0.25

    @pl.when(pl.program_id(1) == 0)
    def _load_state():
        conv_out[...] = conv_in[...]
        h_out[...] = h_in[...]
        sret_out[...] = sret_in[...]
        shg_out[...] = shg_in[...]

    xb = x_ref[...].reshape(n, D_MODEL).astype(BF16)
    xpad_s[:, 0:SUBLANES, :] = conv_out[...]
    xpad_s[:, SUBLANES:SUBLANES + tile, :] = _dot(xb, win_ref[:, SEG[0]:SEG[1]]).reshape(bt, tile, RG_WIDTH)
    for dst, gi in ((rgg_s, 1), (q_s, 2), (k_s, 3), (v_s, 4), (rgate_s, 5), (hq_s, 6), (hf_s, 7), (hi_s, 8),
                    (hgate_s, 9)):
        dst[...] = _dot(xb, win_ref[:, SEG[gi]:SEG[gi + 1]])

    u = p512_ref[CONV_W:CONV_W + 1, :]
    for j in range(CONV_W):
        off = SUBLANES - (CONV_W - 1) + j
        u = u + xpad_s[:, off:off + tile, :] * p512_ref[j:j + 1, :]
    conv_out[...] = xpad_s[:, tile:tile + SUBLANES, :]
    u = u.reshape(n, RG_WIDTH)
    u_s[...] = u
    ub = u.astype(BF16)
    half = RG_WIDTH // 2
    for hf in range(2):
        sl = slice(hf * half, (hf + 1) * half)
        rpre_s[:, sl] = _dot(ub[:, sl], wa_ref[hf])
        ipre_s[:, sl] = _dot(ub[:, sl], wx_ref[hf])

    rg_ba = p512_ref[5:6, :]
    rg_bx = p512_ref[6:7, :]
    c_lam = RG_C * jax.nn.log_sigmoid(p512_ref[7:8, :])
    gn_g = p256_ref[0:1, :]
    hg_bf = p256_ref[1:2, :]
    hg_ng = p256_ref[2:3, :]
    logits = [p256_ref[3 + li:4 + li, :] for li in range(depth)]
    mx = functools.reduce(jnp.maximum, logits)
    ex = [jnp.exp(v - mx) for v in logits]
    den = functools.reduce(lambda s, v: s + v, ex)
    probs = [v / den for v in ex]
    lb = functools.reduce(lambda s, v: s + v, probs[:layer + 1]) - probs[0]
    one_m_lb = 1.0 - lb

    def chunk_body(ci, carry):
        r0 = pl.multiple_of(ci * c, c)
        bi = ci // n_chunks
        t0 = pl.multiple_of((ci - bi * n_chunks) * c, c)
        rows = pl.ds(r0, c)
        trow = pl.ds(t0, c)

        uu = u_s[rows, :]
        r_gate = _sigmoid(rpre_s[rows, :] + rg_ba)
        i_gate = _sigmoid(ipre_s[rows, :] + rg_bx)
        log_a = c_lam * r_gate
        a = jnp.exp(log_a)
        mult = jnp.sqrt(-jnp.tanh(log_a) * (a * a + 1.0))
        h, h_last = _rg_scan(a, mult * (i_gate * uu), h_out[bi])
        h_out[bi] = h_last
        y_s[rows, 0:RG_WIDTH] = (h * _silu(rgg_s[rows, :])).astype(BF16)

        cq, sq, ck, sk = cq_ref[trow, :], sq_ref[trow, :], ck_ref[trow, :], sk_ref[trow, :]
        q1, q2 = q_s[rows, 0:LANES], q_s[rows, LANES:HW]
        k1, k2 = k_s[rows, 0:LANES], k_s[rows, LANES:HW]
        qr = jnp.concatenate([q1 * cq - q2 * sq, q2 * cq + q1 * sq], axis=1)
        kr = jnp.concatenate([k1 * ck - k2 * sk, k2 * ck + k1 * sk], axis=1)
        v = v_s[rows, :]
        qb = qr.astype(BF16)
        scores = _dot_t(qb, _stack_heads(kr, kbm_ref[...]))
        p = (scores * dall_ref[...]).astype(BF16)
        s_ret = sret_out[bi]
        o_b = _dot(p, _stack_heads(v, vbm_ref[...])) + _dot(qb, s_ret.astype(BF16)) * gq_ref[...]
        upd = _tdot((kr * gk_ref[...]).astype(BF16), v.astype(BF16))
        sret_out[bi] = gs_ref[...] * s_ret + bdk_ref[...] * upd
        e64 = e64_ref[...]
        dev = o_b - _seg_mean(o_b, e64)
        y_b = dev * lax.rsqrt(_seg_mean(dev * dev, e64) + LN_EPS) * gn_g
        y_s[rows, RG_WIDTH:RG_WIDTH + HW] = (y_b * _silu(rgate_s[rows, :])).astype(BF16)

        z = hf_s[rows, :] + hg_bf
        ez = jnp.exp(-jnp.abs(z))
        inv = 1.0 / (1.0 + ez)
        pos = z >= 0.0
        sig_p = jnp.where(pos, inv, ez * inv)
        sig_n = jnp.where(pos, ez * inv, inv)
        log_f = jnp.log(jnp.maximum(lb + one_m_lb * sig_p, F_EPS))
        kc = one_m_lb * sig_n
        f_hi = log_f.astype(BF16)
        res = log_f - f_hi.astype(F32)
        f_mid = res.astype(BF16)
        f_lo = (res - f_mid.astype(F32)).astype(BF16)
        expo = _dot(w3_ref[...], jnp.concatenate([f_hi, f_mid, f_lo], axis=0))
        e_cross = expo[0:c]
        qh = hq_s[rows, :]
        vh = hi_s[rows, :]
        vbm = vbm_ref[...]
        lvl = lvl_ref[...]
        e0 = jnp.exp(expo[2 * c:3 * c])
        s0 = _dot_t((qh * e0).astype(BF16), _stack_heads(kc / e0, vbm))
        pm = jnp.where(lvl == 0.0, s0, 0.0)
        for lv in range(1, n_lv + 1):
            el = jnp.exp(expo[(2 + lv) * c:(3 + lv) * c])
            sl_ = _dot_t((qh * el).astype(BF16), _stack_heads(kc * el, vbm))
            pm = jnp.where(lvl == float(lv), sl_, pm)
        s_hg = shg_out[bi]
        o_c = _dot(pm.astype(BF16), _stack_heads(vh, vbm)) + _dot_t((qh * jnp.exp(e_cross)).astype(BF16),
                                                                     s_hg.astype(BF16))
        k_st = (kc * jnp.exp(expo[c:2 * c])).astype(BF16)
        dec = jnp.exp(e_cross[c - 1:c, :])
        shg_out[bi] = dec * s_hg + bdn_ref[...] * _tdot(vh.astype(BF16), k_st)
        y_c = o_c * lax.rsqrt(_seg_mean(o_c * o_c, e64) + LN_EPS) * hg_ng
        y_s[rows, RG_WIDTH + HW:D_MODEL] = (y_c * _silu(hgate_s[rows, :])).astype(BF16)
        return carry

    lax.fori_loop(0, bt * n_chunks, chunk_body, 0)

    xn = dn_alpha * x_ref[...].reshape(n, D_MODEL) + _dot(y_s[...], wout_ref[...])
    mu = jnp.mean(xn, axis=-1, keepdims=True)
    dv = xn - mu
    var = jnp.mean(dv * dv, axis=-1, keepdims=True)
    out = dv * lax.rsqrt(var + LN_EPS) * p1024_ref[0:1, :] + p1024_ref[1:2, :]
    xo_ref[...] = out.reshape(bt, tile, D_MODEL)


def _tiling(batch, seq):
    chunk = min(MAX_CHUNK, seq)
    tile = min(seq, 512)
    bt = max(1, min(batch, 128 // tile)) if tile < 128 else 1
    assert seq % tile == 0 and tile % chunk == 0 and batch % bt == 0 and chunk % 16 == 0
    return bt, tile, chunk


def _const(shape):
    zeros = (0,) * len(shape)
    return pl.BlockSpec(shape, lambda b, i: zeros)


def _layer_call(layer, depth, x, rope, states, weights, tabs, n_lv, bt, tile, chunk):
    batch, seq, _ = x.shape
    n = bt * tile
    grid = (batch // bt, seq // tile)
    kern = functools.partial(_layer_kernel, layer, depth, bt, tile, chunk, n_lv)

    def per_b(shape):
        return pl.BlockSpec((bt,) + shape, lambda b, i: (b,) + (0,) * len(shape))

    rope_spec = pl.BlockSpec((tile, LANES), lambda b, i: (i, 0))
    x_spec = pl.BlockSpec((bt, tile, D_MODEL), lambda b, i: (b, i, 0))
    state_specs = [per_b((SUBLANES, RG_WIDTH)), per_b((SUBLANES, RG_WIDTH)), per_b((HW, HW)), per_b((HW, HW))]
    tab_names = ("dall", "gq", "gk", "gs", "bdk", "bdn", "kbm", "vbm", "w3", "lvl", "e64")
    tab_args = [tabs[k] for k in tab_names]
    in_specs = ([x_spec] + [rope_spec] * 4 + state_specs + [_const(w.shape) for w in weights]
                + [_const(t.shape) for t in tab_args])
    out_shape = [jax.ShapeDtypeStruct(x.shape, F32)] + [jax.ShapeDtypeStruct(s.shape, F32) for s in states]
    out_specs = [x_spec] + state_specs
    scratch = ([pltpu.VMEM((bt, SUBLANES + tile, RG_WIDTH), F32), pltpu.VMEM((n, RG_WIDTH), F32)]
               + [pltpu.VMEM((n, HW), F32) for _ in range(8)]
               + [pltpu.VMEM((n, RG_WIDTH), F32) for _ in range(3)]
               + [pltpu.VMEM((n, D_MODEL), BF16)])
    return pl.pallas_call(
        kern, grid=grid, in_specs=in_specs, out_specs=out_specs, out_shape=out_shape, scratch_shapes=scratch,
        compiler_params=pltpu.CompilerParams(dimension_semantics=("arbitrary", "arbitrary"),
                                             vmem_limit_bytes=VMEM_LIMIT_BYTES),
        name=f"layer{layer}_t{tile}",
    )(x, *rope, *states, *weights, *tab_args)


def _half_split(w):
    d = w.shape[0]
    return w.reshape(d, HEADS, 2, HEAD_DIM // 2).transpose(0, 2, 1, 3).reshape(d, HW)


def _block_diag(w):
    eye = jnp.eye(w.shape[0], dtype=w.dtype)
    return jnp.einsum('gij,gk->gikj', w, eye).reshape(w.shape[0] * RG_BLOCK, w.shape[0] * RG_BLOCK)


_KPERM = np.concatenate([np.arange(HEAD_DIM // 2) + h * HEAD_DIM + p * (HEAD_DIM // 2)
                         for p in range(2) for h in range(HEADS)])


def _states_in(conv, h_rg, s_ret, s_hg):
    b = conv.shape[0]
    eye = jnp.eye(HEADS, dtype=F32)
    conv8 = jnp.pad(conv, ((0, 0), (SUBLANES - (CONV_W - 1), 0), (0, 0)))
    h8 = jnp.broadcast_to(h_rg[:, None, :], (b, SUBLANES, RG_WIDTH))
    ret = jnp.einsum('bhkv,hg->bhkgv', s_ret, eye).reshape(b, HW, HW)[:, _KPERM, :]
    hg = jnp.einsum('bhkv,hg->bhvgk', s_hg, eye).reshape(b, HW, HW)
    return conv8, h8, ret, hg


def _states_out(conv8, h8, ret, hg):
    b = conv8.shape[0]
    inv = np.argsort(_KPERM)
    ret5 = ret[:, inv, :].reshape(b, HEADS, HEAD_DIM, HEADS, HEAD_DIM)
    hg5 = hg.reshape(b, HEADS, HEAD_DIM, HEADS, HEAD_DIM)
    s_ret = jnp.stack([ret5[:, h, :, h, :] for h in range(HEADS)], axis=1)
    s_hg = jnp.stack([hg5[:, h, :, h, :].transpose(0, 2, 1) for h in range(HEADS)], axis=1)
    return conv8[:, SUBLANES - (CONV_W - 1):, :], h8[:, 0, :], s_ret, s_hg


def kernel(x_prompt, x_sample, cache_conv, state_rglru, state_ret, state_hgrn, w_in, conv_w, conv_b, rg_wa, rg_ba,
           rg_wx, rg_bx, rg_lambda, ret_gn_g, hg_bf, hg_lb_logits, hg_norm_g, w_out, ln_g, ln_b):
    depth = w_in.shape[0]
    bp, lp, _ = x_prompt.shape
    bs, ls, _ = x_sample.shape

    streams = []
    for batch, seq, pos0 in ((bp, lp, 0), (bs, ls, PAST_LEN)):
        bt, tile, chunk = _tiling(batch, seq)
        tabs_np, n_lv = _mixer_tables(chunk)
        tabs = {k: jnp.asarray(v, BF16 if k in ("w3", "e64") else F32) for k, v in tabs_np.items()}
        rope = tuple(jnp.asarray(t, F32) for t in _rope_tables(pos0 + np.arange(seq)))
        streams.append((bt, tile, chunk, n_lv, tabs, rope))

    xp, xs = x_prompt, x_sample
    zeros_p = _states_in(jnp.zeros((bp, CONV_W - 1, RG_WIDTH), F32), jnp.zeros((bp, RG_WIDTH), F32),
                         jnp.zeros((bp, HEADS, HEAD_DIM, HEAD_DIM), F32),
                         jnp.zeros((bp, HEADS, HEAD_DIM, HEAD_DIM), F32))
    outs_p, outs_s = [], []
    for l in range(depth):
        wl = w_in[l]
        win = jnp.concatenate([wl[:, SEG[0]:SEG[2]], _half_split(wl[:, SEG[2]:SEG[3]]),
                               _half_split(wl[:, SEG[3]:SEG[4]]), wl[:, SEG[4]:]], axis=1).astype(BF16)
        nb = RG_BLOCKS // 2
        wa = jnp.stack([_block_diag(rg_wa[l, :nb]), _block_diag(rg_wa[l, nb:])]).astype(BF16)
        wx = jnp.stack([_block_diag(rg_wx[l, :nb]), _block_diag(rg_wx[l, nb:])]).astype(BF16)
        p512 = jnp.concatenate([conv_w[l], conv_b[l][None], rg_ba[l][None], rg_bx[l][None], rg_lambda[l][None]], 0)
        p256 = jnp.concatenate([ret_gn_g[l][None], hg_bf[l][None], hg_norm_g[l][None], hg_lb_logits,
                                jnp.zeros((SUBLANES - 3 - depth, HW), F32)], 0)
        p1024 = jnp.concatenate([ln_g[l][None], ln_b[l][None], jnp.zeros((SUBLANES - 2, D_MODEL), F32)], 0)
        weights = (win, wa, wx, w_out[l].astype(BF16), p512, p256, p1024)

        bt, tile, chunk, n_lv, tabs, rope = streams[0]
        res = _layer_call(l, depth, xp, rope, zeros_p, weights, tabs, n_lv, bt, tile, chunk)
        xp = res[0]
        outs_p.append(_states_out(*res[1:]))

        bt, tile, chunk, n_lv, tabs, rope = streams[1]
        st = _states_in(cache_conv[l], state_rglru[l], state_ret[l], state_hgrn[l])
        res = _layer_call(l, depth, xs, rope, st, weights, tabs, n_lv, bt, tile, chunk)
        xs = res[0]
        outs_s.append(_states_out(*res[1:]))

    stack = lambda outs, k: jnp.stack([o[k] for o in outs])
    return (xp, xs, stack(outs_p, 0), stack(outs_p, 1), stack(outs_p, 2), stack(outs_p, 3),
            stack(outs_s, 0), stack(outs_s, 1), stack(outs_s, 2), stack(outs_s, 3))
```

```python
import functools
import math

import jax
import jax.numpy as jnp
import numpy as np
from jax import lax
from jax.experimental import pallas as pl
from jax.experimental.pallas import tpu as pltpu

F32 = jnp.float32
BF16 = jnp.bfloat16

D_MODEL = 1024
RG_WIDTH = 512
RG_BLOCKS = 8
RG_BLOCK = RG_WIDTH // RG_BLOCKS
CONV_W = 4
RG_C = 8.0
HEADS = 4
HEAD_DIM = 64
HW = HEADS * HEAD_DIM
ROPE_BASE = 10000.0
LN_EPS = 1e-5
F_EPS = 1e-6
PAST_LEN = 1024
SPLITS = (RG_WIDTH, RG_WIDTH, HW, HW, HW, HW, HW, HW, HW, HW)
D_IN = sum(SPLITS)
SEG = tuple(int(v) for v in np.cumsum((0,) + SPLITS))
N_NARROW = 8

SUBLANES = 8
LANES = 128
VMEM_LIMIT_BYTES = 56 * 1024 * 1024

MAX_CHUNK = 64
BLOCK_ROWS = 128
PROJ_COLS = 256
MAX_TILE = 512
LEVEL0_ROWS = SUBLANES

QUADS_K = ((slice(0, 64), slice(0, 128)), (slice(128, 192), slice(0, 128)),
           (slice(64, 128), slice(128, 256)), (slice(192, 256), slice(128, 256)))
QUADS_N = ((slice(0, 128), slice(0, 128)), (slice(128, 256), slice(128, 256)))


def _mixer_tables(chunk):
    c = chunk
    t = np.arange(c)
    hid_k = np.tile(np.repeat(np.arange(HEADS), HEAD_DIM // 2), 2)
    hid_n = np.repeat(np.arange(HEADS), HEAD_DIM)
    row_h = np.repeat(np.arange(HEADS), c)

    log_g = np.log1p(-np.exp2(-5.0 - np.arange(HEADS)))
    rel = t[:, None] - t[None, :]
    dmat = np.where(rel >= 0, np.exp(np.maximum(rel, 0)[None] * log_g[:, None, None]), 0.0)
    tabs = {
        "dall": dmat.transpose(1, 0, 2).reshape(c, HEADS * c),
        "gq": np.repeat(np.exp((t[:, None] + 1.0) * log_g[None, :]), HEAD_DIM, 1),
        "gk": np.exp((c - 1.0 - t)[:, None] * log_g[None, :])[:, hid_k],
        "gs": np.exp(c * log_g)[hid_k][:, None] * np.ones((1, HW)),
        "bdk": (hid_k[:, None] == hid_n[None, :]).astype(np.float64),
        "bdn": (hid_n[:, None] == hid_n[None, :]).astype(np.float64),
        "kbm": (row_h[:, None] == hid_k[None, :]).astype(np.float64),
        "vbm": (row_h[:, None] == hid_n[None, :]).astype(np.float64),
    }

    n_lv = int(round(math.log2(c // LEVEL0_ROWS)))
    j = t[None, :]
    tt = t[:, None]
    w_cross = (j <= tt).astype(np.float64)
    w_state = (j > tt).astype(np.float64)
    r0 = (tt // LEVEL0_ROWS) * LEVEL0_ROWS + LEVEL0_ROWS // 2 - 1
    w_l0 = ((j > r0) & (j <= tt)).astype(np.float64) - ((j > tt) & (j <= r0)).astype(np.float64)
    blocks = [w_cross, w_state, w_l0]
    lvl = np.full((c, c), -1.0)
    causal = t[None, :] <= t[:, None]
    same0 = (t[:, None] // LEVEL0_ROWS) == (t[None, :] // LEVEL0_ROWS)
    lvl[causal & same0] = 0.0
    assigned = same0.copy()
    for lv in range(1, n_lv + 1):
        g = LEVEL0_ROWS * 2 ** lv
        r = (tt // g) * g + g // 2 - 1
        up = ((j > r) & (j <= tt)).astype(np.float64)
        lo = ((j > tt) & (j <= r)).astype(np.float64)
        blocks.append(up + lo)
        same = (t[:, None] // g) == (t[None, :] // g)
        lvl[causal & same & ~assigned] = float(lv)
        assigned |= same
    w = np.concatenate(blocks, axis=0)
    tabs["w3"] = np.concatenate([w, w, w], axis=1)
    tabs["lvl"] = np.tile(lvl, (1, HEADS))
    tabs["e64"] = tabs["bdn"] / HEAD_DIM
    return tabs, n_lv


_BF16_TABLES = ("kbm", "vbm", "w3", "e64")
_TABLE_ORDER = ("dall", "gq", "gk", "gs", "bdk", "bdn", "kbm", "vbm", "w3", "lvl", "e64")


def _rope_tables(pos):
    half = HEAD_DIM // 2
    inv = ROPE_BASE ** (-np.arange(half, dtype=np.float64) / half)
    ang = pos.astype(np.float64)[:, None] * inv[None, :]
    cos = np.tile(np.cos(ang), (1, HEADS))
    sin = np.tile(np.sin(ang), (1, HEADS))
    scale = HEAD_DIM ** -0.5
    return cos * scale, sin * scale, cos, sin


def _silu(x):
    return x * jax.nn.sigmoid(x)


def _dot(a, b):
    return jnp.dot(a, b, preferred_element_type=F32)


def _dot_t(a, b):
    return lax.dot_general(a, b, (((1,), (1,)), ((), ())), preferred_element_type=F32)


def _tdot(a, b):
    return lax.dot_general(a, b, (((0,), (0,)), ((), ())), preferred_element_type=F32)


def _seg_mean(x, e64):
    hi = x.astype(BF16)
    lo = (x - hi.astype(F32)).astype(BF16)
    return _dot(hi, e64) + _dot(lo, e64)


def _stack_heads(x, mask):
    xb = x.astype(BF16)
    return jnp.concatenate([xb] * HEADS, axis=0) * mask


def _rg_scan(a, b, h_prev):
    c, w = a.shape
    g = c // SUBLANES
    a3 = a.reshape(g, SUBLANES, w)
    b3 = b.reshape(g, SUBLANES, w)
    sub = lax.broadcasted_iota(jnp.int32, a3.shape, 1)
    shift = 1
    while shift < SUBLANES:
        keep = sub >= shift
        a_sh = pltpu.roll(a3, shift, axis=1)
        b_sh = pltpu.roll(b3, shift, axis=1)
        b3 = jnp.where(keep, a3 * b_sh + b3, b3)
        a3 = jnp.where(keep, a3 * a_sh, a3)
        shift *= 2
    outs = []
    hb = h_prev
    for gi in range(g):
        hg = a3[gi] * hb + b3[gi]
        outs.append(hg)
        hb = jnp.broadcast_to(hg[SUBLANES - 1:SUBLANES, :], (SUBLANES, w))
    return jnp.concatenate(outs, axis=0), hb


def _rows(start, size):
    return pl.ds(start if isinstance(start, int) else pl.multiple_of(start, size), size)


def _update_state(ref, bi, quads, old, scale, mask_ref, upd):
    for rs, cs in quads:
        sc = scale[rs, cs] if scale.shape[0] > 1 else scale[:, cs]
        ref[bi, rs, cs] = sc * old[rs, cs] + mask_ref[rs, cs] * upd[rs, cs]


def _layer_kernel(cfg, *refs):
    layer, depth, bt, tile, c, n_lv, blk, lookahead = cfg
    n = bt * tile
    nb = n // blk
    n_in = 27 + (1 if lookahead else 0)
    ins, outs, scr = refs[:n_in], refs[n_in:n_in + 5], refs[n_in + 5:]
    x_ref = ins[0]
    xn_ref = ins[1] if lookahead else None
    (cq_ref, sq_ref, ck_ref, sk_ref, conv_in, h_in, sret_in, shg_in, win_ref, wa_ref, wx_ref, wout_ref,
     p512_ref, p256_ref, p1024_ref, dall_ref, gq_ref, gk_ref, gs_ref, bdk_ref, bdn_ref, kbm_ref, vbm_ref,
     w3_ref, lvl_ref, e64_ref) = ins[n_in - 26:]
    xo_ref, conv_out, h_out, sret_out, shg_out = outs
    sets = (scr[0:12], scr[12:24])
    seg_rows = blk if bt == 1 else tile
    dn_alpha = (2 * depth) ** 0.25

    @pl.when(pl.program_id(1) == 0)
    def _load_state():
        conv_out[...] = conv_in[...]
        h_out[...] = h_in[...]
        sret_out[...] = sret_in[...]
        shg_out[...] = shg_in[...]

    conv_b = p512_ref[CONV_W:CONV_W + 1, :]
    conv_w = [p512_ref[j:j + 1, :] for j in range(CONV_W)]
    rg_ba = p512_ref[5:6, :]
    rg_bx = p512_ref[6:7, :]
    c_lam = RG_C * jax.nn.log_sigmoid(p512_ref[7:8, :])
    gn_g = p256_ref[0:1, :]
    hg_bf = p256_ref[1:2, :]
    hg_ng = p256_ref[2:3, :]
    logits = [p256_ref[3 + li:4 + li, :] for li in range(depth)]
    mx = functools.reduce(jnp.maximum, logits)
    ex = [jnp.exp(v - mx) for v in logits]
    den = functools.reduce(lambda s, v: s + v, ex)
    probs = [v / den for v in ex]
    lb = functools.reduce(lambda s, v: s + v, probs[:layer + 1]) - probs[0]
    one_m_lb = 1.0 - lb

    def project_steps(xblk, dst):
        xb_s = dst[11]

        def stage():
            xb_s[...] = xblk.astype(BF16)

        def step(gi, lo, hi):
            def run():
                val = _dot(xb_s[...], win_ref[:, SEG[gi] + lo:SEG[gi] + hi])
                if gi == 0:
                    for si in range(blk // seg_rows):
                        dst[0][si, SUBLANES:SUBLANES + seg_rows, lo:hi] = val[si * seg_rows:(si + 1) * seg_rows]
                else:
                    dst[gi][:, lo:hi] = val
            return run

        steps = [stage]
        for gi in range(2 + N_NARROW):
            for lo in range(0, SPLITS[gi], PROJ_COLS):
                steps.append(step(gi, lo, lo + PROJ_COLS))
        return steps

    def mix(src, base, filler=()):
        filler = iter(filler)

        def fill(k=1):
            for _ in range(k):
                nxt = next(filler, None)
                if nxt is not None:
                    nxt()

        rgx_s, rgg_s, q_s, k_s, v_s, rgate_s, hq_s, hf_s, hi_s, hgate_s, y_s = src[:11]
        rgx_s[:, 0:SUBLANES, :] = conv_out[...]
        u3 = conv_b
        for j in range(CONV_W):
            off = SUBLANES - (CONV_W - 1) + j
            u3 = u3 + rgx_s[:, off:off + seg_rows, :] * conv_w[j]
        conv_out[...] = rgx_s[:, seg_rows:seg_rows + SUBLANES, :]
        u = u3.reshape(blk, RG_WIDTH)
        fill(2)
        ub = u.astype(BF16)
        half = RG_WIDTH // 2
        r_pre = jnp.concatenate([_dot(ub[:, :half], wa_ref[0]), _dot(ub[:, half:], wa_ref[1])], axis=1)
        i_pre = jnp.concatenate([_dot(ub[:, :half], wx_ref[0]), _dot(ub[:, half:], wx_ref[1])], axis=1)
        log_a = c_lam * jax.nn.sigmoid(r_pre + rg_ba)
        a = jnp.exp(log_a)
        b_in = jnp.sqrt(-jnp.tanh(log_a) * (a * a + 1.0)) * (jax.nn.sigmoid(i_pre + rg_bx) * u)
        fill(2)
        e64 = e64_ref[...]
        vbm = vbm_ref[...]
        lvl = lvl_ref[...]

        n_ch = blk // c
        st = [dict(rows=slice(ci * c, (ci + 1) * c), bi=0 if bt == 1 else ci,
                   trow=_rows(base + ci * c, c) if bt == 1 else slice(0, c)) for ci in range(n_ch)]

        for d in st:
            rows, trow = d["rows"], d["trow"]
            cq, sq, ck, sk = cq_ref[trow, :], sq_ref[trow, :], ck_ref[trow, :], sk_ref[trow, :]
            q1, q2 = q_s[rows, 0:LANES], q_s[rows, LANES:HW]
            k1, k2 = k_s[rows, 0:LANES], k_s[rows, LANES:HW]
            qr = jnp.concatenate([q1 * cq - q2 * sq, q2 * cq + q1 * sq], axis=1)
            d["kr"] = jnp.concatenate([k1 * ck - k2 * sk, k2 * ck + k1 * sk], axis=1)
            d["v"] = v_s[rows, :]
            d["qb"] = qr.astype(BF16)
            d["scores"] = _dot_t(d["qb"], _stack_heads(d["kr"], kbm_ref[...]))
            z = hf_s[rows, :] + hg_bf
            ez = jnp.exp(-jnp.abs(z))
            inv = 1.0 / (1.0 + ez)
            pos = z >= 0.0
            sig_p = jnp.where(pos, inv, ez * inv)
            sig_n = jnp.where(pos, ez * inv, inv)
            log_f = jnp.log(jnp.maximum(lb + one_m_lb * sig_p, F_EPS))
            d["kc"] = one_m_lb * sig_n
            f_hi = log_f.astype(BF16)
            res = log_f - f_hi.astype(F32)
            f_mid = res.astype(BF16)
            f_lo = (res - f_mid.astype(F32)).astype(BF16)
            d["expo"] = _dot(w3_ref[...], jnp.concatenate([f_hi, f_mid, f_lo], axis=0))
            fill()

        for d in st:
            bi, kr, v, qb = d["bi"], d["kr"], d["v"], d["qb"]
            p = (d["scores"] * dall_ref[...]).astype(BF16)
            s_ret = sret_out[bi]
            d["o_b"] = _dot(p, _stack_heads(v, vbm)) + _dot(qb, s_ret.astype(BF16)) * gq_ref[...]
            upd = _tdot((kr * gk_ref[...]).astype(BF16), v.astype(BF16))
            _update_state(sret_out, bi, QUADS_K, s_ret, gs_ref, bdk_ref, upd)
            fill()
            expo, kc = d["expo"], d["kc"]
            qh = hq_s[d["rows"], :]
            e0 = jnp.exp(expo[2 * c:3 * c])
            s0 = _dot_t((qh * e0).astype(BF16), _stack_heads(kc / e0, vbm))
            pm = jnp.where(lvl == 0.0, s0, 0.0)
            for lv in range(1, n_lv + 1):
                el = jnp.exp(expo[(2 + lv) * c:(3 + lv) * c])
                sl_ = _dot_t((qh * el).astype(BF16), _stack_heads(kc * el, vbm))
                pm = jnp.where(lvl == float(lv), sl_, pm)
            d["pm"], d["qh"] = pm, qh
            fill()

        for d in st:
            bi, expo = d["bi"], d["expo"]
            d["dev"] = d["o_b"] - _seg_mean(d["o_b"], e64)
            fill()
            vh = hi_s[d["rows"], :]
            e_cross = expo[0:c]
            s_hg = shg_out[bi]
            d["o_c"] = (_dot(d["pm"].astype(BF16), _stack_heads(vh, vbm))
                        + _dot_t((d["qh"] * jnp.exp(e_cross)).astype(BF16), s_hg.astype(BF16)))
            k_st = (d["kc"] * jnp.exp(expo[c:2 * c])).astype(BF16)
            dec = jnp.exp(e_cross[c - 1:c, :])
            _update_state(shg_out, bi, QUADS_N, s_hg, dec, bdn_ref, _tdot(vh.astype(BF16), k_st))
            fill()

        for d in st:
            d["var_b"] = _seg_mean(d["dev"] * d["dev"], e64)
            d["ms_c"] = _seg_mean(d["o_c"] * d["o_c"], e64)
            fill()

        for d in st:
            rows, bi = d["rows"], d["bi"]
            h, h_last = _rg_scan(a[rows], b_in[rows], h_out[bi])
            h_out[bi] = h_last
            y_s[rows, 0:RG_WIDTH] = (h * _silu(rgg_s[rows, :])).astype(BF16)
            y_b = d["dev"] * lax.rsqrt(d["var_b"] + LN_EPS) * gn_g
            y_s[rows, RG_WIDTH:RG_WIDTH + HW] = (y_b * _silu(rgate_s[rows, :])).astype(BF16)
            y_c = d["o_c"] * lax.rsqrt(d["ms_c"] + LN_EPS) * hg_ng
            y_s[rows, RG_WIDTH + HW:D_MODEL] = (y_c * _silu(hgate_s[rows, :])).astype(BF16)

        fill(D_IN // PROJ_COLS + 1)
        if bt == 1:
            xrows = _rows(base, blk)
            x_in = x_ref[0, xrows, :]
        else:
            x_in = x_ref[...].reshape(n, D_MODEL)
        xn = dn_alpha * x_in + _dot(y_s[...], wout_ref[...])
        mu = jnp.mean(xn, axis=-1, keepdims=True)
        dv = xn - mu
        var = jnp.mean(dv * dv, axis=-1, keepdims=True)
        out = dv * lax.rsqrt(var + LN_EPS) * p1024_ref[0:1, :] + p1024_ref[1:2, :]
        if bt == 1:
            xo_ref[0, xrows, :] = out
        else:
            xo_ref[...] = out.reshape(bt, tile, D_MODEL)

    def project(xblk, dst):
        for run in project_steps(xblk, dst):
            run()

    if not lookahead:
        project(x_ref[...].reshape(n, D_MODEL), sets[0])
        mix(sets[0], 0)
        return

    @pl.when((pl.program_id(0) == 0) & (pl.program_id(1) == 0))
    def _first_block():
        project(x_ref[0, 0:blk, :], sets[0])

    def pair_body(k, carry):
        base0 = pl.multiple_of(2 * k * blk, 2 * blk)
        mix(sets[0], base0, project_steps(x_ref[0, _rows(base0 + blk, blk), :], sets[1]))
        in_tile = 2 * k + 2 < nb
        nxt = jnp.minimum(2 * k + 2, nb - 1) * blk
        ahead = jnp.where(in_tile, x_ref[0, _rows(nxt, blk), :], xn_ref[0])
        mix(sets[1], base0 + blk, project_steps(ahead, sets[0]))
        return carry

    lax.fori_loop(0, nb // 2, pair_body, 0)


def _tiling(batch, seq):
    chunk = min(MAX_CHUNK, seq)
    tile = min(seq, MAX_TILE)
    bt = 1 if tile >= BLOCK_ROWS else min(batch, BLOCK_ROWS // tile)
    blk = min(BLOCK_ROWS, bt * tile)
    lookahead = bt == 1 and (tile // blk) >= 4
    assert seq % tile == 0 and batch % bt == 0 and tile % chunk == 0 and chunk % 16 == 0 and blk % chunk == 0
    assert lookahead or bt * tile == blk
    assert not lookahead or (tile // blk) % 2 == 0
    return bt, tile, chunk, blk, lookahead


def _const(shape):
    zeros = (0,) * len(shape)
    return pl.BlockSpec(shape, lambda b, i: zeros)


def _layer_call(layer, depth, x, rope, states, weights, tabs, n_lv, tiling):
    bt, tile, chunk, blk, lookahead = tiling
    batch, seq, _ = x.shape
    n_tiles = seq // tile
    grid = (batch // bt, n_tiles)
    n_steps = grid[0] * grid[1]
    kern = functools.partial(_layer_kernel, (layer, depth, bt, tile, chunk, n_lv, blk, lookahead))

    def per_b(shape):
        return pl.BlockSpec((bt,) + shape, lambda b, i: (b,) + (0,) * len(shape))

    def next_block(b, i):
        f = jnp.minimum(b * n_tiles + i + 1, n_steps - 1)
        return (f // n_tiles, (f % n_tiles) * (tile // blk), 0)

    rope_spec = pl.BlockSpec((tile, LANES), lambda b, i: (i, 0))
    x_spec = pl.BlockSpec((bt, tile, D_MODEL), lambda b, i: (b, i, 0))
    state_specs = [per_b((SUBLANES, RG_WIDTH)), per_b((SUBLANES, RG_WIDTH)), per_b((HW, HW)), per_b((HW, HW))]
    tab_args = [tabs[k] for k in _TABLE_ORDER]
    x_args, x_specs = [x], [x_spec]
    if lookahead:
        x_args.append(x)
        x_specs.append(pl.BlockSpec((1, blk, D_MODEL), next_block))
    in_specs = (x_specs + [rope_spec] * 4 + state_specs + [_const(w.shape) for w in weights]
                + [_const(t.shape) for t in tab_args])
    out_shape = [jax.ShapeDtypeStruct(x.shape, F32)] + [jax.ShapeDtypeStruct(s.shape, F32) for s in states]
    out_specs = [x_spec] + state_specs
    seg_rows = blk if bt == 1 else tile
    one_set = ([pltpu.VMEM((blk // seg_rows, SUBLANES + seg_rows, RG_WIDTH), F32), pltpu.VMEM((blk, RG_WIDTH), F32)]
               + [pltpu.VMEM((blk, HW), F32)] * N_NARROW + [pltpu.VMEM((blk, D_MODEL), BF16)] * 2)
    return pl.pallas_call(
        kern, grid=grid, in_specs=in_specs, out_specs=out_specs, out_shape=out_shape, scratch_shapes=one_set * 2,
        compiler_params=pltpu.CompilerParams(dimension_semantics=("arbitrary", "arbitrary"),
                                             vmem_limit_bytes=VMEM_LIMIT_BYTES),
        name=f"layer{layer}_t{tile}",
    )(*x_args, *rope, *states, *weights, *tab_args)


def _half_split(w):
    d = w.shape[0]
    return w.reshape(d, HEADS, 2, HEAD_DIM // 2).transpose(0, 2, 1, 3).reshape(d, HW)


def _block_diag(w):
    eye = jnp.eye(w.shape[0], dtype=w.dtype)
    return jnp.einsum('gij,gk->gikj', w, eye).reshape(w.shape[0] * RG_BLOCK, w.shape[0] * RG_BLOCK)


_KPERM = np.concatenate([np.arange(HEAD_DIM // 2) + h * HEAD_DIM + p * (HEAD_DIM // 2)
                         for p in range(2) for h in range(HEADS)])


def _states_in(conv, h_rg, s_ret, s_hg):
    b = conv.shape[0]
    eye = jnp.eye(HEADS, dtype=F32)
    conv8 = jnp.pad(conv, ((0, 0), (SUBLANES - (CONV_W - 1), 0), (0, 0)))
    h8 = jnp.broadcast_to(h_rg[:, None, :], (b, SUBLANES, RG_WIDTH))
    ret = jnp.einsum('bhkv,hg->bhkgv', s_ret, eye).reshape(b, HW, HW)[:, _KPERM, :]
    hg = jnp.einsum('bhkv,hg->bhvgk', s_hg, eye).reshape(b, HW, HW)
    return conv8, h8, ret, hg


def _states_out(conv8, h8, ret, hg):
    b = conv8.shape[0]
    inv = np.argsort(_KPERM)
    ret5 = ret[:, inv, :].reshape(b, HEADS, HEAD_DIM, HEADS, HEAD_DIM)
    hg5 = hg.reshape(b, HEADS, HEAD_DIM, HEADS, HEAD_DIM)
    s_ret = jnp.stack([ret5[:, h, :, h, :] for h in range(HEADS)], axis=1)
    s_hg = jnp.stack([hg5[:, h, :, h, :].transpose(0, 2, 1) for h in range(HEADS)], axis=1)
    return conv8[:, SUBLANES - (CONV_W - 1):, :], h8[:, 0, :], s_ret, s_hg


def kernel(x_prompt, x_sample, cache_conv, state_rglru, state_ret, state_hgrn, w_in, conv_w, conv_b, rg_wa, rg_ba,
           rg_wx, rg_bx, rg_lambda, ret_gn_g, hg_bf, hg_lb_logits, hg_norm_g, w_out, ln_g, ln_b):
    depth = w_in.shape[0]
    bp, lp, _ = x_prompt.shape
    bs, ls, _ = x_sample.shape

    streams = []
    for batch, seq, pos0 in ((bp, lp, 0), (bs, ls, PAST_LEN)):
        tiling = _tiling(batch, seq)
        tabs_np, n_lv = _mixer_tables(tiling[2])
        tabs = {k: jnp.asarray(v, BF16 if k in _BF16_TABLES else F32) for k, v in tabs_np.items()}
        rope = tuple(jnp.asarray(t, F32) for t in _rope_tables(pos0 + np.arange(seq)))
        streams.append((tiling, n_lv, tabs, rope))

    xp, xs = x_prompt, x_sample
    zeros_p = _states_in(jnp.zeros((bp, CONV_W - 1, RG_WIDTH), F32), jnp.zeros((bp, RG_WIDTH), F32),
                         jnp.zeros((bp, HEADS, HEAD_DIM, HEAD_DIM), F32),
                         jnp.zeros((bp, HEADS, HEAD_DIM, HEAD_DIM), F32))
    outs_p, outs_s = [], []
    for l in range(depth):
        wl = w_in[l]
        win = jnp.concatenate([wl[:, SEG[0]:SEG[2]], _half_split(wl[:, SEG[2]:SEG[3]]),
                               _half_split(wl[:, SEG[3]:SEG[4]]), wl[:, SEG[4]:]], axis=1).astype(BF16)
        nb = RG_BLOCKS // 2
        wa = jnp.stack([_block_diag(rg_wa[l, :nb]), _block_diag(rg_wa[l, nb:])]).astype(BF16)
        wx = jnp.stack([_block_diag(rg_wx[l, :nb]), _block_diag(rg_wx[l, nb:])]).astype(BF16)
        p512 = jnp.concatenate([conv_w[l], conv_b[l][None], rg_ba[l][None], rg_bx[l][None], rg_lambda[l][None]], 0)
        p256 = jnp.concatenate([ret_gn_g[l][None], hg_bf[l][None], hg_norm_g[l][None], hg_lb_logits,
                                jnp.zeros((SUBLANES - 3 - depth, HW), F32)], 0)
        p1024 = jnp.concatenate([ln_g[l][None], ln_b[l][None], jnp.zeros((SUBLANES - 2, D_MODEL), F32)], 0)
        weights = (win, wa, wx, w_out[l].astype(BF16), p512, p256, p1024)

        tiling, n_lv, tabs, rope = streams[0]
        res = _layer_call(l, depth, xp, rope, zeros_p, weights, tabs, n_lv, tiling)
        xp = res[0]
        outs_p.append(_states_out(*res[1:]))

        tiling, n_lv, tabs, rope = streams[1]
        st = _states_in(cache_conv[l], state_rglru[l], state_ret[l], state_hgrn[l])
        res = _layer_call(l, depth, xs, rope, st, weights, tabs, n_lv, tiling)
        xs = res[0]
        outs_s.append(_states_out(*res[1:]))

    stack = lambda outs, k: jnp.stack([o[k] for o in outs])
    return (xp, xs, stack(outs_p, 0), stack(outs_p, 1), stack(outs_p, 2), stack(outs_p, 3),
            stack(outs_s, 0), stack(outs_s, 1), stack(outs_s, 2), stack(outs_s, 3))
```

```python
import functools
import math

import jax
import jax.numpy as jnp
import numpy as np
from jax import lax
from jax.experimental import pallas as pl
from jax.experimental.pallas import tpu as pltpu

F32 = jnp.float32
BF16 = jnp.bfloat16

D_MODEL = 1024
RG_WIDTH = 512
RG_BLOCKS = 8
RG_BLOCK = RG_WIDTH // RG_BLOCKS
CONV_W = 4
RG_C = 8.0
HEADS = 4
HEAD_DIM = 64
HW = HEADS * HEAD_DIM
ROPE_BASE = 10000.0
LN_EPS = 1e-5
F_EPS = 1e-6
PAST_LEN = 1024
SPLITS = (RG_WIDTH, RG_WIDTH, HW, HW, HW, HW, HW, HW, HW, HW)
D_IN = sum(SPLITS)
SEG = tuple(int(v) for v in np.cumsum((0,) + SPLITS))
N_NARROW = 8

SUBLANES = 8
LANES = 128
VMEM_LIMIT_BYTES = 56 * 1024 * 1024

MAX_CHUNK = 64
BLOCK_ROWS = 128
PROJ_COLS = 256
MAX_TILE = 512
LEVEL0_ROWS = SUBLANES

QUADS = ((slice(0, 128), slice(0, 128)), (slice(128, 256), slice(128, 256)))


def _mixer_tables(chunk):
    c = chunk
    t = np.arange(c)
    hid = np.repeat(np.arange(HEADS), HEAD_DIM)
    row_h = np.repeat(np.arange(HEADS), c)
    scale = HEAD_DIM ** -0.5

    log_g = np.log1p(-np.exp2(-5.0 - np.arange(HEADS)))
    rel = t[:, None] - t[None, :]
    dmat = np.where(rel >= 0, np.exp(np.maximum(rel, 0)[None] * log_g[:, None, None]), 0.0)
    bd = (hid[:, None] == hid[None, :]).astype(np.float64)
    tabs = {
        "dall": scale * dmat.transpose(1, 0, 2).reshape(c, HEADS * c),
        "gq": scale * np.repeat(np.exp((t[:, None] + 1.0) * log_g[None, :]), HEAD_DIM, 1),
        "gk": np.exp((c - 1.0 - t)[:, None] * log_g[None, :])[:, hid],
        "gs": np.exp(c * log_g)[hid][:, None] * np.ones((1, HW)),
        "bd": bd,
        "hm": (row_h[:, None] == hid[None, :]).astype(np.float64),
        "e64": bd / HEAD_DIM,
    }

    n_lv = int(round(math.log2(c // LEVEL0_ROWS)))
    j = t[None, :]
    tt = t[:, None]
    w_cross = (j <= tt).astype(np.float64)
    w_state = (j > tt).astype(np.float64)
    r0 = (tt // LEVEL0_ROWS) * LEVEL0_ROWS + LEVEL0_ROWS // 2 - 1
    w_l0 = ((j > r0) & (j <= tt)).astype(np.float64) - ((j > tt) & (j <= r0)).astype(np.float64)
    blocks = [w_cross, w_state, w_l0]
    lvl = np.full((c, c), -1.0)
    causal = t[None, :] <= t[:, None]
    same0 = (t[:, None] // LEVEL0_ROWS) == (t[None, :] // LEVEL0_ROWS)
    lvl[causal & same0] = 0.0
    assigned = same0.copy()
    for lv in range(1, n_lv + 1):
        g = LEVEL0_ROWS * 2 ** lv
        r = (tt // g) * g + g // 2 - 1
        up = ((j > r) & (j <= tt)).astype(np.float64)
        lo = ((j > tt) & (j <= r)).astype(np.float64)
        blocks.append(up + lo)
        same = (t[:, None] // g) == (t[None, :] // g)
        lvl[causal & same & ~assigned] = float(lv)
        assigned |= same
    w = np.concatenate(blocks, axis=0)
    tabs["w3"] = np.concatenate([w, w, w], axis=1)
    tabs["lvl"] = np.tile(lvl, (1, HEADS))
    return tabs, n_lv


_BF16_TABLES = ("hm", "w3", "e64")
_TABLE_ORDER = ("dall", "gq", "gk", "gs", "bd", "hm", "w3", "lvl", "e64")


def _rope_tables(pos):
    half = HEAD_DIM // 2
    inv = ROPE_BASE ** (-np.arange(half, dtype=np.float64) / half)
    ang = pos.astype(np.float64)[:, None] * inv[None, :]
    cos = np.tile(np.cos(ang), (1, 2 * LANES // HEAD_DIM))
    sin = np.tile(np.concatenate([-np.sin(ang), np.sin(ang)], axis=1), (1, LANES // HEAD_DIM))
    return cos, sin


def _silu(x):
    return x * jax.nn.sigmoid(x)


def _dot(a, b):
    return jnp.dot(a, b, preferred_element_type=F32)


def _dot_t(a, b):
    return lax.dot_general(a, b, (((1,), (1,)), ((), ())), preferred_element_type=F32)


def _tdot(a, b):
    return lax.dot_general(a, b, (((0,), (0,)), ((), ())), preferred_element_type=F32)


def _seg_mean(x, e64):
    hi = x.astype(BF16)
    lo = (x - hi.astype(F32)).astype(BF16)
    return _dot(hi, e64) + _dot(lo, e64)


def _stack_heads(x, mask):
    xb = x.astype(BF16)
    return jnp.concatenate([xb] * HEADS, axis=0) * mask


def _rotary(x, cos, sin, first_half):
    partner = jnp.where(first_half, pltpu.roll(x, HW - HEAD_DIM // 2, axis=1), pltpu.roll(x, HEAD_DIM // 2, axis=1))
    return x * cos + partner * sin


def _rg_scan(a, b, h_prev):
    c, w = a.shape
    g = c // SUBLANES
    a3 = a.reshape(g, SUBLANES, w)
    b3 = b.reshape(g, SUBLANES, w)
    sub = lax.broadcasted_iota(jnp.int32, a3.shape, 1)
    shift = 1
    while shift < SUBLANES:
        keep = sub >= shift
        a_sh = pltpu.roll(a3, shift, axis=1)
        b_sh = pltpu.roll(b3, shift, axis=1)
        b3 = jnp.where(keep, a3 * b_sh + b3, b3)
        a3 = jnp.where(keep, a3 * a_sh, a3)
        shift *= 2
    outs = []
    hb = h_prev
    for gi in range(g):
        hg = a3[gi] * hb + b3[gi]
        outs.append(hg)
        hb = jnp.broadcast_to(hg[SUBLANES - 1:SUBLANES, :], (SUBLANES, w))
    return jnp.concatenate(outs, axis=0), hb


def _rows(start, size):
    return pl.ds(start if isinstance(start, int) else pl.multiple_of(start, size), size)


def _update_state(ref, bi, old, scale, mask_ref, upd):
    for rs, cs in QUADS:
        sc = scale[rs, cs] if scale.shape[0] > 1 else scale[:, cs]
        ref[bi, rs, cs] = sc * old[rs, cs] + mask_ref[rs, cs] * upd[rs, cs]


def _head_transpose(s):
    return jnp.concatenate([s[h * HEAD_DIM:(h + 1) * HEAD_DIM, :].T for h in range(HEADS)], axis=0)


def _expand_state(s, bd):
    return jnp.concatenate([s] * HEADS, axis=1) * bd


def _compact_state(s):
    out = s[:, 0:HEAD_DIM]
    for h in range(1, HEADS):
        out = out + s[:, h * HEAD_DIM:(h + 1) * HEAD_DIM]
    return out


def _layer_kernel(cfg, *refs):
    layer, depth, bt, tile, c, n_lv, blk, lookahead, zero_init = cfg
    n = bt * tile
    nb = n // blk
    it = iter(refs)
    x_ref = next(it)
    xn_ref = next(it) if lookahead else None
    cos_ref, sin_ref = next(it), next(it)
    conv_in, h_in, sret_in, shg_in = (None,) * 4 if zero_init else (next(it), next(it), next(it), next(it))
    (win_ref, wg_ref, wout_ref, convw_ref, convb_ref, rgba_ref, rgbx_ref, rglam_ref, gng_ref, hgbf_ref, hgng_ref,
     lbl_ref, lng_ref, lnb_ref) = [next(it) for _ in range(14)]
    dall_ref, gq_ref, gk_ref, gs_ref, bd_ref, hm_ref, w3_ref, lvl_ref, e64_ref = [next(it) for _ in range(9)]
    xo_ref, conv_o, h_o, sret_o, shg_o = [next(it) for _ in range(5)]
    scr = list(it)
    sets = (scr[0:12], scr[12:24])
    conv_c, h_c, sret_c, shg_c = scr[24:28]
    seg_rows = blk if bt == 1 else tile
    dn_alpha = (2 * depth) ** 0.25

    @pl.when(pl.program_id(1) == 0)
    def _load_state():
        if zero_init:
            for ref in (conv_c, h_c, sret_c, shg_c):
                ref[...] = jnp.zeros(ref.shape, F32)
        else:
            conv_c[:, 0:SUBLANES - (CONV_W - 1), :] = jnp.zeros((bt, SUBLANES - (CONV_W - 1), RG_WIDTH), F32)
            conv_c[:, SUBLANES - (CONV_W - 1):SUBLANES, :] = conv_in[...]
            h_c[...] = jnp.broadcast_to(h_in[...], (bt, SUBLANES, RG_WIDTH))
            for bi in range(bt):
                sret_c[bi] = _expand_state(sret_in[bi], bd_ref[...])
                shg_c[bi] = _expand_state(_head_transpose(shg_in[bi]), bd_ref[...])

    conv_b = convb_ref[...]
    conv_w = [convw_ref[j:j + 1, :] for j in range(CONV_W)]
    rg_ba = rgba_ref[...]
    rg_bx = rgbx_ref[...]
    c_lam = RG_C * jax.nn.log_sigmoid(rglam_ref[...])
    gn_g = gng_ref[...]
    hg_bf = hgbf_ref[...]
    hg_ng = hgng_ref[...]
    logits = [lbl_ref[li:li + 1, :] for li in range(depth)]
    mx = functools.reduce(jnp.maximum, logits)
    ex = [jnp.exp(v - mx) for v in logits]
    den = functools.reduce(lambda s, v: s + v, ex)
    probs = [v / den for v in ex]
    lb = functools.reduce(lambda s, v: s + v, probs[:layer + 1]) - probs[0]
    one_m_lb = 1.0 - lb

    def project_steps(xblk, dst):
        xb_s = dst[11]

        def stage():
            xb_s[...] = xblk.astype(BF16)

        def step(gi, lo, hi):
            def run():
                val = _dot(xb_s[...], win_ref[:, SEG[gi] + lo:SEG[gi] + hi])
                if gi == 0:
                    for si in range(blk // seg_rows):
                        dst[0][si, SUBLANES:SUBLANES + seg_rows, lo:hi] = val[si * seg_rows:(si + 1) * seg_rows]
                else:
                    dst[gi][:, lo:hi] = val
            return run

        steps = [stage]
        for gi in range(2 + N_NARROW):
            for lo in range(0, SPLITS[gi], PROJ_COLS):
                steps.append(step(gi, lo, lo + PROJ_COLS))
        return steps

    def mix(src, base, filler=()):
        filler = iter(filler)

        def fill(k=1):
            for _ in range(k):
                nxt = next(filler, None)
                if nxt is not None:
                    nxt()

        rgx_s, rgg_s, q_s, k_s, v_s, rgate_s, hq_s, hf_s, hi_s, hgate_s, y_s = src[:11]
        rgx_s[:, 0:SUBLANES, :] = conv_c[...]
        u3 = conv_b
        for j in range(CONV_W):
            off = SUBLANES - (CONV_W - 1) + j
            u3 = u3 + rgx_s[:, off:off + seg_rows, :] * conv_w[j]
        conv_c[...] = rgx_s[:, seg_rows:seg_rows + SUBLANES, :]
        u = u3.reshape(blk, RG_WIDTH)
        fill(2)
        ub = u.astype(BF16)
        half = RG_WIDTH // 2
        r_pre = jnp.concatenate([_dot(ub[:, :half], wg_ref[0, 0]), _dot(ub[:, half:], wg_ref[0, 1])], axis=1)
        i_pre = jnp.concatenate([_dot(ub[:, :half], wg_ref[1, 0]), _dot(ub[:, half:], wg_ref[1, 1])], axis=1)
        log_a = c_lam * jax.nn.sigmoid(r_pre + rg_ba)
        a = jnp.exp(log_a)
        b_in = jnp.sqrt(-jnp.tanh(log_a) * (a * a + 1.0)) * (jax.nn.sigmoid(i_pre + rg_bx) * u)
        fill(2)
        e64 = e64_ref[...]
        hm = hm_ref[...]
        lvl = lvl_ref[...]
        first_half = (lax.broadcasted_iota(jnp.int32, (c, HW), 1) % HEAD_DIM) < HEAD_DIM // 2

        n_ch = blk // c
        st = [dict(rows=slice(ci * c, (ci + 1) * c), bi=0 if bt == 1 else ci,
                   trow=_rows(base + ci * c, c) if bt == 1 else slice(0, c)) for ci in range(n_ch)]

        for d in st:
            rows, trow = d["rows"], d["trow"]
            cos = jnp.concatenate([cos_ref[trow, :]] * (HW // LANES), axis=1)
            sin = jnp.concatenate([sin_ref[trow, :]] * (HW // LANES), axis=1)
            d["kr"] = _rotary(k_s[rows, :], cos, sin, first_half)
            d["v"] = v_s[rows, :]
            d["qb"] = _rotary(q_s[rows, :], cos, sin, first_half).astype(BF16)
            d["scores"] = _dot_t(d["qb"], _stack_heads(d["kr"], hm))
            z = hf_s[rows, :] + hg_bf
            ez = jnp.exp(-jnp.abs(z))
            inv = 1.0 / (1.0 + ez)
            pos = z >= 0.0
            sig_p = jnp.where(pos, inv, ez * inv)
            sig_n = jnp.where(pos, ez * inv, inv)
            log_f = jnp.log(jnp.maximum(lb + one_m_lb * sig_p, F_EPS))
            d["kc"] = one_m_lb * sig_n
            f_hi = log_f.astype(BF16)
            res = log_f - f_hi.astype(F32)
            f_mid = res.astype(BF16)
            f_lo = (res - f_mid.astype(F32)).astype(BF16)
            d["expo"] = _dot(w3_ref[...], jnp.concatenate([f_hi, f_mid, f_lo], axis=0))
            fill()

        for d in st:
            bi, kr, v, qb = d["bi"], d["kr"], d["v"], d["qb"]
            p = (d["scores"] * dall_ref[...]).astype(BF16)
            s_ret = sret_c[bi]
            d["o_b"] = _dot(p, _stack_heads(v, hm)) + _dot(qb, s_ret.astype(BF16)) * gq_ref[...]
            upd = _tdot((kr * gk_ref[...]).astype(BF16), v.astype(BF16))
            _update_state(sret_c, bi, s_ret, gs_ref, bd_ref, upd)
            fill()
            expo, kc = d["expo"], d["kc"]
            qh = hq_s[d["rows"], :]
            e0 = jnp.exp(expo[2 * c:3 * c])
            s0 = _dot_t((qh * e0).astype(BF16), _stack_heads(kc / e0, hm))
            pm = jnp.where(lvl == 0.0, s0, 0.0)
            for lv in range(1, n_lv + 1):
                el = jnp.exp(expo[(2 + lv) * c:(3 + lv) * c])
                sl_ = _dot_t((qh * el).astype(BF16), _stack_heads(kc * el, hm))
                pm = jnp.where(lvl == float(lv), sl_, pm)
            d["pm"], d["qh"] = pm, qh
            fill()

        for d in st:
            bi, expo = d["bi"], d["expo"]
            d["dev"] = d["o_b"] - _seg_mean(d["o_b"], e64)
            fill()
            vh = hi_s[d["rows"], :]
            e_cross = expo[0:c]
            s_hg = shg_c[bi]
            d["o_c"] = (_dot(d["pm"].astype(BF16), _stack_heads(vh, hm))
                        + _dot_t((d["qh"] * jnp.exp(e_cross)).astype(BF16), s_hg.astype(BF16)))
            k_st = (d["kc"] * jnp.exp(expo[c:2 * c])).astype(BF16)
            dec = jnp.exp(e_cross[c - 1:c, :])
            _update_state(shg_c, bi, s_hg, dec, bd_ref, _tdot(vh.astype(BF16), k_st))
            fill()

        for d in st:
            d["var_b"] = _seg_mean(d["dev"] * d["dev"], e64)
            d["ms_c"] = _seg_mean(d["o_c"] * d["o_c"], e64)
            fill()

        for d in st:
            rows, bi = d["rows"], d["bi"]
            h, h_last = _rg_scan(a[rows], b_in[rows], h_c[bi])
            h_c[bi] = h_last
            y_s[rows, 0:RG_WIDTH] = (h * _silu(rgg_s[rows, :])).astype(BF16)
            y_b = d["dev"] * lax.rsqrt(d["var_b"] + LN_EPS) * gn_g
            y_s[rows, RG_WIDTH:RG_WIDTH + HW] = (y_b * _silu(rgate_s[rows, :])).astype(BF16)
            y_c = d["o_c"] * lax.rsqrt(d["ms_c"] + LN_EPS) * hg_ng
            y_s[rows, RG_WIDTH + HW:D_MODEL] = (y_c * _silu(hgate_s[rows, :])).astype(BF16)

        fill(D_IN // PROJ_COLS + 1)
        if bt == 1:
            xrows = _rows(base, blk)
            x_in = x_ref[0, xrows, :]
        else:
            x_in = x_ref[...].reshape(n, D_MODEL)
        xn = dn_alpha * x_in + _dot(y_s[...], wout_ref[...])
        mu = jnp.mean(xn, axis=-1, keepdims=True)
        dv = xn - mu
        var = jnp.mean(dv * dv, axis=-1, keepdims=True)
        out = dv * lax.rsqrt(var + LN_EPS) * lng_ref[...] + lnb_ref[...]
        if bt == 1:
            xo_ref[0, xrows, :] = out
        else:
            xo_ref[...] = out.reshape(bt, tile, D_MODEL)

    def project(xblk, dst):
        for run in project_steps(xblk, dst):
            run()

    if not lookahead:
        project(x_ref[...].reshape(n, D_MODEL), sets[0])
        mix(sets[0], 0)
    else:
        @pl.when((pl.program_id(0) == 0) & (pl.program_id(1) == 0))
        def _first_block():
            project(x_ref[0, 0:blk, :], sets[0])

        def pair_body(k, carry):
            base0 = pl.multiple_of(2 * k * blk, 2 * blk)
            mix(sets[0], base0, project_steps(x_ref[0, _rows(base0 + blk, blk), :], sets[1]))
            in_tile = 2 * k + 2 < nb
            nxt = jnp.minimum(2 * k + 2, nb - 1) * blk
            ahead = jnp.where(in_tile, x_ref[0, _rows(nxt, blk), :], xn_ref[0])
            mix(sets[1], base0 + blk, project_steps(ahead, sets[0]))
            return carry

        lax.fori_loop(0, nb // 2, pair_body, 0)

    @pl.when(pl.program_id(1) == pl.num_programs(1) - 1)
    def _store_state():
        conv_o[...] = conv_c[:, SUBLANES - (CONV_W - 1):SUBLANES, :]
        h_o[...] = h_c[:, 0:1, :]
        for bi in range(bt):
            sret_o[bi] = _compact_state(sret_c[bi])
            shg_o[bi] = _head_transpose(_compact_state(shg_c[bi]))


def _tiling(batch, seq):
    chunk = min(MAX_CHUNK, seq)
    tile = min(seq, MAX_TILE)
    bt = 1 if tile >= BLOCK_ROWS else min(batch, BLOCK_ROWS // tile)
    blk = min(BLOCK_ROWS, bt * tile)
    lookahead = bt == 1 and (tile // blk) >= 4
    assert seq % tile == 0 and batch % bt == 0 and tile % chunk == 0 and chunk % 16 == 0 and blk % chunk == 0
    assert lookahead or bt * tile == blk
    assert not lookahead or (tile // blk) % 2 == 0
    return bt, tile, chunk, blk, lookahead


def _layer_call(layer, depth, x, rope, states, params, tabs, n_lv, tiling):
    bt, tile, chunk, blk, lookahead = tiling
    batch, seq, _ = x.shape
    n_tiles = seq // tile
    grid = (batch // bt, n_tiles)
    n_steps = grid[0] * grid[1]
    zero_init = states is None
    kern = functools.partial(_layer_kernel, (layer, depth, bt, tile, chunk, n_lv, blk, lookahead, zero_init))

    def layer_const(arr):
        tail = arr.shape[1:]
        return pl.BlockSpec((None,) + tail, lambda b, i: (layer,) + (0,) * len(tail))

    def const(arr):
        zeros = (0,) * arr.ndim
        return pl.BlockSpec(arr.shape, lambda b, i: zeros)

    def per_b(tail):
        return pl.BlockSpec((bt,) + tail, lambda b, i: (b,) + (0,) * len(tail))

    def per_lb(tail):
        return pl.BlockSpec((None, bt) + tail, lambda b, i: (layer, b) + (0,) * len(tail))

    def next_block(b, i):
        f = jnp.minimum(b * n_tiles + i + 1, n_steps - 1)
        return (f // n_tiles, (f % n_tiles) * (tile // blk), 0)

    state_tails = ((CONV_W - 1, RG_WIDTH), (1, RG_WIDTH), (HW, HEAD_DIM), (HW, HEAD_DIM))
    x_spec = pl.BlockSpec((bt, tile, D_MODEL), lambda b, i: (b, i, 0))
    rope_spec = pl.BlockSpec((tile, LANES), lambda b, i: (i, 0))
    args, specs = [x], [x_spec]
    if lookahead:
        args.append(x)
        specs.append(pl.BlockSpec((1, blk, D_MODEL), next_block))
    args += list(rope)
    specs += [rope_spec] * 2
    if not zero_init:
        args += list(states)
        specs += [per_lb(t) for t in state_tails]
    win, wg, wout, convw, convb, rgba, rgbx, rglam, gng, hgbf, hgng, lbl, lng, lnb = params
    args += list(params)
    wg_spec = pl.BlockSpec((2, None) + wg.shape[2:], lambda b, i: (0, layer, 0, 0, 0))
    specs += ([layer_const(win), wg_spec] + [layer_const(p) for p in (wout, convw, convb, rgba, rgbx, rglam, gng,
                                                                      hgbf, hgng)]
              + [const(lbl), layer_const(lng), layer_const(lnb)])
    tab_args = [tabs[k] for k in _TABLE_ORDER]
    args += tab_args
    specs += [const(t) for t in tab_args]

    out_shape = [jax.ShapeDtypeStruct(x.shape, F32)] + [jax.ShapeDtypeStruct((batch,) + t, F32) for t in state_tails]
    out_specs = [x_spec] + [per_b(t) for t in state_tails]
    seg_rows = blk if bt == 1 else tile
    one_set = ([pltpu.VMEM((blk // seg_rows, SUBLANES + seg_rows, RG_WIDTH), F32), pltpu.VMEM((blk, RG_WIDTH), F32)]
               + [pltpu.VMEM((blk, HW), F32)] * N_NARROW + [pltpu.VMEM((blk, D_MODEL), BF16)] * 2)
    carries = [pltpu.VMEM((bt, SUBLANES, RG_WIDTH), F32)] * 2 + [pltpu.VMEM((bt, HW, HW), F32)] * 2
    return pl.pallas_call(
        kern, grid=grid, in_specs=specs, out_specs=out_specs, out_shape=out_shape,
        scratch_shapes=one_set * 2 + carries,
        compiler_params=pltpu.CompilerParams(dimension_semantics=("arbitrary", "arbitrary"),
                                             vmem_limit_bytes=VMEM_LIMIT_BYTES),
        name=f"layer{layer}_t{tile}",
    )(*args)


def _gate_blocks(w):
    depth = w.shape[0]
    per_half = RG_BLOCKS // 2
    eye = jnp.eye(per_half, dtype=w.dtype)
    w5 = w.reshape(depth, 2, per_half, RG_BLOCK, RG_BLOCK)
    return jnp.einsum('lhgij,gk->lhgikj', w5, eye).reshape(depth, 2, per_half * RG_BLOCK, per_half * RG_BLOCK)


def kernel(x_prompt, x_sample, cache_conv, state_rglru, state_ret, state_hgrn, w_in, conv_w, conv_b, rg_wa, rg_ba,
           rg_wx, rg_bx, rg_lambda, ret_gn_g, hg_bf, hg_lb_logits, hg_norm_g, w_out, ln_g, ln_b):
    depth = w_in.shape[0]
    bp, lp, _ = x_prompt.shape
    bs, ls, _ = x_sample.shape

    streams = []
    for batch, seq, pos0 in ((bp, lp, 0), (bs, ls, PAST_LEN)):
        tiling = _tiling(batch, seq)
        tabs_np, n_lv = _mixer_tables(tiling[2])
        tabs = {k: jnp.asarray(v, BF16 if k in _BF16_TABLES else F32) for k, v in tabs_np.items()}
        rope = tuple(jnp.asarray(t, F32) for t in _rope_tables(pos0 + np.arange(seq)))
        streams.append((tiling, n_lv, tabs, rope))

    row = lambda p: p[:, None, :]
    params = (w_in.astype(BF16), jnp.stack([_gate_blocks(rg_wa), _gate_blocks(rg_wx)]).astype(BF16),
              w_out.astype(BF16), conv_w, row(conv_b), row(rg_ba), row(rg_bx), row(rg_lambda), row(ret_gn_g),
              row(hg_bf), row(hg_norm_g), hg_lb_logits, row(ln_g), row(ln_b))
    sample_states = (cache_conv, state_rglru[:, :, None, :], state_ret.reshape(depth, bs, HW, HEAD_DIM),
                     state_hgrn.reshape(depth, bs, HW, HEAD_DIM))

    xp, xs = x_prompt, x_sample
    outs_p, outs_s = [], []
    for l in range(depth):
        tiling, n_lv, tabs, rope = streams[0]
        res = _layer_call(l, depth, xp, rope, None, params, tabs, n_lv, tiling)
        xp = res[0]
        outs_p.append(res[1:])
        tiling, n_lv, tabs, rope = streams[1]
        res = _layer_call(l, depth, xs, rope, sample_states, params, tabs, n_lv, tiling)
        xs = res[0]
        outs_s.append(res[1:])

    def collect(outs, batch):
        conv = jnp.stack([o[0] for o in outs])
        h = jnp.stack([o[1][:, 0, :] for o in outs])
        ret = jnp.stack([o[2] for o in outs]).reshape(depth, batch, HEADS, HEAD_DIM, HEAD_DIM)
        hg = jnp.stack([o[3] for o in outs]).reshape(depth, batch, HEADS, HEAD_DIM, HEAD_DIM)
        return conv, h, ret, hg

    return (xp, xs) + collect(outs_p, bp) + collect(outs_s, bs)
```

```python
import functools
import math

import jax
import jax.numpy as jnp
import numpy as np
from jax import lax
from jax.experimental import pallas as pl
from jax.experimental.pallas import tpu as pltpu

F32 = jnp.float32
BF16 = jnp.bfloat16

D_MODEL = 1024
RG_WIDTH = 512
RG_BLOCKS = 8
RG_BLOCK = RG_WIDTH // RG_BLOCKS
CONV_W = 4
RG_C = 8.0
HEADS = 4
HEAD_DIM = 64
HW = HEADS * HEAD_DIM
ROPE_BASE = 10000.0
LN_EPS = 1e-5
F_EPS = 1e-6
PAST_LEN = 1024
SPLITS = (RG_WIDTH, RG_WIDTH, HW, HW, HW, HW, HW, HW, HW, HW)
D_IN = sum(SPLITS)
SEG = tuple(int(v) for v in np.cumsum((0,) + SPLITS))
N_NARROW = 8

SUBLANES = 8
LANES = 128
VMEM_LIMIT_BYTES = 56 * 1024 * 1024

MAX_CHUNK = 64
BLOCK_ROWS = 256
PROJ_COLS = 512
MAX_TILE = 512
LEVEL0_ROWS = SUBLANES

QUADS = ((slice(0, 128), slice(0, 128)), (slice(128, 256), slice(128, 256)))


def _mixer_tables(chunk):
    c = chunk
    t = np.arange(c)
    hid = np.repeat(np.arange(HEADS), HEAD_DIM)
    row_h = np.repeat(np.arange(HEADS), c)
    scale = HEAD_DIM ** -0.5

    log_g = np.log1p(-np.exp2(-5.0 - np.arange(HEADS)))
    rel = t[:, None] - t[None, :]
    dmat = np.where(rel >= 0, np.exp(np.maximum(rel, 0)[None] * log_g[:, None, None]), 0.0)
    bd = (hid[:, None] == hid[None, :]).astype(np.float64)
    tabs = {
        "dall": scale * dmat.transpose(1, 0, 2).reshape(c, HEADS * c),
        "gq": scale * np.repeat(np.exp((t[:, None] + 1.0) * log_g[None, :]), HEAD_DIM, 1),
        "gk": np.exp((c - 1.0 - t)[:, None] * log_g[None, :])[:, hid],
        "gs": np.exp(c * log_g)[hid][:, None] * np.ones((1, HW)),
        "bd": bd,
        "hm": (row_h[:, None] == hid[None, :]).astype(np.float64),
        "e64": bd / HEAD_DIM,
    }

    n_lv = int(round(math.log2(c // LEVEL0_ROWS)))
    j = t[None, :]
    tt = t[:, None]
    w_cross = (j <= tt).astype(np.float64)
    w_state = (j > tt).astype(np.float64)
    r0 = (tt // LEVEL0_ROWS) * LEVEL0_ROWS + LEVEL0_ROWS // 2 - 1
    w_l0 = ((j > r0) & (j <= tt)).astype(np.float64) - ((j > tt) & (j <= r0)).astype(np.float64)
    blocks = [w_cross, w_state, w_l0]
    lvl = np.full((c, c), -1.0)
    causal = t[None, :] <= t[:, None]
    same0 = (t[:, None] // LEVEL0_ROWS) == (t[None, :] // LEVEL0_ROWS)
    lvl[causal & same0] = 0.0
    assigned = same0.copy()
    for lv in range(1, n_lv + 1):
        g = LEVEL0_ROWS * 2 ** lv
        r = (tt // g) * g + g // 2 - 1
        up = ((j > r) & (j <= tt)).astype(np.float64)
        lo = ((j > tt) & (j <= r)).astype(np.float64)
        blocks.append(up + lo)
        same = (t[:, None] // g) == (t[None, :] // g)
        lvl[causal & same & ~assigned] = float(lv)
        assigned |= same
    w = np.concatenate(blocks, axis=0)
    tabs["w3"] = np.concatenate([w, w, w], axis=1)
    tabs["lvl"] = np.tile(lvl, (1, HEADS))
    return tabs, n_lv


_BF16_TABLES = ("hm", "w3", "e64")
_TABLE_ORDER = ("dall", "gq", "gk", "gs", "bd", "hm", "w3", "lvl", "e64")


def _rope_tables(pos):
    half = HEAD_DIM // 2
    inv = ROPE_BASE ** (-np.arange(half, dtype=np.float64) / half)
    ang = pos.astype(np.float64)[:, None] * inv[None, :]
    cos = np.tile(np.cos(ang), (1, 2 * LANES // HEAD_DIM))
    sin = np.tile(np.concatenate([-np.sin(ang), np.sin(ang)], axis=1), (1, LANES // HEAD_DIM))
    return cos, sin


def _silu(x):
    return x * jax.nn.sigmoid(x)


def _dot(a, b):
    return jnp.dot(a, b, preferred_element_type=F32)


def _dot_t(a, b):
    return lax.dot_general(a, b, (((1,), (1,)), ((), ())), preferred_element_type=F32)


def _tdot(a, b):
    return lax.dot_general(a, b, (((0,), (0,)), ((), ())), preferred_element_type=F32)


def _seg_mean(x, e64):
    hi = x.astype(BF16)
    lo = (x - hi.astype(F32)).astype(BF16)
    return _dot(hi, e64) + _dot(lo, e64)


def _seg_means(xs, e64):
    rows = xs[0].shape[0]
    x = xs[0] if len(xs) == 1 else jnp.concatenate(xs, axis=0)
    hi = x.astype(BF16)
    lo = (x - hi.astype(F32)).astype(BF16)
    r = _dot(jnp.concatenate([hi, lo], axis=0), e64)
    m = r[:x.shape[0]] + r[x.shape[0]:]
    return [m[i * rows:(i + 1) * rows] for i in range(len(xs))]


def _stack_heads(x, mask):
    xb = x.astype(BF16)
    return jnp.concatenate([xb] * HEADS, axis=0) * mask


def _rotary(x, cos, sin, first_half):
    partner = jnp.where(first_half, pltpu.roll(x, HW - HEAD_DIM // 2, axis=1), pltpu.roll(x, HEAD_DIM // 2, axis=1))
    return x * cos + partner * sin


def _rg_scan(a, b, h_prev):
    c, w = a.shape
    g = c // SUBLANES
    a3 = a.reshape(g, SUBLANES, w)
    b3 = b.reshape(g, SUBLANES, w)
    sub = lax.broadcasted_iota(jnp.int32, a3.shape, 1)
    shift = 1
    while shift < SUBLANES:
        keep = sub >= shift
        a_sh = pltpu.roll(a3, shift, axis=1)
        b_sh = pltpu.roll(b3, shift, axis=1)
        b3 = jnp.where(keep, a3 * b_sh + b3, b3)
        a3 = jnp.where(keep, a3 * a_sh, a3)
        shift *= 2
    outs = []
    hb = h_prev
    for gi in range(g):
        hg = a3[gi] * hb + b3[gi]
        outs.append(hg)
        hb = jnp.broadcast_to(hg[SUBLANES - 1:SUBLANES, :], (SUBLANES, w))
    return jnp.concatenate(outs, axis=0), hb


def _rows(start, size):
    return pl.ds(start if isinstance(start, int) else pl.multiple_of(start, size), size)


def _update_state(ref, bi, old, scale, mask_ref, upd):
    for rs, cs in QUADS:
        sc = scale[rs, cs] if scale.shape[0] > 1 else scale[:, cs]
        ref[bi, rs, cs] = sc * old[rs, cs] + mask_ref[rs, cs] * upd[rs, cs]


def _head_transpose(s):
    return jnp.concatenate([s[h * HEAD_DIM:(h + 1) * HEAD_DIM, :].T for h in range(HEADS)], axis=0)


def _expand_state(s, bd):
    return jnp.concatenate([s] * HEADS, axis=1) * bd


def _compact_state(s):
    out = s[:, 0:HEAD_DIM]
    for h in range(1, HEADS):
        out = out + s[:, h * HEAD_DIM:(h + 1) * HEAD_DIM]
    return out


def _layer_kernel(cfg, *refs):
    layer, depth, bt, tile, c, n_lv, blk, lookahead, zero_init = cfg
    n = bt * tile
    nb = n // blk
    it = iter(refs)
    x_ref = next(it)
    xn_ref = next(it) if lookahead else None
    cos_ref, sin_ref = next(it), next(it)
    conv_in, h_in, sret_in, shg_in = (None,) * 4 if zero_init else (next(it), next(it), next(it), next(it))
    (win_ref, wg_ref, wout_ref, convw_ref, convb_ref, rgba_ref, rgbx_ref, rglam_ref, gng_ref, hgbf_ref, hgng_ref,
     lbl_ref, lng_ref, lnb_ref) = [next(it) for _ in range(14)]
    dall_ref, gq_ref, gk_ref, gs_ref, bd_ref, hm_ref, w3_ref, lvl_ref, e64_ref = [next(it) for _ in range(9)]
    xo_ref, conv_o, h_o, sret_o, shg_o = [next(it) for _ in range(5)]
    scr = list(it)
    sets = (scr[0:12], scr[12:24])
    conv_c, h_c, sret_c, shg_c = scr[24:28]
    seg_rows = blk if bt == 1 else tile
    dn_alpha = (2 * depth) ** 0.25

    @pl.when(pl.program_id(1) == 0)
    def _load_state():
        if zero_init:
            for ref in (conv_c, h_c, sret_c, shg_c):
                ref[...] = jnp.zeros(ref.shape, F32)
        else:
            conv_c[:, 0:SUBLANES - (CONV_W - 1), :] = jnp.zeros((bt, SUBLANES - (CONV_W - 1), RG_WIDTH), F32)
            conv_c[:, SUBLANES - (CONV_W - 1):SUBLANES, :] = conv_in[...]
            h_c[...] = jnp.broadcast_to(h_in[...], (bt, SUBLANES, RG_WIDTH))
            for bi in range(bt):
                sret_c[bi] = _expand_state(sret_in[bi], bd_ref[...])
                shg_c[bi] = _expand_state(_head_transpose(shg_in[bi]), bd_ref[...])

    conv_b = convb_ref[...]
    conv_w = [convw_ref[j:j + 1, :] for j in range(CONV_W)]
    rg_ba = rgba_ref[...]
    rg_bx = rgbx_ref[...]
    c_lam = RG_C * jax.nn.log_sigmoid(rglam_ref[...])
    gn_g = gng_ref[...]
    hg_bf = hgbf_ref[...]
    hg_ng = hgng_ref[...]
    logits = [lbl_ref[li:li + 1, :] for li in range(depth)]
    mx = functools.reduce(jnp.maximum, logits)
    ex = [jnp.exp(v - mx) for v in logits]
    den = functools.reduce(lambda s, v: s + v, ex)
    probs = [v / den for v in ex]
    lb = functools.reduce(lambda s, v: s + v, probs[:layer + 1]) - probs[0]
    one_m_lb = 1.0 - lb

    def project_steps(xblk, dst):
        xb_s = dst[11]

        def stage():
            xb_s[...] = xblk.astype(BF16)

        def step(lo):
            def run():
                val = _dot(xb_s[...], win_ref[:, lo:lo + PROJ_COLS])
                for gi in range(2 + N_NARROW):
                    g_lo, g_hi = max(SEG[gi], lo), min(SEG[gi + 1], lo + PROJ_COLS)
                    if g_lo >= g_hi:
                        continue
                    part = val[:, g_lo - lo:g_hi - lo]
                    if gi == 0:
                        for si in range(blk // seg_rows):
                            dst[0][si, SUBLANES:SUBLANES + seg_rows, g_lo:g_hi] = part[si * seg_rows:(si + 1) * seg_rows]
                    else:
                        dst[gi][:, g_lo - SEG[gi]:g_hi - SEG[gi]] = part
            return run

        return [stage] + [step(lo) for lo in range(0, D_IN, PROJ_COLS)]

    def mix(src, base, filler=()):
        filler = list(filler)
        n_ch = blk // c
        fill_points = 4 + 4 * n_ch
        issued = [0, 0]

        def fill():
            issued[0] += 1
            due = -(-len(filler) * issued[0] // fill_points)
            while issued[1] < min(due, len(filler)):
                filler[issued[1]]()
                issued[1] += 1

        rgx_s, rgg_s, q_s, k_s, v_s, rgate_s, hq_s, hf_s, hi_s, hgate_s = src[:10]
        rgx_s[:, 0:SUBLANES, :] = conv_c[...]
        us = []
        for si in range(blk // seg_rows):
            win = rgx_s[si]
            u_seg = conv_b + win[SUBLANES:, :] * conv_w[CONV_W - 1]
            for back in range(1, CONV_W):
                u_seg = u_seg + pltpu.roll(win, back, axis=0)[SUBLANES:, :] * conv_w[CONV_W - 1 - back]
            us.append(u_seg)
        conv_c[...] = rgx_s[:, seg_rows:seg_rows + SUBLANES, :]
        u = us[0] if len(us) == 1 else jnp.concatenate(us, axis=0)
        fill()
        ub = u.astype(BF16)
        half = RG_WIDTH // 2
        r_pre = jnp.concatenate([_dot(ub[:, :half], wg_ref[0, 0]), _dot(ub[:, half:], wg_ref[0, 1])], axis=1)
        i_pre = jnp.concatenate([_dot(ub[:, :half], wg_ref[1, 0]), _dot(ub[:, half:], wg_ref[1, 1])], axis=1)
        log_a = c_lam * jax.nn.sigmoid(r_pre + rg_ba)
        a = jnp.exp(log_a)
        b_in = jnp.sqrt(-jnp.tanh(log_a) * (a * a + 1.0)) * (jax.nn.sigmoid(i_pre + rg_bx) * u)
        fill()
        e64 = e64_ref[...]
        hm = hm_ref[...]
        lvl = lvl_ref[...]
        first_half = (lax.broadcasted_iota(jnp.int32, (c, HW), 1) % HEAD_DIM) < HEAD_DIM // 2

        per_seq = seg_rows // c
        st = [dict(rows=slice(ci * c, (ci + 1) * c), bi=ci // per_seq,
                   trow=_rows(base + (ci % per_seq) * c, c)) for ci in range(n_ch)]

        for d in st:
            rows, trow = d["rows"], d["trow"]
            cos = jnp.concatenate([cos_ref[trow, :]] * (HW // LANES), axis=1)
            sin = jnp.concatenate([sin_ref[trow, :]] * (HW // LANES), axis=1)
            d["kr"] = _rotary(k_s[rows, :], cos, sin, first_half)
            d["v"] = v_s[rows, :]
            d["qb"] = _rotary(q_s[rows, :], cos, sin, first_half).astype(BF16)
            d["scores"] = _dot_t(d["qb"], _stack_heads(d["kr"], hm))
            z = hf_s[rows, :] + hg_bf
            ez = jnp.exp(-jnp.abs(z))
            inv = 1.0 / (1.0 + ez)
            pos = z >= 0.0
            sig_p = jnp.where(pos, inv, ez * inv)
            sig_n = jnp.where(pos, ez * inv, inv)
            log_f = jnp.log(jnp.maximum(lb + one_m_lb * sig_p, F_EPS))
            d["kc"] = one_m_lb * sig_n
            f_hi = log_f.astype(BF16)
            res = log_f - f_hi.astype(F32)
            f_mid = res.astype(BF16)
            f_lo = (res - f_mid.astype(F32)).astype(BF16)
            d["expo"] = _dot(w3_ref[...], jnp.concatenate([f_hi, f_mid, f_lo], axis=0))
            fill()

        for d in st:
            bi, kr, v, qb = d["bi"], d["kr"], d["v"], d["qb"]
            p = (d["scores"] * dall_ref[...]).astype(BF16)
            s_ret = sret_c[bi]
            d["o_b"] = _dot(p, _stack_heads(v, hm)) + _dot(qb, s_ret.astype(BF16)) * gq_ref[...]
            upd = _tdot((kr * gk_ref[...]).astype(BF16), v.astype(BF16))
            _update_state(sret_c, bi, s_ret, gs_ref, bd_ref, upd)
            fill()
            expo, kc = d["expo"], d["kc"]
            qh = hq_s[d["rows"], :]
            e0 = jnp.exp(expo[2 * c:3 * c])
            s0 = _dot_t((qh * e0).astype(BF16), _stack_heads(kc / e0, hm))
            pm = jnp.where(lvl == 0.0, s0, 0.0)
            for lv in range(1, n_lv + 1):
                el = jnp.exp(expo[(2 + lv) * c:(3 + lv) * c])
                sl_ = _dot_t((qh * el).astype(BF16), _stack_heads(kc * el, hm))
                pm = jnp.where(lvl == float(lv), sl_, pm)
            d["pm"], d["qh"] = pm, qh
            fill()

        means = _seg_means([d["o_b"] for d in st], e64)
        for d, m in zip(st, means):
            d["dev"] = d["o_b"] - m
        fill()
        for d in st:
            bi, expo = d["bi"], d["expo"]
            vh = hi_s[d["rows"], :]
            e_cross = expo[0:c]
            s_hg = shg_c[bi]
            d["o_c"] = (_dot(d["pm"].astype(BF16), _stack_heads(vh, hm))
                        + _dot_t((d["qh"] * jnp.exp(e_cross)).astype(BF16), s_hg.astype(BF16)))
            k_st = (d["kc"] * jnp.exp(expo[c:2 * c])).astype(BF16)
            dec = jnp.exp(e_cross[c - 1:c, :])
            _update_state(shg_c, bi, s_hg, dec, bd_ref, _tdot(vh.astype(BF16), k_st))
            fill()

        stats = _seg_means([d["dev"] * d["dev"] for d in st] + [d["o_c"] * d["o_c"] for d in st], e64)
        fill()
        assert issued[0] == fill_points

        ys = []
        for ci, d in enumerate(st):
            rows, bi = d["rows"], d["bi"]
            h, h_last = _rg_scan(a[rows], b_in[rows], h_c[bi])
            h_c[bi] = h_last
            y_a = h * _silu(rgg_s[rows, :])
            y_b = d["dev"] * lax.rsqrt(stats[ci] + LN_EPS) * gn_g * _silu(rgate_s[rows, :])
            y_c = d["o_c"] * lax.rsqrt(stats[n_ch + ci] + LN_EPS) * hg_ng * _silu(hgate_s[rows, :])
            ys.append(jnp.concatenate([y_a.astype(BF16), y_b.astype(BF16), y_c.astype(BF16)], axis=1))
        y = ys[0] if n_ch == 1 else jnp.concatenate(ys, axis=0)

        if bt == 1:
            xrows = _rows(base, blk)
            x_in = x_ref[0, xrows, :]
        else:
            x_in = x_ref[...].reshape(n, D_MODEL)
        xn = dn_alpha * x_in + _dot(y, wout_ref[...])
        mu = jnp.mean(xn, axis=-1, keepdims=True)
        dv = xn - mu
        var = jnp.mean(dv * dv, axis=-1, keepdims=True)
        out = dv * lax.rsqrt(var + LN_EPS) * lng_ref[...] + lnb_ref[...]
        if bt == 1:
            xo_ref[0, xrows, :] = out
        else:
            xo_ref[...] = out.reshape(bt, tile, D_MODEL)

    def project(xblk, dst):
        for run in project_steps(xblk, dst):
            run()

    if not lookahead:
        project(x_ref[...].reshape(n, D_MODEL), sets[0])
        mix(sets[0], 0)
    else:
        @pl.when((pl.program_id(0) == 0) & (pl.program_id(1) == 0))
        def _first_block():
            project(x_ref[0, 0:blk, :], sets[0])

        def pair_body(k, carry):
            base0 = pl.multiple_of(2 * k * blk, 2 * blk)
            mix(sets[0], base0, project_steps(x_ref[0, _rows(base0 + blk, blk), :], sets[1]))
            in_tile = 2 * k + 2 < nb
            nxt = jnp.minimum(2 * k + 2, nb - 1) * blk
            ahead = jnp.where(in_tile, x_ref[0, _rows(nxt, blk), :], xn_ref[0])
            mix(sets[1], base0 + blk, project_steps(ahead, sets[0]))
            return carry

        lax.fori_loop(0, nb // 2, pair_body, 0)

    @pl.when(pl.program_id(1) == pl.num_programs(1) - 1)
    def _store_state():
        conv_o[...] = conv_c[:, SUBLANES - (CONV_W - 1):SUBLANES, :]
        h_o[...] = h_c[:, 0:1, :]
        for bi in range(bt):
            sret_o[bi] = _compact_state(sret_c[bi])
            shg_o[bi] = _head_transpose(_compact_state(shg_c[bi]))


def _tiling(batch, seq):
    chunk = min(MAX_CHUNK, seq)
    tile = min(seq, MAX_TILE)
    bt = 1 if tile >= BLOCK_ROWS else min(batch, BLOCK_ROWS // tile)
    blk = min(BLOCK_ROWS, bt * tile)
    lookahead = bt == 1 and (tile // blk) >= 2
    assert seq % tile == 0 and batch % bt == 0 and tile % chunk == 0 and chunk % 16 == 0 and blk % chunk == 0
    assert lookahead or bt * tile == blk
    assert not lookahead or (tile // blk) % 2 == 0
    return bt, tile, chunk, blk, lookahead


def _layer_call(layer, depth, x, rope, states, params, tabs, n_lv, tiling):
    bt, tile, chunk, blk, lookahead = tiling
    batch, seq, _ = x.shape
    n_tiles = seq // tile
    grid = (batch // bt, n_tiles)
    n_steps = grid[0] * grid[1]
    zero_init = states is None
    kern = functools.partial(_layer_kernel, (layer, depth, bt, tile, chunk, n_lv, blk, lookahead, zero_init))

    def layer_const(arr):
        tail = arr.shape[1:]
        return pl.BlockSpec((None,) + tail, lambda b, i: (layer,) + (0,) * len(tail))

    def const(arr):
        zeros = (0,) * arr.ndim
        return pl.BlockSpec(arr.shape, lambda b, i: zeros)

    def per_b(tail):
        return pl.BlockSpec((bt,) + tail, lambda b, i: (b,) + (0,) * len(tail))

    def per_lb(tail):
        return pl.BlockSpec((None, bt) + tail, lambda b, i: (layer, b) + (0,) * len(tail))

    def next_block(b, i):
        f = jnp.minimum(b * n_tiles + i + 1, n_steps - 1)
        return (f // n_tiles, (f % n_tiles) * (tile // blk), 0)

    state_tails = ((CONV_W - 1, RG_WIDTH), (1, RG_WIDTH), (HW, HEAD_DIM), (HW, HEAD_DIM))
    x_spec = pl.BlockSpec((bt, tile, D_MODEL), lambda b, i: (b, i, 0))
    rope_spec = pl.BlockSpec((tile, LANES), lambda b, i: (i, 0))
    args, specs = [x], [x_spec]
    if lookahead:
        args.append(x)
        specs.append(pl.BlockSpec((1, blk, D_MODEL), next_block))
    args += list(rope)
    specs += [rope_spec] * 2
    if not zero_init:
        args += list(states)
        specs += [per_lb(t) for t in state_tails]
    win, wg, wout, convw, convb, rgba, rgbx, rglam, gng, hgbf, hgng, lbl, lng, lnb = params
    args += list(params)
    wg_spec = pl.BlockSpec((2, None) + wg.shape[2:], lambda b, i: (0, layer, 0, 0, 0))
    specs += ([layer_const(win), wg_spec] + [layer_const(p) for p in (wout, convw, convb, rgba, rgbx, rglam, gng,
                                                                      hgbf, hgng)]
              + [const(lbl), layer_const(lng), layer_const(lnb)])
    tab_args = [tabs[k] for k in _TABLE_ORDER]
    args += tab_args
    specs += [const(t) for t in tab_args]

    out_shape = [jax.ShapeDtypeStruct(x.shape, F32)] + [jax.ShapeDtypeStruct((batch,) + t, F32) for t in state_tails]
    out_specs = [x_spec] + [per_b(t) for t in state_tails]
    seg_rows = blk if bt == 1 else tile
    one_set = ([pltpu.VMEM((blk // seg_rows, SUBLANES + seg_rows, RG_WIDTH), F32), pltpu.VMEM((blk, RG_WIDTH), F32)]
               + [pltpu.VMEM((blk, HW), F32)] * N_NARROW + [pltpu.VMEM((blk, D_MODEL), BF16)] * 2)
    carries = [pltpu.VMEM((bt, SUBLANES, RG_WIDTH), F32)] * 2 + [pltpu.VMEM((bt, HW, HW), F32)] * 2
    return pl.pallas_call(
        kern, grid=grid, in_specs=specs, out_specs=out_specs, out_shape=out_shape,
        scratch_shapes=one_set * 2 + carries,
        compiler_params=pltpu.CompilerParams(dimension_semantics=("arbitrary", "arbitrary"),
                                             vmem_limit_bytes=VMEM_LIMIT_BYTES),
        name=f"layer{layer}_t{tile}",
    )(*args)


def _gate_blocks(w):
    depth = w.shape[0]
    per_half = RG_BLOCKS // 2
    eye = jnp.eye(per_half, dtype=w.dtype)
    w5 = w.reshape(depth, 2, per_half, RG_BLOCK, RG_BLOCK)
    return jnp.einsum('lhgij,gk->lhgikj', w5, eye).reshape(depth, 2, per_half * RG_BLOCK, per_half * RG_BLOCK)


def kernel(x_prompt, x_sample, cache_conv, state_rglru, state_ret, state_hgrn, w_in, conv_w, conv_b, rg_wa, rg_ba,
           rg_wx, rg_bx, rg_lambda, ret_gn_g, hg_bf, hg_lb_logits, hg_norm_g, w_out, ln_g, ln_b):
    depth = w_in.shape[0]
    bp, lp, _ = x_prompt.shape
    bs, ls, _ = x_sample.shape

    streams = []
    for batch, seq, pos0 in ((bp, lp, 0), (bs, ls, PAST_LEN)):
        tiling = _tiling(batch, seq)
        tabs_np, n_lv = _mixer_tables(tiling[2])
        tabs = {k: jnp.asarray(v, BF16 if k in _BF16_TABLES else F32) for k, v in tabs_np.items()}
        rope = tuple(jnp.asarray(t, F32) for t in _rope_tables(pos0 + np.arange(seq)))
        streams.append((tiling, n_lv, tabs, rope))

    row = lambda p: p[:, None, :]
    params = (w_in.astype(BF16), jnp.stack([_gate_blocks(rg_wa), _gate_blocks(rg_wx)]).astype(BF16),
              w_out.astype(BF16), conv_w, row(conv_b), row(rg_ba), row(rg_bx), row(rg_lambda), row(ret_gn_g),
              row(hg_bf), row(hg_norm_g), hg_lb_logits, row(ln_g), row(ln_b))
    sample_states = (cache_conv, state_rglru[:, :, None, :], state_ret.reshape(depth, bs, HW, HEAD_DIM),
                     state_hgrn.reshape(depth, bs, HW, HEAD_DIM))

    xp, xs = x_prompt, x_sample
    outs_p, outs_s = [], []
    for l in range(depth):
        tiling, n_lv, tabs, rope = streams[0]
        res = _layer_call(l, depth, xp, rope, None, params, tabs, n_lv, tiling)
        xp = res[0]
        outs_p.append(res[1:])
        tiling, n_lv, tabs, rope = streams[1]
        res = _layer_call(l, depth, xs, rope, sample_states, params, tabs, n_lv, tiling)
        xs = res[0]
        outs_s.append(res[1:])

    def collect(outs, batch):
        conv = jnp.stack([o[0] for o in outs])
        h = jnp.stack([o[1][:, 0, :] for o in outs])
        ret = jnp.stack([o[2] for o in outs]).reshape(depth, batch, HEADS, HEAD_DIM, HEAD_DIM)
        hg = jnp.stack([o[3] for o in outs]).reshape(depth, batch, HEADS, HEAD_DIM, HEAD_DIM)
        return conv, h, ret, hg

    return (xp, xs) + collect(outs_p, bp) + collect(outs_s, bs)
```

```python
import functools
import math

import jax
import jax.numpy as jnp
import numpy as np
from jax import lax
from jax.experimental import pallas as pl
from jax.experimental.pallas import tpu as pltpu

F32 = jnp.float32
BF16 = jnp.bfloat16

D_MODEL = 1024
RG_WIDTH = 512
RG_BLOCKS = 8
RG_BLOCK = RG_WIDTH // RG_BLOCKS
CONV_W = 4
RG_C = 8.0
HEADS = 4
HEAD_DIM = 64
HW = HEADS * HEAD_DIM
ROPE_BASE = 10000.0
LN_EPS = 1e-5
F_EPS = 1e-6
PAST_LEN = 1024
SPLITS = (RG_WIDTH, RG_WIDTH, HW, HW, HW, HW, HW, HW, HW, HW)
D_IN = sum(SPLITS)
SEG = tuple(int(v) for v in np.cumsum((0,) + SPLITS))
N_NARROW = 8

SUBLANES = 8
LANES = 128
VMEM_LIMIT_BYTES = 56 * 1024 * 1024

MAX_CHUNK = 64
BLOCK_ROWS = 256
PROJ_COLS = 256
MAX_TILE = 512
LEVEL0_ROWS = SUBLANES

QUADS = ((slice(0, 128), slice(0, 128)), (slice(128, 256), slice(128, 256)))


def _mixer_tables(chunk):
    c = chunk
    t = np.arange(c)
    hid = np.repeat(np.arange(HEADS), HEAD_DIM)
    row_h = np.repeat(np.arange(HEADS), c)
    scale = HEAD_DIM ** -0.5

    log_g = np.log1p(-np.exp2(-5.0 - np.arange(HEADS)))
    rel = t[:, None] - t[None, :]
    dmat = np.where(rel >= 0, np.exp(np.maximum(rel, 0)[None] * log_g[:, None, None]), 0.0)
    bd = (hid[:, None] == hid[None, :]).astype(np.float64)
    tabs = {
        "dall": scale * dmat.transpose(1, 0, 2).reshape(c, HEADS * c),
        "gq": scale * np.repeat(np.exp((t[:, None] + 1.0) * log_g[None, :]), HEAD_DIM, 1),
        "gk": np.exp((c - 1.0 - t)[:, None] * log_g[None, :])[:, hid],
        "gs": np.exp(c * log_g)[hid][:, None] * np.ones((1, HW)),
        "bd": bd,
        "hm": (row_h[:, None] == hid[None, :]).astype(np.float64),
        "e64": bd / HEAD_DIM,
    }

    n_lv = int(round(math.log2(c // LEVEL0_ROWS)))
    j = t[None, :]
    tt = t[:, None]
    w_cross = (j <= tt).astype(np.float64)
    w_state = (j > tt).astype(np.float64)
    r0 = (tt // LEVEL0_ROWS) * LEVEL0_ROWS + LEVEL0_ROWS // 2 - 1
    w_l0 = ((j > r0) & (j <= tt)).astype(np.float64) - ((j > tt) & (j <= r0)).astype(np.float64)
    blocks = [w_cross, w_state, w_l0]
    lvl = np.full((c, c), -1.0)
    causal = t[None, :] <= t[:, None]
    same0 = (t[:, None] // LEVEL0_ROWS) == (t[None, :] // LEVEL0_ROWS)
    lvl[causal & same0] = 0.0
    assigned = same0.copy()
    for lv in range(1, n_lv + 1):
        g = LEVEL0_ROWS * 2 ** lv
        r = (tt // g) * g + g // 2 - 1
        up = ((j > r) & (j <= tt)).astype(np.float64)
        lo = ((j > tt) & (j <= r)).astype(np.float64)
        blocks.append(up + lo)
        same = (t[:, None] // g) == (t[None, :] // g)
        lvl[causal & same & ~assigned] = float(lv)
        assigned |= same
    w = np.concatenate(blocks, axis=0)
    tabs["w3"] = np.concatenate([w, w, w], axis=1)
    tabs["lvl"] = np.tile(lvl, (1, HEADS))
    return tabs, n_lv


_BF16_TABLES = ("hm", "w3", "e64")
_TABLE_ORDER = ("dall", "gq", "gk", "gs", "bd", "hm", "w3", "lvl", "e64")


def _rope_tables(pos):
    half = HEAD_DIM // 2
    inv = ROPE_BASE ** (-np.arange(half, dtype=np.float64) / half)
    ang = pos.astype(np.float64)[:, None] * inv[None, :]
    cos = np.tile(np.cos(ang), (1, 2 * LANES // HEAD_DIM))
    sin = np.tile(np.concatenate([-np.sin(ang), np.sin(ang)], axis=1), (1, LANES // HEAD_DIM))
    return cos, sin


def _silu(x):
    return x * jax.nn.sigmoid(x)


def _dot(a, b):
    return jnp.dot(a, b, preferred_element_type=F32)


def _dot_t(a, b):
    return lax.dot_general(a, b, (((1,), (1,)), ((), ())), preferred_element_type=F32)


def _tdot(a, b):
    return lax.dot_general(a, b, (((0,), (0,)), ((), ())), preferred_element_type=F32)


def _seg_mean(x, e64):
    hi = x.astype(BF16)
    lo = (x - hi.astype(F32)).astype(BF16)
    return _dot(hi, e64) + _dot(lo, e64)


def _seg_means(xs, e64):
    rows = xs[0].shape[0]
    x = xs[0] if len(xs) == 1 else jnp.concatenate(xs, axis=0)
    hi = x.astype(BF16)
    lo = (x - hi.astype(F32)).astype(BF16)
    r = _dot(jnp.concatenate([hi, lo], axis=0), e64)
    m = r[:x.shape[0]] + r[x.shape[0]:]
    return [m[i * rows:(i + 1) * rows] for i in range(len(xs))]


def _stack_heads(x, mask):
    xb = x.astype(BF16)
    return jnp.concatenate([xb] * HEADS, axis=0) * mask


def _rotary(x, cos, sin, first_half):
    partner = jnp.where(first_half, pltpu.roll(x, HW - HEAD_DIM // 2, axis=1), pltpu.roll(x, HEAD_DIM // 2, axis=1))
    return x * cos + partner * sin


def _rg_scan(a, b, h_prev):
    c, w = a.shape
    g = c // SUBLANES
    a3 = a.reshape(g, SUBLANES, w)
    b3 = b.reshape(g, SUBLANES, w)
    sub = lax.broadcasted_iota(jnp.int32, a3.shape, 1)
    shift = 1
    while shift < SUBLANES:
        keep = sub >= shift
        a_sh = pltpu.roll(a3, shift, axis=1)
        b_sh = pltpu.roll(b3, shift, axis=1)
        b3 = jnp.where(keep, a3 * b_sh + b3, b3)
        a3 = jnp.where(keep, a3 * a_sh, a3)
        shift *= 2
    outs = []
    hb = h_prev
    for gi in range(g):
        hg = a3[gi] * hb + b3[gi]
        outs.append(hg)
        hb = jnp.broadcast_to(hg[SUBLANES - 1:SUBLANES, :], (SUBLANES, w))
    return jnp.concatenate(outs, axis=0), hb


def _rows(start, size):
    return pl.ds(start if isinstance(start, int) else pl.multiple_of(start, size), size)


def _update_state(ref, bi, old, scale, mask_ref, row_op, col_op):
    for rs, cs in QUADS:
        sc = scale[rs, cs] if scale.shape[0] > 1 else scale[:, cs]
        ref[bi, rs, cs] = sc * old[rs, cs] + mask_ref[rs, cs] * _tdot(row_op[:, rs], col_op[:, cs])


def _head_transpose(s):
    return jnp.concatenate([s[h * HEAD_DIM:(h + 1) * HEAD_DIM, :].T for h in range(HEADS)], axis=0)


def _expand_state(s, bd):
    return jnp.concatenate([s] * HEADS, axis=1) * bd


def _compact_state(s):
    out = s[:, 0:HEAD_DIM]
    for h in range(1, HEADS):
        out = out + s[:, h * HEAD_DIM:(h + 1) * HEAD_DIM]
    return out


def _layer_kernel(cfg, *refs):
    layer, depth, bt, tile, c, n_lv, blk, lookahead, zero_init = cfg
    n = bt * tile
    nb = n // blk
    it = iter(refs)
    x_ref = next(it)
    xn_ref = next(it) if lookahead else None
    cos_ref, sin_ref = next(it), next(it)
    conv_in, h_in, sret_in, shg_in = (None,) * 4 if zero_init else (next(it), next(it), next(it), next(it))
    (win_ref, wg_ref, wout_ref, convw_ref, convb_ref, rgba_ref, rgbx_ref, rglam_ref, gng_ref, hgbf_ref, hgng_ref,
     lbl_ref, lng_ref, lnb_ref) = [next(it) for _ in range(14)]
    dall_ref, gq_ref, gk_ref, gs_ref, bd_ref, hm_ref, w3_ref, lvl_ref, e64_ref = [next(it) for _ in range(9)]
    xo_ref, conv_o, h_o, sret_o, shg_o = [next(it) for _ in range(5)]
    scr = list(it)
    sets = (scr[0:12], scr[12:24])
    conv_c, h_c, sret_c, shg_c = scr[24:28]
    seg_rows = blk if bt == 1 else tile
    dn_alpha = (2 * depth) ** 0.25

    @pl.when(pl.program_id(1) == 0)
    def _load_state():
        if zero_init:
            for ref in (conv_c, h_c, sret_c, shg_c):
                ref[...] = jnp.zeros(ref.shape, F32)
        else:
            conv_c[:, 0:SUBLANES - (CONV_W - 1), :] = jnp.zeros((bt, SUBLANES - (CONV_W - 1), RG_WIDTH), F32)
            conv_c[:, SUBLANES - (CONV_W - 1):SUBLANES, :] = conv_in[...]
            h_c[...] = jnp.broadcast_to(h_in[...], (bt, SUBLANES, RG_WIDTH))
            for bi in range(bt):
                sret_c[bi] = _expand_state(sret_in[bi], bd_ref[...])
                shg_c[bi] = _expand_state(_head_transpose(shg_in[bi]), bd_ref[...])

    conv_b = convb_ref[...]
    conv_w = [convw_ref[j:j + 1, :] for j in range(CONV_W)]
    rg_ba = rgba_ref[...]
    rg_bx = rgbx_ref[...]
    c_lam = RG_C * jax.nn.log_sigmoid(rglam_ref[...])
    gn_g = gng_ref[...]
    hg_bf = hgbf_ref[...]
    hg_ng = hgng_ref[...]
    logits = [lbl_ref[li:li + 1, :] for li in range(depth)]
    mx = functools.reduce(jnp.maximum, logits)
    ex = [jnp.exp(v - mx) for v in logits]
    den = functools.reduce(lambda s, v: s + v, ex)
    probs = [v / den for v in ex]
    lb = functools.reduce(lambda s, v: s + v, probs[:layer + 1]) - probs[0]
    one_m_lb = 1.0 - lb

    def project_steps(xblk, dst):
        xb_s = dst[11]

        def stage():
            xb_s[...] = xblk.astype(BF16)

        def step(lo):
            def run():
                val = _dot(xb_s[...], win_ref[:, lo:lo + PROJ_COLS])
                for gi in range(2 + N_NARROW):
                    g_lo, g_hi = max(SEG[gi], lo), min(SEG[gi + 1], lo + PROJ_COLS)
                    if g_lo >= g_hi:
                        continue
                    part = val[:, g_lo - lo:g_hi - lo]
                    if gi == 0:
                        for si in range(blk // seg_rows):
                            dst[0][si, SUBLANES:SUBLANES + seg_rows, g_lo:g_hi] = part[si * seg_rows:(si + 1) * seg_rows]
                    else:
                        dst[gi][:, g_lo - SEG[gi]:g_hi - SEG[gi]] = part
            return run

        return [stage] + [step(lo) for lo in range(0, D_IN, PROJ_COLS)]

    def mix(src, base, filler=()):
        filler = list(filler)
        n_ch = blk // c
        fill_total = 4.0 + 4 * n_ch
        issued = [0.0, 0]

        def fill(weight):
            issued[0] += weight
            due = math.ceil(len(filler) * min(issued[0] / fill_total, 1.0) - 1e-9)
            while issued[1] < due:
                filler[issued[1]]()
                issued[1] += 1

        rgx_s, rgg_s, q_s, k_s, v_s, rgate_s, hq_s, hf_s, hi_s, hgate_s = src[:10]
        rgx_s[:, 0:SUBLANES, :] = conv_c[...]
        us = []
        for si in range(blk // seg_rows):
            win = rgx_s[si]
            u_seg = conv_b + win[SUBLANES:, :] * conv_w[CONV_W - 1]
            for back in range(1, CONV_W):
                u_seg = u_seg + pltpu.roll(win, back, axis=0)[SUBLANES:, :] * conv_w[CONV_W - 1 - back]
            us.append(u_seg)
        conv_c[...] = rgx_s[:, seg_rows:seg_rows + SUBLANES, :]
        u = us[0] if len(us) == 1 else jnp.concatenate(us, axis=0)
        fill(1.0)
        ub = u.astype(BF16)
        half = RG_WIDTH // 2
        r_pre = jnp.concatenate([_dot(ub[:, :half], wg_ref[0, 0]), _dot(ub[:, half:], wg_ref[0, 1])], axis=1)
        i_pre = jnp.concatenate([_dot(ub[:, :half], wg_ref[1, 0]), _dot(ub[:, half:], wg_ref[1, 1])], axis=1)
        log_a = c_lam * jax.nn.sigmoid(r_pre + rg_ba)
        a = jnp.exp(log_a)
        b_in = jnp.sqrt(-jnp.tanh(log_a) * (a * a + 1.0)) * (jax.nn.sigmoid(i_pre + rg_bx) * u)
        fill(1.0)
        e64 = e64_ref[...]
        hm = hm_ref[...]
        lvl = lvl_ref[...]
        first_half = (lax.broadcasted_iota(jnp.int32, (c, HW), 1) % HEAD_DIM) < HEAD_DIM // 2

        per_seq = seg_rows // c
        st = [dict(rows=slice(ci * c, (ci + 1) * c), bi=ci // per_seq,
                   trow=_rows(base + (ci % per_seq) * c, c)) for ci in range(n_ch)]

        for d in st:
            rows, trow = d["rows"], d["trow"]
            cos = jnp.concatenate([cos_ref[trow, :]] * (HW // LANES), axis=1)
            sin = jnp.concatenate([sin_ref[trow, :]] * (HW // LANES), axis=1)
            d["kr"] = _rotary(k_s[rows, :], cos, sin, first_half)
            d["qb"] = _rotary(q_s[rows, :], cos, sin, first_half).astype(BF16)
            d["scores"] = _dot_t(d["qb"], _stack_heads(d["kr"], hm))
            z = hf_s[rows, :] + hg_bf
            ez = jnp.exp(-jnp.abs(z))
            inv = 1.0 / (1.0 + ez)
            pos = z >= 0.0
            sig_p = jnp.where(pos, inv, ez * inv)
            sig_n = jnp.where(pos, ez * inv, inv)
            log_f = jnp.log(jnp.maximum(lb + one_m_lb * sig_p, F_EPS))
            d["kc"] = one_m_lb * sig_n
            f_hi = log_f.astype(BF16)
            res = log_f - f_hi.astype(F32)
            f_mid = res.astype(BF16)
            f_lo = (res - f_mid.astype(F32)).astype(BF16)
            d["f3"] = jnp.concatenate([f_hi, f_mid, f_lo], axis=0)
            fill(1.0)

        for d in st:
            d["e_lv"] = _dot(w3_ref[2 * c:, :], d["f3"])
        for d in st:
            bi, kr, qb = d["bi"], d["kr"], d["qb"]
            v = v_s[d["rows"], :]
            p = (d["scores"] * dall_ref[...]).astype(BF16)
            s_ret = sret_c[bi]
            d["o_b"] = _dot(p, _stack_heads(v, hm)) + _dot(qb, s_ret.astype(BF16)) * gq_ref[...]
            _update_state(sret_c, bi, s_ret, gs_ref, bd_ref, (kr * gk_ref[...]).astype(BF16), v.astype(BF16))
            fill(1.0)
            expo, kc = d["e_lv"], d["kc"]
            qh = hq_s[d["rows"], :]
            e0 = jnp.exp(expo[0:c])
            s0 = _dot_t((qh * e0).astype(BF16), _stack_heads(kc / e0, hm))
            pm = jnp.where(lvl == 0.0, s0, 0.0)
            for lv in range(1, n_lv + 1):
                el = jnp.exp(expo[lv * c:(lv + 1) * c])
                sl_ = _dot_t((qh * el).astype(BF16), _stack_heads(kc * el, hm))
                pm = jnp.where(lvl == float(lv), sl_, pm)
            d["pm"] = pm
            fill(1.0)

        means = _seg_means([d["o_b"] for d in st], e64)
        for d, m in zip(st, means):
            d["dev"] = d["o_b"] - m
        fill(1.0)
        for d in st:
            d["e_cs"] = _dot(w3_ref[0:2 * c, :], d["f3"])
        for d in st:
            bi, expo = d["bi"], d["e_cs"]
            vh = hi_s[d["rows"], :]
            qh = hq_s[d["rows"], :]
            e_cross = expo[0:c]
            s_hg = shg_c[bi]
            d["o_c"] = (_dot(d["pm"].astype(BF16), _stack_heads(vh, hm))
                        + _dot_t((qh * jnp.exp(e_cross)).astype(BF16), s_hg.astype(BF16)))
            k_st = (d["kc"] * jnp.exp(expo[c:2 * c])).astype(BF16)
            dec = jnp.exp(e_cross[c - 1:c, :])
            _update_state(shg_c, bi, s_hg, dec, bd_ref, vh.astype(BF16), k_st)
            fill(1.0)

        stats = _seg_means([d["dev"] * d["dev"] for d in st] + [d["o_c"] * d["o_c"] for d in st], e64)
        fill(1.0)

        ys = []
        for ci, d in enumerate(st):
            rows, bi = d["rows"], d["bi"]
            h, h_last = _rg_scan(a[rows], b_in[rows], h_c[bi])
            h_c[bi] = h_last
            y_a = h * _silu(rgg_s[rows, :])
            y_b = d["dev"] * lax.rsqrt(stats[ci] + LN_EPS) * gn_g * _silu(rgate_s[rows, :])
            y_c = d["o_c"] * lax.rsqrt(stats[n_ch + ci] + LN_EPS) * hg_ng * _silu(hgate_s[rows, :])
            ys.append(jnp.concatenate([y_a.astype(BF16), y_b.astype(BF16), y_c.astype(BF16)], axis=1))
        y = ys[0] if n_ch == 1 else jnp.concatenate(ys, axis=0)

        assert abs(issued[0] - fill_total) < 1e-6 and issued[1] == len(filler)

        if bt == 1:
            xrows = _rows(base, blk)
            x_in = x_ref[0, xrows, :]
        else:
            x_in = x_ref[...].reshape(n, D_MODEL)
        xn = dn_alpha * x_in + _dot(y, wout_ref[...])
        mu = jnp.mean(xn, axis=-1, keepdims=True)
        dv = xn - mu
        var = jnp.mean(dv * dv, axis=-1, keepdims=True)
        out = dv * lax.rsqrt(var + LN_EPS) * lng_ref[...] + lnb_ref[...]
        if bt == 1:
            xo_ref[0, xrows, :] = out
        else:
            xo_ref[...] = out.reshape(bt, tile, D_MODEL)

    def project(xblk, dst):
        for run in project_steps(xblk, dst):
            run()

    if not lookahead:
        project(x_ref[...].reshape(n, D_MODEL), sets[0])
        mix(sets[0], 0)
    else:
        @pl.when((pl.program_id(0) == 0) & (pl.program_id(1) == 0))
        def _first_block():
            project(x_ref[0, 0:blk, :], sets[0])

        def pair_body(k, carry):
            base0 = pl.multiple_of(2 * k * blk, 2 * blk)
            mix(sets[0], base0, project_steps(x_ref[0, _rows(base0 + blk, blk), :], sets[1]))
            in_tile = 2 * k + 2 < nb
            nxt = jnp.minimum(2 * k + 2, nb - 1) * blk
            ahead = jnp.where(in_tile, x_ref[0, _rows(nxt, blk), :], xn_ref[0])
            mix(sets[1], base0 + blk, project_steps(ahead, sets[0]))
            return carry

        lax.fori_loop(0, nb // 2, pair_body, 0)

    @pl.when(pl.program_id(1) == pl.num_programs(1) - 1)
    def _store_state():
        conv_o[...] = conv_c[:, SUBLANES - (CONV_W - 1):SUBLANES, :]
        h_o[...] = h_c[:, 0:1, :]
        for bi in range(bt):
            sret_o[bi] = _compact_state(sret_c[bi])
            shg_o[bi] = _head_transpose(_compact_state(shg_c[bi]))


def _tiling(batch, seq):
    chunk = min(MAX_CHUNK, seq)
    tile = min(seq, MAX_TILE)
    bt = 1 if tile >= BLOCK_ROWS else min(batch, BLOCK_ROWS // tile)
    blk = min(BLOCK_ROWS, bt * tile)
    lookahead = bt == 1 and (tile // blk) >= 2
    assert seq % tile == 0 and batch % bt == 0 and tile % chunk == 0 and chunk % 16 == 0 and blk % chunk == 0
    assert lookahead or bt * tile == blk
    assert not lookahead or (tile // blk) % 2 == 0
    return bt, tile, chunk, blk, lookahead


def _layer_call(layer, depth, x, rope, states, params, tabs, n_lv, tiling):
    bt, tile, chunk, blk, lookahead = tiling
    batch, seq, _ = x.shape
    n_tiles = seq // tile
    grid = (batch // bt, n_tiles)
    n_steps = grid[0] * grid[1]
    zero_init = states is None
    kern = functools.partial(_layer_kernel, (layer, depth, bt, tile, chunk, n_lv, blk, lookahead, zero_init))

    def layer_const(arr):
        tail = arr.shape[1:]
        return pl.BlockSpec((None,) + tail, lambda b, i: (layer,) + (0,) * len(tail))

    def const(arr):
        zeros = (0,) * arr.ndim
        return pl.BlockSpec(arr.shape, lambda b, i: zeros)

    def per_b(tail):
        return pl.BlockSpec((bt,) + tail, lambda b, i: (b,) + (0,) * len(tail))

    def per_lb(tail):
        return pl.BlockSpec((None, bt) + tail, lambda b, i: (layer, b) + (0,) * len(tail))

    def next_block(b, i):
        f = jnp.minimum(b * n_tiles + i + 1, n_steps - 1)
        return (f // n_tiles, (f % n_tiles) * (tile // blk), 0)

    state_tails = ((CONV_W - 1, RG_WIDTH), (1, RG_WIDTH), (HW, HEAD_DIM), (HW, HEAD_DIM))
    x_spec = pl.BlockSpec((bt, tile, D_MODEL), lambda b, i: (b, i, 0))
    rope_spec = pl.BlockSpec((tile, LANES), lambda b, i: (i, 0))
    args, specs = [x], [x_spec]
    if lookahead:
        args.append(x)
        specs.append(pl.BlockSpec((1, blk, D_MODEL), next_block))
    args += list(rope)
    specs += [rope_spec] * 2
    if not zero_init:
        args += list(states)
        specs += [per_lb(t) for t in state_tails]
    win, wg, wout, convw, convb, rgba, rgbx, rglam, gng, hgbf, hgng, lbl, lng, lnb = params
    args += list(params)
    wg_spec = pl.BlockSpec((2, None) + wg.shape[2:], lambda b, i: (0, layer, 0, 0, 0))
    specs += ([layer_const(win), wg_spec] + [layer_const(p) for p in (wout, convw, convb, rgba, rgbx, rglam, gng,
                                                                      hgbf, hgng)]
              + [const(lbl), layer_const(lng), layer_const(lnb)])
    tab_args = [tabs[k] for k in _TABLE_ORDER]
    args += tab_args
    specs += [const(t) for t in tab_args]

    out_shape = [jax.ShapeDtypeStruct(x.shape, F32)] + [jax.ShapeDtypeStruct((batch,) + t, F32) for t in state_tails]
    out_specs = [x_spec] + [per_b(t) for t in state_tails]
    seg_rows = blk if bt == 1 else tile
    one_set = ([pltpu.VMEM((blk // seg_rows, SUBLANES + seg_rows, RG_WIDTH), F32), pltpu.VMEM((blk, RG_WIDTH), F32)]
               + [pltpu.VMEM((blk, HW), F32)] * N_NARROW + [pltpu.VMEM((blk, D_MODEL), BF16)] * 2)
    carries = [pltpu.VMEM((bt, SUBLANES, RG_WIDTH), F32)] * 2 + [pltpu.VMEM((bt, HW, HW), F32)] * 2
    return pl.pallas_call(
        kern, grid=grid, in_specs=specs, out_specs=out_specs, out_shape=out_shape,
        scratch_shapes=one_set * 2 + carries,
        compiler_params=pltpu.CompilerParams(dimension_semantics=("arbitrary", "arbitrary"),
                                             vmem_limit_bytes=VMEM_LIMIT_BYTES),
        name=f"layer{layer}_t{tile}",
    )(*args)


def _gate_blocks(w):
    depth = w.shape[0]
    per_half = RG_BLOCKS // 2
    eye = jnp.eye(per_half, dtype=w.dtype)
    w5 = w.reshape(depth, 2, per_half, RG_BLOCK, RG_BLOCK)
    return jnp.einsum('lhgij,gk->lhgikj', w5, eye).reshape(depth, 2, per_half * RG_BLOCK, per_half * RG_BLOCK)


def kernel(x_prompt, x_sample, cache_conv, state_rglru, state_ret, state_hgrn, w_in, conv_w, conv_b, rg_wa, rg_ba,
           rg_wx, rg_bx, rg_lambda, ret_gn_g, hg_bf, hg_lb_logits, hg_norm_g, w_out, ln_g, ln_b):
    depth = w_in.shape[0]
    bp, lp, _ = x_prompt.shape
    bs, ls, _ = x_sample.shape

    streams = []
    for batch, seq, pos0 in ((bp, lp, 0), (bs, ls, PAST_LEN)):
        tiling = _tiling(batch, seq)
        tabs_np, n_lv = _mixer_tables(tiling[2])
        tabs = {k: jnp.asarray(v, BF16 if k in _BF16_TABLES else F32) for k, v in tabs_np.items()}
        rope = tuple(jnp.asarray(t, F32) for t in _rope_tables(pos0 + np.arange(seq)))
        streams.append((tiling, n_lv, tabs, rope))

    row = lambda p: p[:, None, :]
    params = (w_in.astype(BF16), jnp.stack([_gate_blocks(rg_wa), _gate_blocks(rg_wx)]).astype(BF16),
              w_out.astype(BF16), conv_w, row(conv_b), row(rg_ba), row(rg_bx), row(rg_lambda), row(ret_gn_g),
              row(hg_bf), row(hg_norm_g), hg_lb_logits, row(ln_g), row(ln_b))
    sample_states = (cache_conv, state_rglru[:, :, None, :], state_ret.reshape(depth, bs, HW, HEAD_DIM),
                     state_hgrn.reshape(depth, bs, HW, HEAD_DIM))

    xp, xs = x_prompt, x_sample
    outs_p, outs_s = [], []
    for l in range(depth):
        tiling, n_lv, tabs, rope = streams[0]
        res = _layer_call(l, depth, xp, rope, None, params, tabs, n_lv, tiling)
        xp = res[0]
        outs_p.append(res[1:])
        tiling, n_lv, tabs, rope = streams[1]
        res = _layer_call(l, depth, xs, rope, sample_states, params, tabs, n_lv, tiling)
        xs = res[0]
        outs_s.append(res[1:])

    def collect(outs, batch):
        conv = jnp.stack([o[0] for o in outs])
        h = jnp.stack([o[1][:, 0, :] for o in outs])
        ret = jnp.stack([o[2] for o in outs]).reshape(depth, batch, HEADS, HEAD_DIM, HEAD_DIM)
        hg = jnp.stack([o[3] for o in outs]).reshape(depth, batch, HEADS, HEAD_DIM, HEAD_DIM)
        return conv, h, ret, hg

    return (xp, xs) + collect(outs_p, bp) + collect(outs_s, bs)
```

```python
import functools
import math

import jax
import jax.numpy as jnp
import numpy as np
from jax import lax
from jax.experimental import pallas as pl
from jax.experimental.pallas import tpu as pltpu

F32 = jnp.float32
BF16 = jnp.bfloat16

D_MODEL = 1024
RG_WIDTH = 512
RG_BLOCKS = 8
RG_BLOCK = RG_WIDTH // RG_BLOCKS
CONV_W = 4
RG_C = 8.0
HEADS = 4
HEAD_DIM = 64
HW = HEADS * HEAD_DIM
ROPE_BASE = 10000.0
LN_EPS = 1e-5
F_EPS = 1e-6
PAST_LEN = 1024
SPLITS = (RG_WIDTH, RG_WIDTH, HW, HW, HW, HW, HW, HW, HW, HW)
D_IN = sum(SPLITS)
SEG = tuple(int(v) for v in np.cumsum((0,) + SPLITS))
N_NARROW = 8

SUBLANES = 8
LANES = 128
VMEM_LIMIT_BYTES = 56 * 1024 * 1024

MAX_CHUNK = 64
BLOCK_ROWS = 256
PROJ_COLS = 256
MAX_TILE = 512
LEVEL0_ROWS = SUBLANES

QUADS = ((slice(0, 128), slice(0, 128)), (slice(128, 256), slice(128, 256)))


def _mixer_tables(chunk):
    c = chunk
    t = np.arange(c)
    hid = np.repeat(np.arange(HEADS), HEAD_DIM)
    row_h = np.repeat(np.arange(HEADS), c)
    scale = HEAD_DIM ** -0.5

    log_g = np.log1p(-np.exp2(-5.0 - np.arange(HEADS)))
    rel = t[:, None] - t[None, :]
    dmat = np.where(rel >= 0, np.exp(np.maximum(rel, 0)[None] * log_g[:, None, None]), 0.0)
    bd = (hid[:, None] == hid[None, :]).astype(np.float64)
    tabs = {
        "dall": scale * dmat.transpose(1, 0, 2).reshape(c, HEADS * c),
        "gq": scale * np.repeat(np.exp((t[:, None] + 1.0) * log_g[None, :]), HEAD_DIM, 1),
        "gk": np.exp((c - 1.0 - t)[:, None] * log_g[None, :])[:, hid],
        "gs": np.exp(c * log_g)[hid][:, None] * np.ones((1, HW)),
        "bd": bd,
        "hm": (row_h[:, None] == hid[None, :]).astype(np.float64),
        "e64": bd / HEAD_DIM,
    }

    n_lv = int(round(math.log2(c // LEVEL0_ROWS)))
    j = t[None, :]
    tt = t[:, None]
    w_cross = (j <= tt).astype(np.float64)
    w_state = (j > tt).astype(np.float64)
    r0 = (tt // LEVEL0_ROWS) * LEVEL0_ROWS + LEVEL0_ROWS // 2 - 1
    w_l0 = ((j > r0) & (j <= tt)).astype(np.float64) - ((j > tt) & (j <= r0)).astype(np.float64)
    blocks = [w_cross, w_state, w_l0]
    lvl = np.full((c, c), -1.0)
    causal = t[None, :] <= t[:, None]
    same0 = (t[:, None] // LEVEL0_ROWS) == (t[None, :] // LEVEL0_ROWS)
    lvl[causal & same0] = 0.0
    assigned = same0.copy()
    for lv in range(1, n_lv + 1):
        g = LEVEL0_ROWS * 2 ** lv
        r = (tt // g) * g + g // 2 - 1
        up = ((j > r) & (j <= tt)).astype(np.float64)
        lo = ((j > tt) & (j <= r)).astype(np.float64)
        blocks.append(up + lo)
        same = (t[:, None] // g) == (t[None, :] // g)
        lvl[causal & same & ~assigned] = float(lv)
        assigned |= same
    w = np.concatenate(blocks, axis=0)
    tabs["w3"] = np.concatenate([w, w, w], axis=1)
    tabs["lvl"] = np.tile(lvl, (1, HEADS))
    return tabs, n_lv


_BF16_TABLES = ("hm", "w3", "e64")
_TABLE_ORDER = ("dall", "gq", "gk", "gs", "bd", "hm", "w3", "lvl", "e64")


def _rope_tables(pos):
    half = HEAD_DIM // 2
    inv = ROPE_BASE ** (-np.arange(half, dtype=np.float64) / half)
    ang = pos.astype(np.float64)[:, None] * inv[None, :]
    cos = np.tile(np.cos(ang), (1, 2 * LANES // HEAD_DIM))
    sin = np.tile(np.concatenate([-np.sin(ang), np.sin(ang)], axis=1), (1, LANES // HEAD_DIM))
    return cos, sin


def _silu(x):
    return x * jax.nn.sigmoid(x)


def _dot(a, b):
    return jnp.dot(a, b, preferred_element_type=F32)


def _dot_t(a, b):
    return lax.dot_general(a, b, (((1,), (1,)), ((), ())), preferred_element_type=F32)


def _tdot(a, b):
    return lax.dot_general(a, b, (((0,), (0,)), ((), ())), preferred_element_type=F32)


def _seg_means(xs, e64):
    rows = xs[0].shape[0]
    x = xs[0] if len(xs) == 1 else jnp.concatenate(xs, axis=0)
    m = _dot(x.astype(BF16), e64)
    return [m[i * rows:(i + 1) * rows] for i in range(len(xs))]


def _stack_heads(x, mask):
    xb = x.astype(BF16)
    return jnp.concatenate([xb] * HEADS, axis=0) * mask


def _rotary(x, cos, sin, first_half):
    partner = jnp.where(first_half, pltpu.roll(x, HW - HEAD_DIM // 2, axis=1), pltpu.roll(x, HEAD_DIM // 2, axis=1))
    return x * cos + partner * sin


def _rg_scan(a, b, h_prev):
    c, w = a.shape
    g = c // SUBLANES
    a3 = a.reshape(g, SUBLANES, w)
    b3 = b.reshape(g, SUBLANES, w)
    sub = lax.broadcasted_iota(jnp.int32, a3.shape, 1)
    shift = 1
    while shift < SUBLANES:
        keep = sub >= shift
        a_sh = pltpu.roll(a3, shift, axis=1)
        b_sh = pltpu.roll(b3, shift, axis=1)
        b3 = jnp.where(keep, a3 * b_sh + b3, b3)
        a3 = jnp.where(keep, a3 * a_sh, a3)
        shift *= 2
    outs = []
    hb = h_prev
    for gi in range(g):
        hg = a3[gi] * hb + b3[gi]
        outs.append(hg)
        hb = jnp.broadcast_to(hg[SUBLANES - 1:SUBLANES, :], (SUBLANES, w))
    return jnp.concatenate(outs, axis=0), hb


def _rows(start, size):
    return pl.ds(start if isinstance(start, int) else pl.multiple_of(start, size), size)


def _update_state(ref, bi, old, scale, mask_ref, row_op, col_op):
    for rs, cs in QUADS:
        sc = scale[rs, cs] if scale.shape[0] > 1 else scale[:, cs]
        ref[bi, rs, cs] = sc * old[rs, cs] + mask_ref[rs, cs] * _tdot(row_op[:, rs], col_op[:, cs])


def _head_transpose(s):
    return jnp.concatenate([s[h * HEAD_DIM:(h + 1) * HEAD_DIM, :].T for h in range(HEADS)], axis=0)


def _expand_state(s, bd):
    return jnp.concatenate([s] * HEADS, axis=1) * bd


def _compact_state(s):
    out = s[:, 0:HEAD_DIM]
    for h in range(1, HEADS):
        out = out + s[:, h * HEAD_DIM:(h + 1) * HEAD_DIM]
    return out


def _layer_kernel(cfg, *refs):
    layer, depth, bt, tile, c, n_lv, blk, lookahead, zero_init = cfg
    n = bt * tile
    nb = n // blk
    it = iter(refs)
    x_ref = next(it)
    xn_ref = next(it) if lookahead else None
    cos_ref, sin_ref = next(it), next(it)
    conv_in, h_in, sret_in, shg_in = (None,) * 4 if zero_init else (next(it), next(it), next(it), next(it))
    (win_ref, wa_ref, wx_ref, wout_ref, convw_ref, convb_ref, rgba_ref, rgbx_ref, rglam_ref, gng_ref, hgbf_ref,
     hgng_ref, lbl_ref, lng_ref, lnb_ref) = [next(it) for _ in range(15)]
    dall_ref, gq_ref, gk_ref, gs_ref, bd_ref, hm_ref, w3_ref, lvl_ref, e64_ref = [next(it) for _ in range(9)]
    xo_ref, conv_o, h_o, sret_o, shg_o = [next(it) for _ in range(5)]
    scr = list(it)
    sets = (scr[0:12], scr[12:24])
    conv_c, h_c, sret_c, shg_c = scr[24:28]
    wg_s = scr[28]
    seg_rows = blk if bt == 1 else tile
    dn_alpha = (2 * depth) ** 0.25

    @pl.when(pl.program_id(1) == 0)
    def _load_state():
        if zero_init:
            for ref in (conv_c, h_c, sret_c, shg_c):
                ref[...] = jnp.zeros(ref.shape, F32)
        else:
            conv_c[:, 0:SUBLANES - (CONV_W - 1), :] = jnp.zeros((bt, SUBLANES - (CONV_W - 1), RG_WIDTH), F32)
            conv_c[:, SUBLANES - (CONV_W - 1):SUBLANES, :] = conv_in[...]
            h_c[...] = jnp.broadcast_to(h_in[...], (bt, SUBLANES, RG_WIDTH))
            for bi in range(bt):
                sret_c[bi] = _expand_state(sret_in[bi], bd_ref[...])
                shg_c[bi] = _expand_state(_head_transpose(shg_in[bi]), bd_ref[...])

    @pl.when((pl.program_id(0) == 0) & (pl.program_id(1) == 0))
    def _expand_gate_weights():
        for gi, ref in enumerate((wa_ref, wx_ref)):
            for hf in range(2):
                wg_s[gi, hf] = _expand_state(ref[hf], bd_ref[...]).astype(BF16)

    conv_b = convb_ref[...]
    conv_w = [convw_ref[j:j + 1, :] for j in range(CONV_W)]
    rg_ba = rgba_ref[...]
    rg_bx = rgbx_ref[...]
    c_lam = RG_C * jax.nn.log_sigmoid(rglam_ref[...])
    gn_g = gng_ref[...]
    hg_bf = hgbf_ref[...]
    hg_ng = hgng_ref[...]
    logits = [lbl_ref[li:li + 1, :] for li in range(depth)]
    mx = functools.reduce(jnp.maximum, logits)
    ex = [jnp.exp(v - mx) for v in logits]
    den = functools.reduce(lambda s, v: s + v, ex)
    probs = [v / den for v in ex]
    lb = functools.reduce(lambda s, v: s + v, probs[:layer + 1]) - probs[0]
    one_m_lb = 1.0 - lb

    def project_steps(xblk, dst):
        xb_s = dst[11]

        def stage():
            xb_s[...] = xblk.astype(BF16)

        def step(lo):
            def run():
                val = _dot(xb_s[...], win_ref[:, lo:lo + PROJ_COLS])
                for gi in range(2 + N_NARROW):
                    g_lo, g_hi = max(SEG[gi], lo), min(SEG[gi + 1], lo + PROJ_COLS)
                    if g_lo >= g_hi:
                        continue
                    part = val[:, g_lo - lo:g_hi - lo]
                    if gi == 0:
                        for si in range(blk // seg_rows):
                            dst[0][si, SUBLANES:SUBLANES + seg_rows, g_lo:g_hi] = part[si * seg_rows:(si + 1) * seg_rows]
                    else:
                        dst[gi][:, g_lo - SEG[gi]:g_hi - SEG[gi]] = part
            return run

        return [stage] + [step(lo) for lo in range(0, D_IN, PROJ_COLS)]

    def mix(src, base, filler=()):
        filler = list(filler)
        n_ch = blk // c
        fill_total = 4.0 + 4 * n_ch
        issued = [0.0, 0]

        def fill(weight):
            issued[0] += weight
            due = math.ceil(len(filler) * min(issued[0] / fill_total, 1.0) - 1e-9)
            while issued[1] < due:
                filler[issued[1]]()
                issued[1] += 1

        rgx_s, rgg_s, q_s, k_s, v_s, rgate_s, hq_s, hf_s, hi_s, hgate_s = src[:10]
        rgx_s[:, 0:SUBLANES, :] = conv_c[...]
        us = []
        for si in range(blk // seg_rows):
            win = rgx_s[si]
            u_seg = conv_b + win[SUBLANES:, :] * conv_w[CONV_W - 1]
            for back in range(1, CONV_W):
                u_seg = u_seg + pltpu.roll(win, back, axis=0)[SUBLANES:, :] * conv_w[CONV_W - 1 - back]
            us.append(u_seg)
        conv_c[...] = rgx_s[:, seg_rows:seg_rows + SUBLANES, :]
        u = us[0] if len(us) == 1 else jnp.concatenate(us, axis=0)
        fill(1.0)
        ub = u.astype(BF16)
        half = RG_WIDTH // 2
        r_pre = jnp.concatenate([_dot(ub[:, :half], wg_s[0, 0]), _dot(ub[:, half:], wg_s[0, 1])], axis=1)
        i_pre = jnp.concatenate([_dot(ub[:, :half], wg_s[1, 0]), _dot(ub[:, half:], wg_s[1, 1])], axis=1)
        log_a = c_lam * jax.nn.sigmoid(r_pre + rg_ba)
        a = jnp.exp(log_a)
        b_in = jnp.sqrt(-jnp.tanh(log_a) * (a * a + 1.0)) * (jax.nn.sigmoid(i_pre + rg_bx) * u)
        fill(1.0)
        e64 = e64_ref[...]
        hm = hm_ref[...]
        lvl = lvl_ref[...]
        first_half = (lax.broadcasted_iota(jnp.int32, (c, HW), 1) % HEAD_DIM) < HEAD_DIM // 2

        per_seq = seg_rows // c
        st = [dict(rows=slice(ci * c, (ci + 1) * c), bi=ci // per_seq,
                   trow=_rows(base + (ci % per_seq) * c, c)) for ci in range(n_ch)]

        for d in st:
            rows, trow = d["rows"], d["trow"]
            cos = jnp.concatenate([cos_ref[trow, :]] * (HW // LANES), axis=1)
            sin = jnp.concatenate([sin_ref[trow, :]] * (HW // LANES), axis=1)
            d["kr"] = _rotary(k_s[rows, :], cos, sin, first_half)
            d["qb"] = _rotary(q_s[rows, :], cos, sin, first_half).astype(BF16)
            d["scores"] = _dot_t(d["qb"], _stack_heads(d["kr"], hm))
            z = hf_s[rows, :] + hg_bf
            ez = jnp.exp(-jnp.abs(z))
            inv = 1.0 / (1.0 + ez)
            pos = z >= 0.0
            sig_p = jnp.where(pos, inv, ez * inv)
            sig_n = jnp.where(pos, ez * inv, inv)
            log_f = jnp.log(jnp.maximum(lb + one_m_lb * sig_p, F_EPS))
            d["kc"] = one_m_lb * sig_n
            f_hi = log_f.astype(BF16)
            res = log_f - f_hi.astype(F32)
            f_mid = res.astype(BF16)
            f_lo = (res - f_mid.astype(F32)).astype(BF16)
            d["expo"] = _dot(w3_ref[...], jnp.concatenate([f_hi, f_mid, f_lo], axis=0))
            fill(1.0)

        for d in st:
            bi, kr, qb = d["bi"], d["kr"], d["qb"]
            v = v_s[d["rows"], :]
            p = (d["scores"] * dall_ref[...]).astype(BF16)
            s_ret = sret_c[bi]
            d["o_b"] = _dot(p, _stack_heads(v, hm)) + _dot(qb, s_ret.astype(BF16)) * gq_ref[...]
            _update_state(sret_c, bi, s_ret, gs_ref, bd_ref, (kr * gk_ref[...]).astype(BF16), v.astype(BF16))
            fill(1.0)
            expo, kc = d["expo"], d["kc"]
            qh = hq_s[d["rows"], :]
            e0 = jnp.exp(expo[2 * c:3 * c])
            s0 = _dot_t((qh * e0).astype(BF16), _stack_heads(kc / e0, hm))
            pm = jnp.where(lvl == 0.0, s0, 0.0)
            for lv in range(1, n_lv + 1):
                el = jnp.exp(expo[(2 + lv) * c:(3 + lv) * c])
                sl_ = _dot_t((qh * el).astype(BF16), _stack_heads(kc * el, hm))
                pm = jnp.where(lvl == float(lv), sl_, pm)
            d["pm"] = pm
            fill(1.0)

        means = _seg_means([d["o_b"] for d in st], e64)
        for d, m in zip(st, means):
            d["dev"] = d["o_b"] - m
        fill(1.0)
        for d in st:
            bi, expo = d["bi"], d["expo"]
            vh = hi_s[d["rows"], :]
            qh = hq_s[d["rows"], :]
            e_cross = expo[0:c]
            s_hg = shg_c[bi]
            d["o_c"] = (_dot(d["pm"].astype(BF16), _stack_heads(vh, hm))
                        + _dot_t((qh * jnp.exp(e_cross)).astype(BF16), s_hg.astype(BF16)))
            k_st = (d["kc"] * jnp.exp(expo[c:2 * c])).astype(BF16)
            dec = jnp.exp(e_cross[c - 1:c, :])
            _update_state(shg_c, bi, s_hg, dec, bd_ref, vh.astype(BF16), k_st)
            fill(1.0)

        stats = _seg_means([d["dev"] * d["dev"] for d in st] + [d["o_c"] * d["o_c"] for d in st], e64)
        fill(1.0)

        ys = []
        for ci, d in enumerate(st):
            rows, bi = d["rows"], d["bi"]
            h, h_last = _rg_scan(a[rows], b_in[rows], h_c[bi])
            h_c[bi] = h_last
            y_a = h * _silu(rgg_s[rows, :])
            y_b = d["dev"] * lax.rsqrt(stats[ci] + LN_EPS) * gn_g * _silu(rgate_s[rows, :])
            y_c = d["o_c"] * lax.rsqrt(stats[n_ch + ci] + LN_EPS) * hg_ng * _silu(hgate_s[rows, :])
            ys.append(jnp.concatenate([y_a.astype(BF16), y_b.astype(BF16), y_c.astype(BF16)], axis=1))
        y = ys[0] if n_ch == 1 else jnp.concatenate(ys, axis=0)

        assert abs(issued[0] - fill_total) < 1e-6 and issued[1] == len(filler)

        if bt == 1:
            xrows = _rows(base, blk)
            x_in = x_ref[0, xrows, :]
        else:
            x_in = x_ref[...].reshape(n, D_MODEL)
        xn = dn_alpha * x_in + _dot(y, wout_ref[...])
        mu = jnp.mean(xn, axis=-1, keepdims=True)
        dv = xn - mu
        var = jnp.mean(dv * dv, axis=-1, keepdims=True)
        out = dv * lax.rsqrt(var + LN_EPS) * lng_ref[...] + lnb_ref[...]
        if bt == 1:
            xo_ref[0, xrows, :] = out
        else:
            xo_ref[...] = out.reshape(bt, tile, D_MODEL)

    def project(xblk, dst):
        for run in project_steps(xblk, dst):
            run()

    if not lookahead:
        project(x_ref[...].reshape(n, D_MODEL), sets[0])
        mix(sets[0], 0)
    else:
        @pl.when((pl.program_id(0) == 0) & (pl.program_id(1) == 0))
        def _first_block():
            project(x_ref[0, 0:blk, :], sets[0])

        def pair_body(k, carry):
            base0 = pl.multiple_of(2 * k * blk, 2 * blk)
            mix(sets[0], base0, project_steps(x_ref[0, _rows(base0 + blk, blk), :], sets[1]))
            in_tile = 2 * k + 2 < nb
            nxt = jnp.minimum(2 * k + 2, nb - 1) * blk
            ahead = jnp.where(in_tile, x_ref[0, _rows(nxt, blk), :], xn_ref[0])
            mix(sets[1], base0 + blk, project_steps(ahead, sets[0]))
            return carry

        lax.fori_loop(0, nb // 2, pair_body, 0)

    @pl.when(pl.program_id(1) == pl.num_programs(1) - 1)
    def _store_state():
        conv_o[...] = conv_c[:, SUBLANES - (CONV_W - 1):SUBLANES, :]
        h_o[...] = h_c[:, 0:1, :]
        for bi in range(bt):
            sret_o[bi] = _compact_state(sret_c[bi])
            shg_o[bi] = _head_transpose(_compact_state(shg_c[bi]))


def _tiling(batch, seq):
    chunk = min(MAX_CHUNK, seq)
    tile = min(seq, MAX_TILE)
    bt = 1 if tile >= BLOCK_ROWS else min(batch, BLOCK_ROWS // tile)
    blk = min(BLOCK_ROWS, bt * tile)
    lookahead = bt == 1 and (tile // blk) >= 2
    assert seq % tile == 0 and batch % bt == 0 and tile % chunk == 0 and chunk % 16 == 0 and blk % chunk == 0
    assert lookahead or bt * tile == blk
    assert not lookahead or (tile // blk) % 2 == 0
    return bt, tile, chunk, blk, lookahead


def _layer_call(layer, depth, x, rope, states, params, tabs, n_lv, tiling):
    bt, tile, chunk, blk, lookahead = tiling
    batch, seq, _ = x.shape
    n_tiles = seq // tile
    grid = (batch // bt, n_tiles)
    n_steps = grid[0] * grid[1]
    zero_init = states is None
    kern = functools.partial(_layer_kernel, (layer, depth, bt, tile, chunk, n_lv, blk, lookahead, zero_init))

    def layer_const(arr):
        tail = arr.shape[1:]
        return pl.BlockSpec((None,) + tail, lambda b, i: (layer,) + (0,) * len(tail))

    def const(arr):
        zeros = (0,) * arr.ndim
        return pl.BlockSpec(arr.shape, lambda b, i: zeros)

    def per_b(tail):
        return pl.BlockSpec((bt,) + tail, lambda b, i: (b,) + (0,) * len(tail))

    def per_lb(tail):
        return pl.BlockSpec((None, bt) + tail, lambda b, i: (layer, b) + (0,) * len(tail))

    def next_block(b, i):
        f = jnp.minimum(b * n_tiles + i + 1, n_steps - 1)
        return (f // n_tiles, (f % n_tiles) * (tile // blk), 0)

    state_tails = ((CONV_W - 1, RG_WIDTH), (1, RG_WIDTH), (HW, HEAD_DIM), (HW, HEAD_DIM))
    x_spec = pl.BlockSpec((bt, tile, D_MODEL), lambda b, i: (b, i, 0))
    rope_spec = pl.BlockSpec((tile, LANES), lambda b, i: (i, 0))
    args, specs = [x], [x_spec]
    if lookahead:
        args.append(x)
        specs.append(pl.BlockSpec((1, blk, D_MODEL), next_block))
    args += list(rope)
    specs += [rope_spec] * 2
    if not zero_init:
        args += list(states)
        specs += [per_lb(t) for t in state_tails]
    args += list(params)
    specs += [const(p) if i == 12 else layer_const(p) for i, p in enumerate(params)]
    tab_args = [tabs[k] for k in _TABLE_ORDER]
    args += tab_args
    specs += [const(t) for t in tab_args]

    out_shape = [jax.ShapeDtypeStruct(x.shape, F32)] + [jax.ShapeDtypeStruct((batch,) + t, F32) for t in state_tails]
    out_specs = [x_spec] + [per_b(t) for t in state_tails]
    seg_rows = blk if bt == 1 else tile
    one_set = ([pltpu.VMEM((blk // seg_rows, SUBLANES + seg_rows, RG_WIDTH), F32), pltpu.VMEM((blk, RG_WIDTH), F32)]
               + [pltpu.VMEM((blk, HW), F32)] * N_NARROW + [pltpu.VMEM((blk, D_MODEL), BF16)] * 2)
    carries = [pltpu.VMEM((bt, SUBLANES, RG_WIDTH), F32)] * 2 + [pltpu.VMEM((bt, HW, HW), F32)] * 2
    gate_w = [pltpu.VMEM((2, 2, HW, HW), BF16)]
    return pl.pallas_call(
        kern, grid=grid, in_specs=specs, out_specs=out_specs, out_shape=out_shape,
        scratch_shapes=one_set * 2 + carries + gate_w,
        compiler_params=pltpu.CompilerParams(dimension_semantics=("arbitrary", "arbitrary"),
                                             vmem_limit_bytes=VMEM_LIMIT_BYTES),
        name=f"layer{layer}_t{tile}",
    )(*args)


def kernel(x_prompt, x_sample, cache_conv, state_rglru, state_ret, state_hgrn, w_in, conv_w, conv_b, rg_wa, rg_ba,
           rg_wx, rg_bx, rg_lambda, ret_gn_g, hg_bf, hg_lb_logits, hg_norm_g, w_out, ln_g, ln_b):
    depth = w_in.shape[0]
    bp, lp, _ = x_prompt.shape
    bs, ls, _ = x_sample.shape

    streams = []
    for batch, seq, pos0 in ((bp, lp, 0), (bs, ls, PAST_LEN)):
        tiling = _tiling(batch, seq)
        tabs_np, n_lv = _mixer_tables(tiling[2])
        tabs = {k: jnp.asarray(v, BF16 if k in _BF16_TABLES else F32) for k, v in tabs_np.items()}
        rope = tuple(jnp.asarray(t, F32) for t in _rope_tables(pos0 + np.arange(seq)))
        streams.append((tiling, n_lv, tabs, rope))

    row = lambda p: p[:, None, :]
    halves = lambda w: w.reshape(depth, 2, HW, RG_BLOCK)
    params = (w_in.astype(BF16), halves(rg_wa), halves(rg_wx), w_out.astype(BF16), conv_w, row(conv_b), row(rg_ba),
              row(rg_bx), row(rg_lambda), row(ret_gn_g), row(hg_bf), row(hg_norm_g), hg_lb_logits, row(ln_g),
              row(ln_b))
    sample_states = (cache_conv, state_rglru[:, :, None, :], state_ret.reshape(depth, bs, HW, HEAD_DIM),
                     state_hgrn.reshape(depth, bs, HW, HEAD_DIM))

    xp, xs = x_prompt, x_sample
    outs_p, outs_s = [], []
    for l in range(depth):
        tiling, n_lv, tabs, rope = streams[0]
        res = _layer_call(l, depth, xp, rope, None, params, tabs, n_lv, tiling)
        xp = res[0]
        outs_p.append(res[1:])
        tiling, n_lv, tabs, rope = streams[1]
        res = _layer_call(l, depth, xs, rope, sample_states, params, tabs, n_lv, tiling)
        xs = res[0]
        outs_s.append(res[1:])

    def collect(outs, batch):
        conv = jnp.stack([o[0] for o in outs])
        h = jnp.stack([o[1][:, 0, :] for o in outs])
        ret = jnp.stack([o[2] for o in outs]).reshape(depth, batch, HEADS, HEAD_DIM, HEAD_DIM)
        hg = jnp.stack([o[3] for o in outs]).reshape(depth, batch, HEADS, HEAD_DIM, HEAD_DIM)
        return conv, h, ret, hg

    return (xp, xs) + collect(outs_p, bp) + collect(outs_s, bs)
```

```python
import functools
import math

import jax
import jax.numpy as jnp
import numpy as np
from jax import lax
from jax.experimental import pallas as pl
from jax.experimental.pallas import tpu as pltpu

F32 = jnp.float32
BF16 = jnp.bfloat16

D_MODEL = 1024
RG_WIDTH = 512
RG_BLOCKS = 8
RG_BLOCK = RG_WIDTH // RG_BLOCKS
CONV_W = 4
RG_C = 8.0
HEADS = 4
HEAD_DIM = 64
HW = HEADS * HEAD_DIM
ROPE_BASE = 10000.0
LN_EPS = 1e-5
F_EPS = 1e-6
PAST_LEN = 1024
SPLITS = (RG_WIDTH, RG_WIDTH, HW, HW, HW, HW, HW, HW, HW, HW)
D_IN = sum(SPLITS)
SEG = tuple(int(v) for v in np.cumsum((0,) + SPLITS))
N_NARROW = 8

SUBLANES = 8
LANES = 128
VMEM_LIMIT_BYTES = 56 * 1024 * 1024

MAX_CHUNK = 64
BLOCK_ROWS = 256
PROJ_COLS = 256
MAX_TILE = 1024
LEVEL0_ROWS = SUBLANES

QUADS = ((slice(0, 128), slice(0, 128)), (slice(128, 256), slice(128, 256)))


def _mixer_tables(chunk):
    c = chunk
    t = np.arange(c)
    hid = np.repeat(np.arange(HEADS), HEAD_DIM)
    row_h = np.repeat(np.arange(HEADS), c)
    scale = HEAD_DIM ** -0.5

    log_g = np.log1p(-np.exp2(-5.0 - np.arange(HEADS)))
    rel = t[:, None] - t[None, :]
    dmat = np.where(rel >= 0, np.exp(np.maximum(rel, 0)[None] * log_g[:, None, None]), 0.0)
    bd = (hid[:, None] == hid[None, :]).astype(np.float64)
    tabs = {
        "dall": scale * dmat.transpose(1, 0, 2).reshape(c, HEADS * c),
        "gq": scale * np.repeat(np.exp((t[:, None] + 1.0) * log_g[None, :]), HEAD_DIM, 1),
        "gk": np.exp((c - 1.0 - t)[:, None] * log_g[None, :])[:, hid],
        "gs": np.exp(c * log_g)[hid][:, None] * np.ones((1, HW)),
        "bd": bd,
        "hm": (row_h[:, None] == hid[None, :]).astype(np.float64),
        "e64": bd / HEAD_DIM,
    }

    n_lv = int(round(math.log2(c // LEVEL0_ROWS)))
    j = t[None, :]
    tt = t[:, None]
    w_cross = (j <= tt).astype(np.float64)
    w_state = (j > tt).astype(np.float64)
    r0 = (tt // LEVEL0_ROWS) * LEVEL0_ROWS + LEVEL0_ROWS // 2 - 1
    w_l0 = ((j > r0) & (j <= tt)).astype(np.float64) - ((j > tt) & (j <= r0)).astype(np.float64)
    blocks = [w_cross, w_state, w_l0]
    lvl = np.full((c, c), -1.0)
    causal = t[None, :] <= t[:, None]
    same0 = (t[:, None] // LEVEL0_ROWS) == (t[None, :] // LEVEL0_ROWS)
    lvl[causal & same0] = 0.0
    assigned = same0.copy()
    for lv in range(1, n_lv + 1):
        g = LEVEL0_ROWS * 2 ** lv
        r = (tt // g) * g + g // 2 - 1
        up = ((j > r) & (j <= tt)).astype(np.float64)
        lo = ((j > tt) & (j <= r)).astype(np.float64)
        blocks.append(up + lo)
        same = (t[:, None] // g) == (t[None, :] // g)
        lvl[causal & same & ~assigned] = float(lv)
        assigned |= same
    w = np.concatenate(blocks, axis=0)
    tabs["w3"] = np.concatenate([w, w, w], axis=1)
    tabs["lvl"] = np.tile(lvl, (1, HEADS))
    return tabs, n_lv


_BF16_TABLES = ("hm", "w3", "e64")
_TABLE_ORDER = ("dall", "gq", "gk", "gs", "bd", "hm", "w3", "lvl", "e64")


def _rope_tables(pos):
    half = HEAD_DIM // 2
    inv = ROPE_BASE ** (-np.arange(half, dtype=np.float64) / half)
    ang = pos.astype(np.float64)[:, None] * inv[None, :]
    cos = np.tile(np.cos(ang), (1, 2 * LANES // HEAD_DIM))
    sin = np.tile(np.concatenate([-np.sin(ang), np.sin(ang)], axis=1), (1, LANES // HEAD_DIM))
    return cos, sin


def _silu(x):
    return x * jax.nn.sigmoid(x)


def _dot(a, b):
    return jnp.dot(a, b, preferred_element_type=F32)


def _dot_t(a, b):
    return lax.dot_general(a, b, (((1,), (1,)), ((), ())), preferred_element_type=F32)


def _tdot(a, b):
    return lax.dot_general(a, b, (((0,), (0,)), ((), ())), preferred_element_type=F32)


def _seg_means(xs, e64):
    rows = xs[0].shape[0]
    x = xs[0] if len(xs) == 1 else jnp.concatenate(xs, axis=0)
    m = _dot(x.astype(BF16), e64)
    return [m[i * rows:(i + 1) * rows] for i in range(len(xs))]


def _stack_heads(x, mask):
    xb = x.astype(BF16)
    return jnp.concatenate([xb] * HEADS, axis=0) * mask


def _rotary(x, cos, sin, first_half):
    partner = jnp.where(first_half, pltpu.roll(x, HW - HEAD_DIM // 2, axis=1), pltpu.roll(x, HEAD_DIM // 2, axis=1))
    return x * cos + partner * sin


def _rg_scan(a, b, h_prev):
    c, w = a.shape
    g = c // SUBLANES
    a3 = a.reshape(g, SUBLANES, w)
    b3 = b.reshape(g, SUBLANES, w)
    sub = lax.broadcasted_iota(jnp.int32, a3.shape, 1)
    shift = 1
    while shift < SUBLANES:
        keep = sub >= shift
        a_sh = pltpu.roll(a3, shift, axis=1)
        b_sh = pltpu.roll(b3, shift, axis=1)
        b3 = jnp.where(keep, a3 * b_sh + b3, b3)
        a3 = jnp.where(keep, a3 * a_sh, a3)
        shift *= 2
    outs = []
    hb = h_prev
    for gi in range(g):
        hg = a3[gi] * hb + b3[gi]
        outs.append(hg)
        hb = jnp.broadcast_to(hg[SUBLANES - 1:SUBLANES, :], (SUBLANES, w))
    return jnp.concatenate(outs, axis=0), hb


def _rows(start, size):
    return pl.ds(start if isinstance(start, int) else pl.multiple_of(start, size), size)


def _update_state(ref, bi, old, scale, mask_ref, row_op, col_op):
    for rs, cs in QUADS:
        sc = scale[rs, cs] if scale.shape[0] > 1 else scale[:, cs]
        ref[bi, rs, cs] = sc * old[rs, cs] + mask_ref[rs, cs] * _tdot(row_op[:, rs], col_op[:, cs])


def _head_transpose(s):
    return jnp.concatenate([s[h * HEAD_DIM:(h + 1) * HEAD_DIM, :].T for h in range(HEADS)], axis=0)


def _expand_state(s, bd):
    return jnp.concatenate([s] * HEADS, axis=1) * bd


def _compact_state(s):
    out = s[:, 0:HEAD_DIM]
    for h in range(1, HEADS):
        out = out + s[:, h * HEAD_DIM:(h + 1) * HEAD_DIM]
    return out


def _layer_kernel(cfg, *refs):
    layer, depth, bt, tile, c, n_lv, blk, lookahead, zero_init = cfg
    n = bt * tile
    nb = n // blk
    it = iter(refs)
    x_ref = next(it)
    xn_ref = next(it) if lookahead else None
    cos_ref, sin_ref = next(it), next(it)
    conv_in, h_in, sret_in, shg_in = (None,) * 4 if zero_init else (next(it), next(it), next(it), next(it))
    (win_ref, wa_ref, wx_ref, wout_ref, convw_ref, convb_ref, rgba_ref, rgbx_ref, rglam_ref, gng_ref, hgbf_ref,
     hgng_ref, lbl_ref, lng_ref, lnb_ref) = [next(it) for _ in range(15)]
    dall_ref, gq_ref, gk_ref, gs_ref, bd_ref, hm_ref, w3_ref, lvl_ref, e64_ref = [next(it) for _ in range(9)]
    xo_ref, conv_o, h_o, sret_o, shg_o = [next(it) for _ in range(5)]
    scr = list(it)
    sets = (scr[0:12], scr[12:24])
    conv_c, h_c, sret_c, shg_c = scr[24:28]
    wg_s = scr[28]
    seg_rows = blk if bt == 1 else tile
    dn_alpha = (2 * depth) ** 0.25

    @pl.when(pl.program_id(1) == 0)
    def _load_state():
        if zero_init:
            for ref in (conv_c, h_c, sret_c, shg_c):
                ref[...] = jnp.zeros(ref.shape, F32)
        else:
            conv_c[:, 0:SUBLANES - (CONV_W - 1), :] = jnp.zeros((bt, SUBLANES - (CONV_W - 1), RG_WIDTH), F32)
            conv_c[:, SUBLANES - (CONV_W - 1):SUBLANES, :] = conv_in[...]
            h_c[...] = jnp.broadcast_to(h_in[...], (bt, SUBLANES, RG_WIDTH))
            for bi in range(bt):
                sret_c[bi] = _expand_state(sret_in[bi], bd_ref[...])
                shg_c[bi] = _expand_state(_head_transpose(shg_in[bi]), bd_ref[...])

    @pl.when((pl.program_id(0) == 0) & (pl.program_id(1) == 0))
    def _expand_gate_weights():
        for gi, ref in enumerate((wa_ref, wx_ref)):
            for hf in range(2):
                wg_s[gi, hf] = _expand_state(ref[hf], bd_ref[...]).astype(BF16)

    conv_b = convb_ref[...]
    conv_w = [convw_ref[j:j + 1, :] for j in range(CONV_W)]
    rg_ba = rgba_ref[...]
    rg_bx = rgbx_ref[...]
    c_lam = RG_C * jax.nn.log_sigmoid(rglam_ref[...])
    gn_g = gng_ref[...]
    hg_bf = hgbf_ref[...]
    hg_ng = hgng_ref[...]
    logits = [lbl_ref[li:li + 1, :] for li in range(depth)]
    mx = functools.reduce(jnp.maximum, logits)
    ex = [jnp.exp(v - mx) for v in logits]
    den = functools.reduce(lambda s, v: s + v, ex)
    probs = [v / den for v in ex]
    lb = functools.reduce(lambda s, v: s + v, probs[:layer + 1]) - probs[0]
    one_m_lb = 1.0 - lb

    def project_steps(xblk, dst):
        xb_s = dst[11]

        def stage():
            xb_s[...] = xblk.astype(BF16)

        def step(lo):
            def run():
                val = _dot(xb_s[...], win_ref[:, lo:lo + PROJ_COLS])
                for gi in range(2 + N_NARROW):
                    g_lo, g_hi = max(SEG[gi], lo), min(SEG[gi + 1], lo + PROJ_COLS)
                    if g_lo >= g_hi:
                        continue
                    part = val[:, g_lo - lo:g_hi - lo]
                    if gi == 0:
                        for si in range(blk // seg_rows):
                            dst[0][si, SUBLANES:SUBLANES + seg_rows, g_lo:g_hi] = part[si * seg_rows:(si + 1) * seg_rows]
                    else:
                        dst[gi][:, g_lo - SEG[gi]:g_hi - SEG[gi]] = part
            return run

        return [stage] + [step(lo) for lo in range(0, D_IN, PROJ_COLS)]

    def mix(src, base, filler=()):
        filler = list(filler)
        n_ch = blk // c
        fill_total = 4.0 + 4 * n_ch
        issued = [0.0, 0]

        def fill(weight):
            issued[0] += weight
            due = math.ceil(len(filler) * min(issued[0] / fill_total, 1.0) - 1e-9)
            while issued[1] < due:
                filler[issued[1]]()
                issued[1] += 1

        rgx_s, rgg_s, q_s, k_s, v_s, rgate_s, hq_s, hf_s, hi_s, hgate_s = src[:10]
        rgx_s[:, 0:SUBLANES, :] = conv_c[...]
        us = []
        for si in range(blk // seg_rows):
            win = rgx_s[si]
            u_seg = conv_b + win[SUBLANES:, :] * conv_w[CONV_W - 1]
            for back in range(1, CONV_W):
                u_seg = u_seg + pltpu.roll(win, back, axis=0)[SUBLANES:, :] * conv_w[CONV_W - 1 - back]
            us.append(u_seg)
        conv_c[...] = rgx_s[:, seg_rows:seg_rows + SUBLANES, :]
        u = us[0] if len(us) == 1 else jnp.concatenate(us, axis=0)
        fill(1.0)
        ub = u.astype(BF16)
        half = RG_WIDTH // 2
        r_pre = jnp.concatenate([_dot(ub[:, :half], wg_s[0, 0]), _dot(ub[:, half:], wg_s[0, 1])], axis=1)
        i_pre = jnp.concatenate([_dot(ub[:, :half], wg_s[1, 0]), _dot(ub[:, half:], wg_s[1, 1])], axis=1)
        log_a = c_lam * jax.nn.sigmoid(r_pre + rg_ba)
        a = jnp.exp(log_a)
        b_in = jnp.sqrt(-jnp.tanh(log_a) * (a * a + 1.0)) * (jax.nn.sigmoid(i_pre + rg_bx) * u)
        fill(1.0)
        e64 = e64_ref[...]
        hm = hm_ref[...]
        lvl = lvl_ref[...]
        first_half = (lax.broadcasted_iota(jnp.int32, (c, HW), 1) % HEAD_DIM) < HEAD_DIM // 2

        per_seq = seg_rows // c
        st = [dict(rows=slice(ci * c, (ci + 1) * c), bi=ci // per_seq,
                   trow=_rows(base + (ci % per_seq) * c, c)) for ci in range(n_ch)]

        for d in st:
            rows, trow = d["rows"], d["trow"]
            cos = jnp.concatenate([cos_ref[trow, :]] * (HW // LANES), axis=1)
            sin = jnp.concatenate([sin_ref[trow, :]] * (HW // LANES), axis=1)
            d["kr"] = _rotary(k_s[rows, :], cos, sin, first_half)
            d["qb"] = _rotary(q_s[rows, :], cos, sin, first_half).astype(BF16)
            d["scores"] = _dot_t(d["qb"], _stack_heads(d["kr"], hm))
            z = hf_s[rows, :] + hg_bf
            ez = jnp.exp(-jnp.abs(z))
            inv = 1.0 / (1.0 + ez)
            pos = z >= 0.0
            sig_p = jnp.where(pos, inv, ez * inv)
            sig_n = jnp.where(pos, ez * inv, inv)
            log_f = jnp.log(jnp.maximum(lb + one_m_lb * sig_p, F_EPS))
            d["kc"] = one_m_lb * sig_n
            f_hi = log_f.astype(BF16)
            res = log_f - f_hi.astype(F32)
            f_mid = res.astype(BF16)
            f_lo = (res - f_mid.astype(F32)).astype(BF16)
            d["expo"] = _dot(w3_ref[...], jnp.concatenate([f_hi, f_mid, f_lo], axis=0))
            fill(1.0)

        for d in st:
            bi, kr, qb = d["bi"], d["kr"], d["qb"]
            v = v_s[d["rows"], :]
            p = (d["scores"] * dall_ref[...]).astype(BF16)
            s_ret = sret_c[bi]
            d["o_b"] = _dot(p, _stack_heads(v, hm)) + _dot(qb, s_ret.astype(BF16)) * gq_ref[...]
            _update_state(sret_c, bi, s_ret, gs_ref, bd_ref, (kr * gk_ref[...]).astype(BF16), v.astype(BF16))
            fill(1.0)
            expo, kc = d["expo"], d["kc"]
            qh = hq_s[d["rows"], :]
            e0 = jnp.exp(expo[2 * c:3 * c])
            s0 = _dot_t((qh * e0).astype(BF16), _stack_heads(kc / e0, hm))
            pm = jnp.where(lvl == 0.0, s0, 0.0)
            for lv in range(1, n_lv + 1):
                el = jnp.exp(expo[(2 + lv) * c:(3 + lv) * c])
                sl_ = _dot_t((qh * el).astype(BF16), _stack_heads(kc * el, hm))
                pm = jnp.where(lvl == float(lv), sl_, pm)
            d["pm"] = pm
            fill(1.0)

        means = _seg_means([d["o_b"] for d in st], e64)
        for d, m in zip(st, means):
            d["dev"] = d["o_b"] - m
        fill(1.0)
        for d in st:
            bi, expo = d["bi"], d["expo"]
            vh = hi_s[d["rows"], :]
            qh = hq_s[d["rows"], :]
            e_cross = expo[0:c]
            s_hg = shg_c[bi]
            d["o_c"] = (_dot(d["pm"].astype(BF16), _stack_heads(vh, hm))
                        + _dot_t((qh * jnp.exp(e_cross)).astype(BF16), s_hg.astype(BF16)))
            k_st = (d["kc"] * jnp.exp(expo[c:2 * c])).astype(BF16)
            dec = jnp.exp(e_cross[c - 1:c, :])
            _update_state(shg_c, bi, s_hg, dec, bd_ref, vh.astype(BF16), k_st)
            fill(1.0)

        stats = _seg_means([d["dev"] * d["dev"] for d in st] + [d["o_c"] * d["o_c"] for d in st], e64)
        fill(1.0)

        ys = []
        for ci, d in enumerate(st):
            rows, bi = d["rows"], d["bi"]
            h, h_last = _rg_scan(a[rows], b_in[rows], h_c[bi])
            h_c[bi] = h_last
            y_a = h * _silu(rgg_s[rows, :])
            y_b = d["dev"] * lax.rsqrt(stats[ci] + LN_EPS) * gn_g * _silu(rgate_s[rows, :])
            y_c = d["o_c"] * lax.rsqrt(stats[n_ch + ci] + LN_EPS) * hg_ng * _silu(hgate_s[rows, :])
            ys.append(jnp.concatenate([y_a.astype(BF16), y_b.astype(BF16), y_c.astype(BF16)], axis=1))
        y = ys[0] if n_ch == 1 else jnp.concatenate(ys, axis=0)

        assert abs(issued[0] - fill_total) < 1e-6 and issued[1] == len(filler)

        if bt == 1:
            xrows = _rows(base, blk)
            x_in = x_ref[0, xrows, :]
        else:
            x_in = x_ref[...].reshape(n, D_MODEL)
        xn = dn_alpha * x_in + _dot(y, wout_ref[...])
        mu = jnp.mean(xn, axis=-1, keepdims=True)
        dv = xn - mu
        var = jnp.mean(dv * dv, axis=-1, keepdims=True)
        out = dv * lax.rsqrt(var + LN_EPS) * lng_ref[...] + lnb_ref[...]
        if bt == 1:
            xo_ref[0, xrows, :] = out
        else:
            xo_ref[...] = out.reshape(bt, tile, D_MODEL)

    def project(xblk, dst):
        for run in project_steps(xblk, dst):
            run()

    if not lookahead:
        project(x_ref[...].reshape(n, D_MODEL), sets[0])
        mix(sets[0], 0)
    else:
        @pl.when((pl.program_id(0) == 0) & (pl.program_id(1) == 0))
        def _first_block():
            project(x_ref[0, 0:blk, :], sets[0])

        def pair_body(k, carry):
            base0 = pl.multiple_of(2 * k * blk, 2 * blk)
            mix(sets[0], base0, project_steps(x_ref[0, _rows(base0 + blk, blk), :], sets[1]))
            in_tile = 2 * k + 2 < nb
            nxt = jnp.minimum(2 * k + 2, nb - 1) * blk
            ahead = jnp.where(in_tile, x_ref[0, _rows(nxt, blk), :], xn_ref[0])
            mix(sets[1], base0 + blk, project_steps(ahead, sets[0]))
            return carry

        lax.fori_loop(0, nb // 2, pair_body, 0)

    @pl.when(pl.program_id(1) == pl.num_programs(1) - 1)
    def _store_state():
        conv_o[...] = conv_c[:, SUBLANES - (CONV_W - 1):SUBLANES, :]
        h_o[...] = h_c[:, 0:1, :]
        for bi in range(bt):
            sret_o[bi] = _compact_state(sret_c[bi])
            shg_o[bi] = _head_transpose(_compact_state(shg_c[bi]))


def _tiling(batch, seq):
    chunk = min(MAX_CHUNK, seq)
    tile = min(seq, MAX_TILE)
    bt = 1 if tile >= BLOCK_ROWS else min(batch, BLOCK_ROWS // tile)
    blk = min(BLOCK_ROWS, bt * tile)
    lookahead = bt == 1 and (tile // blk) >= 2
    assert seq % tile == 0 and batch % bt == 0 and tile % chunk == 0 and chunk % 16 == 0 and blk % chunk == 0
    assert lookahead or bt * tile == blk
    assert not lookahead or (tile // blk) % 2 == 0
    return bt, tile, chunk, blk, lookahead


def _layer_call(layer, depth, x, rope, states, params, tabs, n_lv, tiling):
    bt, tile, chunk, blk, lookahead = tiling
    batch, seq, _ = x.shape
    n_tiles = seq // tile
    grid = (batch // bt, n_tiles)
    n_steps = grid[0] * grid[1]
    zero_init = states is None
    kern = functools.partial(_layer_kernel, (layer, depth, bt, tile, chunk, n_lv, blk, lookahead, zero_init))

    def layer_const(arr):
        tail = arr.shape[1:]
        return pl.BlockSpec((None,) + tail, lambda b, i: (layer,) + (0,) * len(tail))

    def const(arr):
        zeros = (0,) * arr.ndim
        return pl.BlockSpec(arr.shape, lambda b, i: zeros)

    def per_b(tail):
        return pl.BlockSpec((bt,) + tail, lambda b, i: (b,) + (0,) * len(tail))

    def per_lb(tail):
        return pl.BlockSpec((None, bt) + tail, lambda b, i: (layer, b) + (0,) * len(tail))

    def next_block(b, i):
        f = jnp.minimum(b * n_tiles + i + 1, n_steps - 1)
        return (f // n_tiles, (f % n_tiles) * (tile // blk), 0)

    state_tails = ((CONV_W - 1, RG_WIDTH), (1, RG_WIDTH), (HW, HEAD_DIM), (HW, HEAD_DIM))
    x_spec = pl.BlockSpec((bt, tile, D_MODEL), lambda b, i: (b, i, 0))
    rope_spec = pl.BlockSpec((tile, LANES), lambda b, i: (i, 0))
    args, specs = [x], [x_spec]
    if lookahead:
        args.append(x)
        specs.append(pl.BlockSpec((1, blk, D_MODEL), next_block))
    args += list(rope)
    specs += [rope_spec] * 2
    if not zero_init:
        args += list(states)
        specs += [per_lb(t) for t in state_tails]
    args += list(params)
    specs += [const(p) if i == 12 else layer_const(p) for i, p in enumerate(params)]
    tab_args = [tabs[k] for k in _TABLE_ORDER]
    args += tab_args
    specs += [const(t) for t in tab_args]

    out_shape = [jax.ShapeDtypeStruct(x.shape, F32)] + [jax.ShapeDtypeStruct((batch,) + t, F32) for t in state_tails]
    out_specs = [x_spec] + [per_b(t) for t in state_tails]
    seg_rows = blk if bt == 1 else tile
    one_set = ([pltpu.VMEM((blk // seg_rows, SUBLANES + seg_rows, RG_WIDTH), F32), pltpu.VMEM((blk, RG_WIDTH), F32)]
               + [pltpu.VMEM((blk, HW), F32)] * N_NARROW + [pltpu.VMEM((blk, D_MODEL), BF16)] * 2)
    carries = [pltpu.VMEM((bt, SUBLANES, RG_WIDTH), F32)] * 2 + [pltpu.VMEM((bt, HW, HW), F32)] * 2
    gate_w = [pltpu.VMEM((2, 2, HW, HW), BF16)]
    return pl.pallas_call(
        kern, grid=grid, in_specs=specs, out_specs=out_specs, out_shape=out_shape,
        scratch_shapes=one_set * 2 + carries + gate_w,
        compiler_params=pltpu.CompilerParams(dimension_semantics=("arbitrary", "arbitrary"),
                                             vmem_limit_bytes=VMEM_LIMIT_BYTES),
        name=f"layer{layer}_t{tile}",
    )(*args)


def kernel(x_prompt, x_sample, cache_conv, state_rglru, state_ret, state_hgrn, w_in, conv_w, conv_b, rg_wa, rg_ba,
           rg_wx, rg_bx, rg_lambda, ret_gn_g, hg_bf, hg_lb_logits, hg_norm_g, w_out, ln_g, ln_b):
    depth = w_in.shape[0]
    bp, lp, _ = x_prompt.shape
    bs, ls, _ = x_sample.shape

    streams = []
    for batch, seq, pos0 in ((bp, lp, 0), (bs, ls, PAST_LEN)):
        tiling = _tiling(batch, seq)
        tabs_np, n_lv = _mixer_tables(tiling[2])
        tabs = {k: jnp.asarray(v, BF16 if k in _BF16_TABLES else F32) for k, v in tabs_np.items()}
        rope = tuple(jnp.asarray(t, F32) for t in _rope_tables(pos0 + np.arange(seq)))
        streams.append((tiling, n_lv, tabs, rope))

    row = lambda p: p[:, None, :]
    halves = lambda w: w.reshape(depth, 2, HW, RG_BLOCK)
    params = (w_in.astype(BF16), halves(rg_wa), halves(rg_wx), w_out.astype(BF16), conv_w, row(conv_b), row(rg_ba),
              row(rg_bx), row(rg_lambda), row(ret_gn_g), row(hg_bf), row(hg_norm_g), hg_lb_logits, row(ln_g),
              row(ln_b))
    sample_states = (cache_conv, state_rglru[:, :, None, :], state_ret.reshape(depth, bs, HW, HEAD_DIM),
                     state_hgrn.reshape(depth, bs, HW, HEAD_DIM))

    xp, xs = x_prompt, x_sample
    outs_p, outs_s = [], []
    for l in range(depth):
        tiling, n_lv, tabs, rope = streams[0]
        res = _layer_call(l, depth, xp, rope, None, params, tabs, n_lv, tiling)
        xp = res[0]
        outs_p.append(res[1:])
        tiling, n_lv, tabs, rope = streams[1]
        res = _layer_call(l, depth, xs, rope, sample_states, params, tabs, n_lv, tiling)
        xs = res[0]
        outs_s.append(res[1:])

    def collect(outs, batch):
        conv = jnp.stack([o[0] for o in outs])
        h = jnp.stack([o[1][:, 0, :] for o in outs])
        ret = jnp.stack([o[2] for o in outs]).reshape(depth, batch, HEADS, HEAD_DIM, HEAD_DIM)
        hg = jnp.stack([o[3] for o in outs]).reshape(depth, batch, HEADS, HEAD_DIM, HEAD_DIM)
        return conv, h, ret, hg

    return (xp, xs) + collect(outs_p, bp) + collect(outs_s, bs)
```

```python
import functools
import math

import jax
import jax.numpy as jnp
import numpy as np
from jax import lax
from jax.experimental import pallas as pl
from jax.experimental.pallas import tpu as pltpu

F32 = jnp.float32
BF16 = jnp.bfloat16

D_MODEL = 1024
RG_WIDTH = 512
RG_BLOCKS = 8
RG_BLOCK = RG_WIDTH // RG_BLOCKS
CONV_W = 4
RG_C = 8.0
HEADS = 4
HEAD_DIM = 64
HW = HEADS * HEAD_DIM
ROPE_BASE = 10000.0
LN_EPS = 1e-5
F_EPS = 1e-6
PAST_LEN = 1024
SPLITS = (RG_WIDTH, RG_WIDTH, HW, HW, HW, HW, HW, HW, HW, HW)
D_IN = sum(SPLITS)
SEG = tuple(int(v) for v in np.cumsum((0,) + SPLITS))
N_NARROW = 8

SUBLANES = 8
LANES = 128
VMEM_LIMIT_BYTES = 56 * 1024 * 1024

MAX_CHUNK = 64
BLOCK_ROWS = 256
PROJ_COLS = 256
MAX_TILE = 1024
LEVEL0_ROWS = SUBLANES

QUADS = ((slice(0, 128), slice(0, 128)), (slice(128, 256), slice(128, 256)))


def _mixer_tables(chunk):
    c = chunk
    t = np.arange(c)
    hid = np.repeat(np.arange(HEADS), HEAD_DIM)
    row_h = np.repeat(np.arange(HEADS), c)
    scale = HEAD_DIM ** -0.5

    log_g = np.log1p(-np.exp2(-5.0 - np.arange(HEADS)))
    rel = t[:, None] - t[None, :]
    dmat = np.where(rel >= 0, np.exp(np.maximum(rel, 0)[None] * log_g[:, None, None]), 0.0)
    bd = (hid[:, None] == hid[None, :]).astype(np.float64)
    tabs = {
        "dall": scale * dmat.transpose(1, 0, 2).reshape(c, HEADS * c),
        "gq": scale * np.repeat(np.exp((t[:, None] + 1.0) * log_g[None, :]), HEAD_DIM, 1),
        "gk": np.exp((c - 1.0 - t)[:, None] * log_g[None, :])[:, hid],
        "gs": np.exp(c * log_g)[hid][:, None] * np.ones((1, HW)),
        "bd": bd,
        "hm": (row_h[:, None] == hid[None, :]).astype(np.float64),
        "e64": bd / HEAD_DIM,
    }

    n_lv = int(round(math.log2(c // LEVEL0_ROWS)))
    tri = (t[None, :] <= t[:, None]).astype(np.float64)
    signs = []
    lvl = np.full((c, c), -1.0)
    causal = t[None, :] <= t[:, None]
    same0 = (t[:, None] // LEVEL0_ROWS) == (t[None, :] // LEVEL0_ROWS)
    lvl[causal & same0] = 0.0
    assigned = same0.copy()
    for lv in range(1, n_lv + 1):
        g = LEVEL0_ROWS * 2 ** lv
        upper = (t % g) >= g // 2
        signs.append(np.where(upper, 1.0, -1.0)[:, None] * np.ones((1, HW)))
        same = (t[:, None] // g) == (t[None, :] // g)
        lvl[causal & same & ~assigned] = float(lv)
        assigned |= same
    tabs["tri3"] = np.concatenate([tri, tri, tri], axis=1)
    tabs["lsgn"] = np.concatenate(signs, axis=0)
    tabs["lvl"] = np.tile(lvl, (1, HEADS))
    return tabs, n_lv


_BF16_TABLES = ("hm", "tri3", "e64")
_TABLE_ORDER = ("dall", "gq", "gk", "gs", "bd", "hm", "tri3", "lsgn", "lvl", "e64")


def _rope_tables(pos):
    half = HEAD_DIM // 2
    inv = ROPE_BASE ** (-np.arange(half, dtype=np.float64) / half)
    ang = pos.astype(np.float64)[:, None] * inv[None, :]
    cos = np.tile(np.cos(ang), (1, 2 * LANES // HEAD_DIM))
    sin = np.tile(np.concatenate([-np.sin(ang), np.sin(ang)], axis=1), (1, LANES // HEAD_DIM))
    return cos, sin


def _silu(x):
    return x * jax.nn.sigmoid(x)


def _dot(a, b):
    return jnp.dot(a, b, preferred_element_type=F32)


def _dot_t(a, b):
    return lax.dot_general(a, b, (((1,), (1,)), ((), ())), preferred_element_type=F32)


def _tdot(a, b):
    return lax.dot_general(a, b, (((0,), (0,)), ((), ())), preferred_element_type=F32)


def _seg_means(xs, e64):
    rows = xs[0].shape[0]
    x = xs[0] if len(xs) == 1 else jnp.concatenate(xs, axis=0)
    m = _dot(x.astype(BF16), e64)
    return [m[i * rows:(i + 1) * rows] for i in range(len(xs))]


def _stack_heads(x, mask):
    xb = x.astype(BF16)
    return jnp.concatenate([xb] * HEADS, axis=0) * mask


def _rotary(x, cos, sin, first_half):
    partner = jnp.where(first_half, pltpu.roll(x, HW - HEAD_DIM // 2, axis=1), pltpu.roll(x, HEAD_DIM // 2, axis=1))
    return x * cos + partner * sin


def _rg_scan(a, b, h_prev):
    c, w = a.shape
    g = c // SUBLANES
    a3 = a.reshape(g, SUBLANES, w)
    b3 = b.reshape(g, SUBLANES, w)
    sub = lax.broadcasted_iota(jnp.int32, a3.shape, 1)
    shift = 1
    while shift < SUBLANES:
        keep = sub >= shift
        a_sh = pltpu.roll(a3, shift, axis=1)
        b_sh = pltpu.roll(b3, shift, axis=1)
        b3 = jnp.where(keep, a3 * b_sh + b3, b3)
        a3 = jnp.where(keep, a3 * a_sh, a3)
        shift *= 2
    outs = []
    hb = h_prev
    for gi in range(g):
        hg = a3[gi] * hb + b3[gi]
        outs.append(hg)
        hb = jnp.broadcast_to(hg[SUBLANES - 1:SUBLANES, :], (SUBLANES, w))
    return jnp.concatenate(outs, axis=0), hb


def _group_row(x, g, r):
    return jnp.concatenate([jnp.broadcast_to(x[g0 + r:g0 + r + 1, :], (g, x.shape[1]))
                            for g0 in range(0, x.shape[0], g)], axis=0)


def _rows(start, size):
    return pl.ds(start if isinstance(start, int) else pl.multiple_of(start, size), size)


def _update_state(ref, bi, old, scale, mask_ref, row_op, col_op):
    for rs, cs in QUADS:
        sc = scale[rs, cs] if scale.shape[0] > 1 else scale[:, cs]
        ref[bi, rs, cs] = sc * old[rs, cs] + mask_ref[rs, cs] * _tdot(row_op[:, rs], col_op[:, cs])


def _head_transpose(s):
    return jnp.concatenate([s[h * HEAD_DIM:(h + 1) * HEAD_DIM, :].T for h in range(HEADS)], axis=0)


def _expand_state(s, bd):
    return jnp.concatenate([s] * HEADS, axis=1) * bd


def _compact_state(s):
    out = s[:, 0:HEAD_DIM]
    for h in range(1, HEADS):
        out = out + s[:, h * HEAD_DIM:(h + 1) * HEAD_DIM]
    return out


def _layer_kernel(cfg, *refs):
    layer, depth, bt, tile, c, n_lv, blk, lookahead, zero_init = cfg
    n = bt * tile
    nb = n // blk
    it = iter(refs)
    x_ref = next(it)
    xn_ref = next(it) if lookahead else None
    cos_ref, sin_ref = next(it), next(it)
    conv_in, h_in, sret_in, shg_in = (None,) * 4 if zero_init else (next(it), next(it), next(it), next(it))
    (win_ref, wa_ref, wx_ref, wout_ref, convw_ref, convb_ref, rgba_ref, rgbx_ref, rglam_ref, gng_ref, hgbf_ref,
     hgng_ref, lbl_ref, lng_ref, lnb_ref) = [next(it) for _ in range(15)]
    (dall_ref, gq_ref, gk_ref, gs_ref, bd_ref, hm_ref, tri3_ref, lsgn_ref, lvl_ref,
     e64_ref) = [next(it) for _ in range(10)]
    xo_ref, conv_o, h_o, sret_o, shg_o = [next(it) for _ in range(5)]
    scr = list(it)
    sets = (scr[0:12], scr[12:24])
    conv_c, h_c, sret_c, shg_c = scr[24:28]
    wg_s = scr[28]
    seg_rows = blk if bt == 1 else tile
    dn_alpha = (2 * depth) ** 0.25

    @pl.when(pl.program_id(1) == 0)
    def _load_state():
        if zero_init:
            for ref in (conv_c, h_c, sret_c, shg_c):
                ref[...] = jnp.zeros(ref.shape, F32)
        else:
            conv_c[:, 0:SUBLANES - (CONV_W - 1), :] = jnp.zeros((bt, SUBLANES - (CONV_W - 1), RG_WIDTH), F32)
            conv_c[:, SUBLANES - (CONV_W - 1):SUBLANES, :] = conv_in[...]
            h_c[...] = jnp.broadcast_to(h_in[...], (bt, SUBLANES, RG_WIDTH))
            for bi in range(bt):
                sret_c[bi] = _expand_state(sret_in[bi], bd_ref[...])
                shg_c[bi] = _expand_state(_head_transpose(shg_in[bi]), bd_ref[...])

    @pl.when((pl.program_id(0) == 0) & (pl.program_id(1) == 0))
    def _expand_gate_weights():
        for gi, ref in enumerate((wa_ref, wx_ref)):
            for hf in range(2):
                wg_s[gi, hf] = _expand_state(ref[hf], bd_ref[...]).astype(BF16)

    conv_b = convb_ref[...]
    conv_w = [convw_ref[j:j + 1, :] for j in range(CONV_W)]
    rg_ba = rgba_ref[...]
    rg_bx = rgbx_ref[...]
    c_lam = RG_C * jax.nn.log_sigmoid(rglam_ref[...])
    gn_g = gng_ref[...]
    hg_bf = hgbf_ref[...]
    hg_ng = hgng_ref[...]
    logits = [lbl_ref[li:li + 1, :] for li in range(depth)]
    mx = functools.reduce(jnp.maximum, logits)
    ex = [jnp.exp(v - mx) for v in logits]
    den = functools.reduce(lambda s, v: s + v, ex)
    probs = [v / den for v in ex]
    lb = functools.reduce(lambda s, v: s + v, probs[:layer + 1]) - probs[0]
    one_m_lb = 1.0 - lb

    def project_steps(xblk, dst):
        xb_s = dst[11]

        def stage():
            xb_s[...] = xblk.astype(BF16)

        def step(lo):
            def run():
                val = _dot(xb_s[...], win_ref[:, lo:lo + PROJ_COLS])
                for gi in range(2 + N_NARROW):
                    g_lo, g_hi = max(SEG[gi], lo), min(SEG[gi + 1], lo + PROJ_COLS)
                    if g_lo >= g_hi:
                        continue
                    part = val[:, g_lo - lo:g_hi - lo]
                    if gi == 0:
                        for si in range(blk // seg_rows):
                            dst[0][si, SUBLANES:SUBLANES + seg_rows, g_lo:g_hi] = part[si * seg_rows:(si + 1) * seg_rows]
                    else:
                        dst[gi][:, g_lo - SEG[gi]:g_hi - SEG[gi]] = part
            return run

        return [stage] + [step(lo) for lo in range(0, D_IN, PROJ_COLS)]

    def mix(src, base, filler=()):
        filler = list(filler)
        n_ch = blk // c
        fill_total = 4.0 + 4 * n_ch
        issued = [0.0, 0]

        def fill(weight):
            issued[0] += weight
            due = math.ceil(len(filler) * min(issued[0] / fill_total, 1.0) - 1e-9)
            while issued[1] < due:
                filler[issued[1]]()
                issued[1] += 1

        rgx_s, rgg_s, q_s, k_s, v_s, rgate_s, hq_s, hf_s, hi_s, hgate_s = src[:10]
        rgx_s[:, 0:SUBLANES, :] = conv_c[...]
        us = []
        for si in range(blk // seg_rows):
            win = rgx_s[si]
            u_seg = conv_b + win[SUBLANES:, :] * conv_w[CONV_W - 1]
            for back in range(1, CONV_W):
                u_seg = u_seg + pltpu.roll(win, back, axis=0)[SUBLANES:, :] * conv_w[CONV_W - 1 - back]
            us.append(u_seg)
        conv_c[...] = rgx_s[:, seg_rows:seg_rows + SUBLANES, :]
        u = us[0] if len(us) == 1 else jnp.concatenate(us, axis=0)
        fill(1.0)
        ub = u.astype(BF16)
        half = RG_WIDTH // 2
        r_pre = jnp.concatenate([_dot(ub[:, :half], wg_s[0, 0]), _dot(ub[:, half:], wg_s[0, 1])], axis=1)
        i_pre = jnp.concatenate([_dot(ub[:, :half], wg_s[1, 0]), _dot(ub[:, half:], wg_s[1, 1])], axis=1)
        log_a = c_lam * jax.nn.sigmoid(r_pre + rg_ba)
        a = jnp.exp(log_a)
        b_in = jnp.sqrt(-jnp.tanh(log_a) * (a * a + 1.0)) * (jax.nn.sigmoid(i_pre + rg_bx) * u)
        fill(1.0)
        e64 = e64_ref[...]
        hm = hm_ref[...]
        lvl = lvl_ref[...]
        first_half = (lax.broadcasted_iota(jnp.int32, (c, HW), 1) % HEAD_DIM) < HEAD_DIM // 2

        per_seq = seg_rows // c
        st = [dict(rows=slice(ci * c, (ci + 1) * c), bi=ci // per_seq,
                   trow=_rows(base + (ci % per_seq) * c, c)) for ci in range(n_ch)]

        for d in st:
            rows, trow = d["rows"], d["trow"]
            cos = jnp.concatenate([cos_ref[trow, :]] * (HW // LANES), axis=1)
            sin = jnp.concatenate([sin_ref[trow, :]] * (HW // LANES), axis=1)
            d["kr"] = _rotary(k_s[rows, :], cos, sin, first_half)
            d["qb"] = _rotary(q_s[rows, :], cos, sin, first_half).astype(BF16)
            d["scores"] = _dot_t(d["qb"], _stack_heads(d["kr"], hm))
            z = hf_s[rows, :] + hg_bf
            ez = jnp.exp(-jnp.abs(z))
            inv = 1.0 / (1.0 + ez)
            pos = z >= 0.0
            sig_p = jnp.where(pos, inv, ez * inv)
            sig_n = jnp.where(pos, ez * inv, inv)
            log_f = jnp.log(jnp.maximum(lb + one_m_lb * sig_p, F_EPS))
            d["kc"] = one_m_lb * sig_n
            f_hi = log_f.astype(BF16)
            res = log_f - f_hi.astype(F32)
            f_mid = res.astype(BF16)
            f_lo = (res - f_mid.astype(F32)).astype(BF16)
            d["cum"] = _dot(tri3_ref[...], jnp.concatenate([f_hi, f_mid, f_lo], axis=0))
            fill(1.0)

        for d in st:
            bi, kr, qb = d["bi"], d["kr"], d["qb"]
            v = v_s[d["rows"], :]
            p = (d["scores"] * dall_ref[...]).astype(BF16)
            s_ret = sret_c[bi]
            d["o_b"] = _dot(p, _stack_heads(v, hm)) + _dot(qb, s_ret.astype(BF16)) * gq_ref[...]
            _update_state(sret_c, bi, s_ret, gs_ref, bd_ref, (kr * gk_ref[...]).astype(BF16), v.astype(BF16))
            fill(1.0)
            cum, kc = d["cum"], d["kc"]
            qh = hq_s[d["rows"], :]
            e0 = jnp.exp(cum - _group_row(cum, LEVEL0_ROWS, LEVEL0_ROWS // 2 - 1))
            s0 = _dot_t((qh * e0).astype(BF16), _stack_heads(kc / e0, hm))
            pm = jnp.where(lvl == 0.0, s0, 0.0)
            for lv in range(1, n_lv + 1):
                g = LEVEL0_ROWS * 2 ** lv
                el = jnp.exp(lsgn_ref[(lv - 1) * c:lv * c, :] * (cum - _group_row(cum, g, g // 2 - 1)))
                sl_ = _dot_t((qh * el).astype(BF16), _stack_heads(kc * el, hm))
                pm = jnp.where(lvl == float(lv), sl_, pm)
            d["pm"] = pm
            fill(1.0)

        means = _seg_means([d["o_b"] for d in st], e64)
        for d, m in zip(st, means):
            d["dev"] = d["o_b"] - m
        fill(1.0)
        for d in st:
            bi, cum = d["bi"], d["cum"]
            vh = hi_s[d["rows"], :]
            qh = hq_s[d["rows"], :]
            s_hg = shg_c[bi]
            d["o_c"] = (_dot(d["pm"].astype(BF16), _stack_heads(vh, hm))
                        + _dot_t((qh * jnp.exp(cum)).astype(BF16), s_hg.astype(BF16)))
            k_st = (d["kc"] * jnp.exp(cum[c - 1:c, :] - cum)).astype(BF16)
            dec = jnp.exp(cum[c - 1:c, :])
            _update_state(shg_c, bi, s_hg, dec, bd_ref, vh.astype(BF16), k_st)
            fill(1.0)

        stats = _seg_means([d["dev"] * d["dev"] for d in st] + [d["o_c"] * d["o_c"] for d in st], e64)
        fill(1.0)

        ys = []
        for ci, d in enumerate(st):
            rows, bi = d["rows"], d["bi"]
            h, h_last = _rg_scan(a[rows], b_in[rows], h_c[bi])
            h_c[bi] = h_last
            y_a = h * _silu(rgg_s[rows, :])
            y_b = d["dev"] * lax.rsqrt(stats[ci] + LN_EPS) * gn_g * _silu(rgate_s[rows, :])
            y_c = d["o_c"] * lax.rsqrt(stats[n_ch + ci] + LN_EPS) * hg_ng * _silu(hgate_s[rows, :])
            ys.append(jnp.concatenate([y_a.astype(BF16), y_b.astype(BF16), y_c.astype(BF16)], axis=1))
        y = ys[0] if n_ch == 1 else jnp.concatenate(ys, axis=0)

        assert abs(issued[0] - fill_total) < 1e-6 and issued[1] == len(filler)

        if bt == 1:
            xrows = _rows(base, blk)
            x_in = x_ref[0, xrows, :]
        else:
            x_in = x_ref[...].reshape(n, D_MODEL)
        xn = dn_alpha * x_in + _dot(y, wout_ref[...])
        mu = jnp.mean(xn, axis=-1, keepdims=True)
        dv = xn - mu
        var = jnp.mean(dv * dv, axis=-1, keepdims=True)
        out = dv * lax.rsqrt(var + LN_EPS) * lng_ref[...] + lnb_ref[...]
        if bt == 1:
            xo_ref[0, xrows, :] = out
        else:
            xo_ref[...] = out.reshape(bt, tile, D_MODEL)

    def project(xblk, dst):
        for run in project_steps(xblk, dst):
            run()

    if not lookahead:
        project(x_ref[...].reshape(n, D_MODEL), sets[0])
        mix(sets[0], 0)
    else:
        @pl.when((pl.program_id(0) == 0) & (pl.program_id(1) == 0))
        def _first_block():
            project(x_ref[0, 0:blk, :], sets[0])

        def pair_body(k, carry):
            base0 = pl.multiple_of(2 * k * blk, 2 * blk)
            mix(sets[0], base0, project_steps(x_ref[0, _rows(base0 + blk, blk), :], sets[1]))
            in_tile = 2 * k + 2 < nb
            nxt = jnp.minimum(2 * k + 2, nb - 1) * blk
            ahead = jnp.where(in_tile, x_ref[0, _rows(nxt, blk), :], xn_ref[0])
            mix(sets[1], base0 + blk, project_steps(ahead, sets[0]))
            return carry

        lax.fori_loop(0, nb // 2, pair_body, 0)

    @pl.when(pl.program_id(1) == pl.num_programs(1) - 1)
    def _store_state():
        conv_o[...] = conv_c[:, SUBLANES - (CONV_W - 1):SUBLANES, :]
        h_o[...] = h_c[:, 0:1, :]
        for bi in range(bt):
            sret_o[bi] = _compact_state(sret_c[bi])
            shg_o[bi] = _head_transpose(_compact_state(shg_c[bi]))


def _tiling(batch, seq):
    chunk = min(MAX_CHUNK, seq)
    tile = min(seq, MAX_TILE)
    bt = 1 if tile >= BLOCK_ROWS else min(batch, BLOCK_ROWS // tile)
    blk = min(BLOCK_ROWS, bt * tile)
    lookahead = bt == 1 and (tile // blk) >= 2
    assert seq % tile == 0 and batch % bt == 0 and tile % chunk == 0 and chunk % 16 == 0 and blk % chunk == 0
    assert lookahead or bt * tile == blk
    assert not lookahead or (tile // blk) % 2 == 0
    return bt, tile, chunk, blk, lookahead


def _layer_call(layer, depth, x, rope, states, params, tabs, n_lv, tiling):
    bt, tile, chunk, blk, lookahead = tiling
    batch, seq, _ = x.shape
    n_tiles = seq // tile
    grid = (batch // bt, n_tiles)
    n_steps = grid[0] * grid[1]
    zero_init = states is None
    kern = functools.partial(_layer_kernel, (layer, depth, bt, tile, chunk, n_lv, blk, lookahead, zero_init))

    def layer_const(arr):
        tail = arr.shape[1:]
        return pl.BlockSpec((None,) + tail, lambda b, i: (layer,) + (0,) * len(tail))

    def const(arr):
        zeros = (0,) * arr.ndim
        return pl.BlockSpec(arr.shape, lambda b, i: zeros)

    def per_b(tail):
        return pl.BlockSpec((bt,) + tail, lambda b, i: (b,) + (0,) * len(tail))

    def per_lb(tail):
        return pl.BlockSpec((None, bt) + tail, lambda b, i: (layer, b) + (0,) * len(tail))

    def next_block(b, i):
        f = jnp.minimum(b * n_tiles + i + 1, n_steps - 1)
        return (f // n_tiles, (f % n_tiles) * (tile // blk), 0)

    state_tails = ((CONV_W - 1, RG_WIDTH), (1, RG_WIDTH), (HW, HEAD_DIM), (HW, HEAD_DIM))
    x_spec = pl.BlockSpec((bt, tile, D_MODEL), lambda b, i: (b, i, 0))
    rope_spec = pl.BlockSpec((tile, LANES), lambda b, i: (i, 0))
    args, specs = [x], [x_spec]
    if lookahead:
        args.append(x)
        specs.append(pl.BlockSpec((1, blk, D_MODEL), next_block))
    args += list(rope)
    specs += [rope_spec] * 2
    if not zero_init:
        args += list(states)
        specs += [per_lb(t) for t in state_tails]
    args += list(params)
    specs += [const(p) if i == 12 else layer_const(p) for i, p in enumerate(params)]
    tab_args = [tabs[k] for k in _TABLE_ORDER]
    args += tab_args
    specs += [const(t) for t in tab_args]

    out_shape = [jax.ShapeDtypeStruct(x.shape, F32)] + [jax.ShapeDtypeStruct((batch,) + t, F32) for t in state_tails]
    out_specs = [x_spec] + [per_b(t) for t in state_tails]
    seg_rows = blk if bt == 1 else tile
    one_set = ([pltpu.VMEM((blk // seg_rows, SUBLANES + seg_rows, RG_WIDTH), F32), pltpu.VMEM((blk, RG_WIDTH), F32)]
               + [pltpu.VMEM((blk, HW), F32)] * N_NARROW + [pltpu.VMEM((blk, D_MODEL), BF16)] * 2)
    carries = [pltpu.VMEM((bt, SUBLANES, RG_WIDTH), F32)] * 2 + [pltpu.VMEM((bt, HW, HW), F32)] * 2
    gate_w = [pltpu.VMEM((2, 2, HW, HW), BF16)]
    return pl.pallas_call(
        kern, grid=grid, in_specs=specs, out_specs=out_specs, out_shape=out_shape,
        scratch_shapes=one_set * 2 + carries + gate_w,
        compiler_params=pltpu.CompilerParams(dimension_semantics=("arbitrary", "arbitrary"),
                                             vmem_limit_bytes=VMEM_LIMIT_BYTES),
        name=f"layer{layer}_t{tile}",
    )(*args)


def kernel(x_prompt, x_sample, cache_conv, state_rglru, state_ret, state_hgrn, w_in, conv_w, conv_b, rg_wa, rg_ba,
           rg_wx, rg_bx, rg_lambda, ret_gn_g, hg_bf, hg_lb_logits, hg_norm_g, w_out, ln_g, ln_b):
    depth = w_in.shape[0]
    bp, lp, _ = x_prompt.shape
    bs, ls, _ = x_sample.shape

    streams = []
    for batch, seq, pos0 in ((bp, lp, 0), (bs, ls, PAST_LEN)):
        tiling = _tiling(batch, seq)
        tabs_np, n_lv = _mixer_tables(tiling[2])
        tabs = {k: jnp.asarray(v, BF16 if k in _BF16_TABLES else F32) for k, v in tabs_np.items()}
        rope = tuple(jnp.asarray(t, F32) for t in _rope_tables(pos0 + np.arange(seq)))
        streams.append((tiling, n_lv, tabs, rope))

    row = lambda p: p[:, None, :]
    halves = lambda w: w.reshape(depth, 2, HW, RG_BLOCK)
    params = (w_in.astype(BF16), halves(rg_wa), halves(rg_wx), w_out.astype(BF16), conv_w, row(conv_b), row(rg_ba),
              row(rg_bx), row(rg_lambda), row(ret_gn_g), row(hg_bf), row(hg_norm_g), hg_lb_logits, row(ln_g),
              row(ln_b))
    sample_states = (cache_conv, state_rglru[:, :, None, :], state_ret.reshape(depth, bs, HW, HEAD_DIM),
                     state_hgrn.reshape(depth, bs, HW, HEAD_DIM))

    xp, xs = x_prompt, x_sample
    outs_p, outs_s = [], []
    for l in range(depth):
        tiling, n_lv, tabs, rope = streams[0]
        res = _layer_call(l, depth, xp, rope, None, params, tabs, n_lv, tiling)
        xp = res[0]
        outs_p.append(res[1:])
        tiling, n_lv, tabs, rope = streams[1]
        res = _layer_call(l, depth, xs, rope, sample_states, params, tabs, n_lv, tiling)
        xs = res[0]
        outs_s.append(res[1:])

    def collect(outs, batch):
        conv = jnp.stack([o[0] for o in outs])
        h = jnp.stack([o[1][:, 0, :] for o in outs])
        ret = jnp.stack([o[2] for o in outs]).reshape(depth, batch, HEADS, HEAD_DIM, HEAD_DIM)
        hg = jnp.stack([o[3] for o in outs]).reshape(depth, batch, HEADS, HEAD_DIM, HEAD_DIM)
        return conv, h, ret, hg

    return (xp, xs) + collect(outs_p, bp) + collect(outs_s, bs)
```

```python
import functools
import math

import jax
import jax.numpy as jnp
import numpy as np
from jax import lax
from jax.experimental import pallas as pl
from jax.experimental.pallas import tpu as pltpu

F32 = jnp.float32
BF16 = jnp.bfloat16

D_MODEL = 1024
RG_WIDTH = 512
RG_BLOCKS = 8
RG_BLOCK = RG_WIDTH // RG_BLOCKS
CONV_W = 4
RG_C = 8.0
HEADS = 4
HEAD_DIM = 64
HW = HEADS * HEAD_DIM
ROPE_BASE = 10000.0
LN_EPS = 1e-5
F_EPS = 1e-6
PAST_LEN = 1024
SPLITS = (RG_WIDTH, RG_WIDTH, HW, HW, HW, HW, HW, HW, HW, HW)
D_IN = sum(SPLITS)
SEG = tuple(int(v) for v in np.cumsum((0,) + SPLITS))
N_NARROW = 8

SUBLANES = 8
LANES = 128
VMEM_LIMIT_BYTES = 56 * 1024 * 1024

MAX_CHUNK = 64
BLOCK_ROWS = 256
PROJ_COLS = 256
MAX_TILE = 1024
LEVEL0_ROWS = SUBLANES

QUADS = ((slice(0, 128), slice(0, 128)), (slice(128, 256), slice(128, 256)))


def _mixer_tables(chunk):
    c = chunk
    t = np.arange(c)
    hid = np.repeat(np.arange(HEADS), HEAD_DIM)
    row_h = np.repeat(np.arange(HEADS), c)
    scale = HEAD_DIM ** -0.5

    log_g = np.log1p(-np.exp2(-5.0 - np.arange(HEADS)))
    rel = t[:, None] - t[None, :]
    dmat = np.where(rel >= 0, np.exp(np.maximum(rel, 0)[None] * log_g[:, None, None]), 0.0)
    bd = (hid[:, None] == hid[None, :]).astype(np.float64)
    tabs = {
        "dall": scale * dmat.transpose(1, 0, 2).reshape(c, HEADS * c),
        "gq": scale * np.repeat(np.exp((t[:, None] + 1.0) * log_g[None, :]), HEAD_DIM, 1),
        "gk": np.exp((c - 1.0 - t)[:, None] * log_g[None, :])[:, hid],
        "gs": np.exp(c * log_g)[hid][:, None] * np.ones((1, HW)),
        "bd": bd,
        "hm": (row_h[:, None] == hid[None, :]).astype(np.float64),
        "e64": bd / HEAD_DIM,
    }

    n_lv = int(round(math.log2(c // LEVEL0_ROWS)))
    tri = (t[None, :] <= t[:, None]).astype(np.float64)
    signs = []
    lvl = np.full((c, c), -1.0)
    causal = t[None, :] <= t[:, None]
    same0 = (t[:, None] // LEVEL0_ROWS) == (t[None, :] // LEVEL0_ROWS)
    lvl[causal & same0] = 0.0
    assigned = same0.copy()
    for lv in range(1, n_lv + 1):
        g = LEVEL0_ROWS * 2 ** lv
        upper = (t % g) >= g // 2
        signs.append(np.where(upper, 1.0, -1.0)[:, None] * np.ones((1, HW)))
        same = (t[:, None] // g) == (t[None, :] // g)
        lvl[causal & same & ~assigned] = float(lv)
        assigned |= same
    tabs["tri3"] = np.concatenate([tri, tri, tri], axis=1)
    tabs["lsgn"] = np.concatenate(signs, axis=0)
    tabs["lvl"] = np.tile(lvl, (1, HEADS))
    return tabs, n_lv


_BF16_TABLES = ("hm", "tri3", "e64")
_TABLE_ORDER = ("dall", "gq", "gk", "gs", "bd", "hm", "tri3", "lsgn", "lvl", "e64")


def _rope_tables(pos):
    half = HEAD_DIM // 2
    inv = ROPE_BASE ** (-np.arange(half, dtype=np.float64) / half)
    ang = pos.astype(np.float64)[:, None] * inv[None, :]
    cos = np.tile(np.cos(ang), (1, 2 * LANES // HEAD_DIM))
    sin = np.tile(np.concatenate([-np.sin(ang), np.sin(ang)], axis=1), (1, LANES // HEAD_DIM))
    return cos, sin


def _silu(x):
    return x * jax.nn.sigmoid(x)


def _dot(a, b):
    return jnp.dot(a, b, preferred_element_type=F32)


def _dot_t(a, b):
    return lax.dot_general(a, b, (((1,), (1,)), ((), ())), preferred_element_type=F32)


def _tdot(a, b):
    return lax.dot_general(a, b, (((0,), (0,)), ((), ())), preferred_element_type=F32)


def _seg_means(xs, e64):
    rows = xs[0].shape[0]
    x = xs[0] if len(xs) == 1 else jnp.concatenate(xs, axis=0)
    m = _dot(x.astype(BF16), e64)
    return [m[i * rows:(i + 1) * rows] for i in range(len(xs))]


def _stack_heads(x, mask):
    xb = x.astype(BF16)
    return jnp.concatenate([xb] * HEADS, axis=0) * mask


def _rotary(x, cos, sin, first_half):
    partner = jnp.where(first_half, pltpu.roll(x, HW - HEAD_DIM // 2, axis=1), pltpu.roll(x, HEAD_DIM // 2, axis=1))
    return x * cos + partner * sin


def _rg_scan(a, b, h_prev):
    c, w = a.shape
    g = c // SUBLANES
    a3 = a.reshape(g, SUBLANES, w)
    b3 = b.reshape(g, SUBLANES, w)
    sub = lax.broadcasted_iota(jnp.int32, a3.shape, 1)
    shift = 1
    while shift < SUBLANES:
        keep = sub >= shift
        a_sh = pltpu.roll(a3, shift, axis=1)
        b_sh = pltpu.roll(b3, shift, axis=1)
        b3 = jnp.where(keep, a3 * b_sh + b3, b3)
        a3 = jnp.where(keep, a3 * a_sh, a3)
        shift *= 2
    outs = []
    hb = h_prev
    for gi in range(g):
        hg = a3[gi] * hb + b3[gi]
        outs.append(hg)
        hb = jnp.broadcast_to(hg[SUBLANES - 1:SUBLANES, :], (SUBLANES, w))
    return jnp.concatenate(outs, axis=0), hb


def _group_row(x, g, r):
    return jnp.concatenate([jnp.broadcast_to(x[g0 + r:g0 + r + 1, :], (g, x.shape[1]))
                            for g0 in range(0, x.shape[0], g)], axis=0)


def _rows(start, size):
    return pl.ds(start if isinstance(start, int) else pl.multiple_of(start, size), size)


def _update_state(ref, bi, old, scale, mask_ref, row_op, col_op):
    for rs, cs in QUADS:
        sc = scale[rs, cs] if scale.shape[0] > 1 else scale[:, cs]
        ref[bi, rs, cs] = sc * old[rs, cs] + mask_ref[rs, cs] * _tdot(row_op[:, rs], col_op[:, cs])


def _head_transpose(s):
    return jnp.concatenate([s[h * HEAD_DIM:(h + 1) * HEAD_DIM, :].T for h in range(HEADS)], axis=0)


def _expand_state(s, bd):
    return jnp.concatenate([s] * HEADS, axis=1) * bd


def _compact_state(s):
    out = s[:, 0:HEAD_DIM]
    for h in range(1, HEADS):
        out = out + s[:, h * HEAD_DIM:(h + 1) * HEAD_DIM]
    return out


def _layer_kernel(cfg, *refs):
    layer, depth, bt, tile, c, n_lv, blk, lookahead, zero_init = cfg
    n = bt * tile
    nb = n // blk
    it = iter(refs)
    x_ref = next(it)
    xn_ref = next(it) if lookahead else None
    cos_ref, sin_ref = next(it), next(it)
    conv_in, h_in, sret_in, shg_in = (None,) * 4 if zero_init else (next(it), next(it), next(it), next(it))
    (win_ref, wa_ref, wx_ref, wout_ref, convw_ref, convb_ref, rgba_ref, rgbx_ref, rglam_ref, gng_ref, hgbf_ref,
     hgng_ref, lbl_ref, lng_ref, lnb_ref) = [next(it) for _ in range(15)]
    (dall_ref, gq_ref, gk_ref, gs_ref, bd_ref, hm_ref, tri3_ref, lsgn_ref, lvl_ref,
     e64_ref) = [next(it) for _ in range(10)]
    xo_ref, conv_o, h_o, sret_o, shg_o = [next(it) for _ in range(5)]
    scr = list(it)
    sets = (scr[0:12], scr[12:24])
    conv_c, h_c, sret_c, shg_c = scr[24:28]
    wg_s = scr[28]
    seg_rows = blk if bt == 1 else tile
    dn_alpha = (2 * depth) ** 0.25

    @pl.when(pl.program_id(1) == 0)
    def _load_state():
        if zero_init:
            for ref in (conv_c, h_c, sret_c, shg_c):
                ref[...] = jnp.zeros(ref.shape, F32)
        else:
            conv_c[:, 0:SUBLANES - (CONV_W - 1), :] = jnp.zeros((bt, SUBLANES - (CONV_W - 1), RG_WIDTH), F32)
            conv_c[:, SUBLANES - (CONV_W - 1):SUBLANES, :] = conv_in[...]
            h_c[...] = jnp.broadcast_to(h_in[...], (bt, SUBLANES, RG_WIDTH))
            for bi in range(bt):
                sret_c[bi] = _expand_state(sret_in[bi], bd_ref[...])
                shg_c[bi] = _expand_state(_head_transpose(shg_in[bi]), bd_ref[...])

    @pl.when((pl.program_id(0) == 0) & (pl.program_id(1) == 0))
    def _expand_gate_weights():
        for gi, ref in enumerate((wa_ref, wx_ref)):
            for hf in range(2):
                wg_s[gi, hf] = _expand_state(ref[hf], bd_ref[...]).astype(BF16)

    conv_b = convb_ref[...]
    conv_w = [convw_ref[j:j + 1, :] for j in range(CONV_W)]
    rg_ba = rgba_ref[...]
    rg_bx = rgbx_ref[...]
    c_lam = RG_C * jax.nn.log_sigmoid(rglam_ref[...])
    gn_g = gng_ref[...]
    hg_bf = hgbf_ref[...]
    hg_ng = hgng_ref[...]
    logits = [lbl_ref[li:li + 1, :] for li in range(depth)]
    mx = functools.reduce(jnp.maximum, logits)
    ex = [jnp.exp(v - mx) for v in logits]
    den = functools.reduce(lambda s, v: s + v, ex)
    probs = [v / den for v in ex]
    lb = functools.reduce(lambda s, v: s + v, probs[:layer + 1]) - probs[0]
    one_m_lb = 1.0 - lb

    def project_steps(xblk, dst):
        xb_s = dst[11]

        def stage():
            xb_s[...] = xblk.astype(BF16)

        def step(lo):
            def run():
                val = _dot(xb_s[...], win_ref[:, lo:lo + PROJ_COLS])
                for gi in range(2 + N_NARROW):
                    g_lo, g_hi = max(SEG[gi], lo), min(SEG[gi + 1], lo + PROJ_COLS)
                    if g_lo >= g_hi:
                        continue
                    part = val[:, g_lo - lo:g_hi - lo]
                    if gi == 0:
                        for si in range(blk // seg_rows):
                            dst[0][si, SUBLANES:SUBLANES + seg_rows, g_lo:g_hi] = part[si * seg_rows:(si + 1) * seg_rows]
                    else:
                        dst[gi][:, g_lo - SEG[gi]:g_hi - SEG[gi]] = part
            return run

        return [stage] + [step(lo) for lo in range(0, D_IN, PROJ_COLS)]

    def mix(src, base, filler=(), deferred=None):
        if deferred is not None:
            deferred()
        filler = list(filler)
        n_ch = blk // c
        fill_total = 4.0 + 4 * n_ch
        issued = [0.0, 0]

        def fill(weight):
            issued[0] += weight
            due = math.ceil(len(filler) * min(issued[0] / fill_total, 1.0) - 1e-9)
            while issued[1] < due:
                filler[issued[1]]()
                issued[1] += 1

        rgx_s, rgg_s, q_s, k_s, v_s, rgate_s, hq_s, hf_s, hi_s, hgate_s = src[:10]
        rgx_s[:, 0:SUBLANES, :] = conv_c[...]
        us = []
        for si in range(blk // seg_rows):
            win = rgx_s[si]
            u_seg = conv_b + win[SUBLANES:, :] * conv_w[CONV_W - 1]
            for back in range(1, CONV_W):
                u_seg = u_seg + pltpu.roll(win, back, axis=0)[SUBLANES:, :] * conv_w[CONV_W - 1 - back]
            us.append(u_seg)
        conv_c[...] = rgx_s[:, seg_rows:seg_rows + SUBLANES, :]
        u = us[0] if len(us) == 1 else jnp.concatenate(us, axis=0)
        fill(1.0)
        ub = u.astype(BF16)
        half = RG_WIDTH // 2
        r_pre = jnp.concatenate([_dot(ub[:, :half], wg_s[0, 0]), _dot(ub[:, half:], wg_s[0, 1])], axis=1)
        i_pre = jnp.concatenate([_dot(ub[:, :half], wg_s[1, 0]), _dot(ub[:, half:], wg_s[1, 1])], axis=1)
        log_a = c_lam * jax.nn.sigmoid(r_pre + rg_ba)
        a = jnp.exp(log_a)
        b_in = jnp.sqrt(-jnp.tanh(log_a) * (a * a + 1.0)) * (jax.nn.sigmoid(i_pre + rg_bx) * u)
        fill(1.0)
        e64 = e64_ref[...]
        hm = hm_ref[...]
        lvl = lvl_ref[...]
        first_half = (lax.broadcasted_iota(jnp.int32, (c, HW), 1) % HEAD_DIM) < HEAD_DIM // 2

        per_seq = seg_rows // c
        st = [dict(rows=slice(ci * c, (ci + 1) * c), bi=ci // per_seq,
                   trow=_rows(base + (ci % per_seq) * c, c)) for ci in range(n_ch)]

        for d in st:
            rows, trow = d["rows"], d["trow"]
            cos = jnp.concatenate([cos_ref[trow, :]] * (HW // LANES), axis=1)
            sin = jnp.concatenate([sin_ref[trow, :]] * (HW // LANES), axis=1)
            d["kr"] = _rotary(k_s[rows, :], cos, sin, first_half)
            d["qb"] = _rotary(q_s[rows, :], cos, sin, first_half).astype(BF16)
            d["scores"] = _dot_t(d["qb"], _stack_heads(d["kr"], hm))
            z = hf_s[rows, :] + hg_bf
            ez = jnp.exp(-jnp.abs(z))
            inv = 1.0 / (1.0 + ez)
            pos = z >= 0.0
            sig_p = jnp.where(pos, inv, ez * inv)
            sig_n = jnp.where(pos, ez * inv, inv)
            log_f = jnp.log(jnp.maximum(lb + one_m_lb * sig_p, F_EPS))
            d["kc"] = one_m_lb * sig_n
            f_hi = log_f.astype(BF16)
            res = log_f - f_hi.astype(F32)
            f_mid = res.astype(BF16)
            f_lo = (res - f_mid.astype(F32)).astype(BF16)
            d["cum"] = _dot(tri3_ref[...], jnp.concatenate([f_hi, f_mid, f_lo], axis=0))
            fill(1.0)

        for d in st:
            bi, kr, qb = d["bi"], d["kr"], d["qb"]
            v = v_s[d["rows"], :]
            p = (d["scores"] * dall_ref[...]).astype(BF16)
            s_ret = sret_c[bi]
            d["o_b"] = _dot(p, _stack_heads(v, hm)) + _dot(qb, s_ret.astype(BF16)) * gq_ref[...]
            _update_state(sret_c, bi, s_ret, gs_ref, bd_ref, (kr * gk_ref[...]).astype(BF16), v.astype(BF16))
            fill(1.0)
            cum, kc = d["cum"], d["kc"]
            qh = hq_s[d["rows"], :]
            e0 = jnp.exp(cum - _group_row(cum, LEVEL0_ROWS, LEVEL0_ROWS // 2 - 1))
            s0 = _dot_t((qh * e0).astype(BF16), _stack_heads(kc / e0, hm))
            pm = jnp.where(lvl == 0.0, s0, 0.0)
            for lv in range(1, n_lv + 1):
                g = LEVEL0_ROWS * 2 ** lv
                el = jnp.exp(lsgn_ref[(lv - 1) * c:lv * c, :] * (cum - _group_row(cum, g, g // 2 - 1)))
                sl_ = _dot_t((qh * el).astype(BF16), _stack_heads(kc * el, hm))
                pm = jnp.where(lvl == float(lv), sl_, pm)
            d["pm"] = pm
            fill(1.0)

        means = _seg_means([d["o_b"] for d in st], e64)
        for d, m in zip(st, means):
            d["dev"] = d["o_b"] - m
        fill(1.0)
        for d in st:
            bi, cum = d["bi"], d["cum"]
            vh = hi_s[d["rows"], :]
            qh = hq_s[d["rows"], :]
            s_hg = shg_c[bi]
            d["o_c"] = (_dot(d["pm"].astype(BF16), _stack_heads(vh, hm))
                        + _dot_t((qh * jnp.exp(cum)).astype(BF16), s_hg.astype(BF16)))
            k_st = (d["kc"] * jnp.exp(cum[c - 1:c, :] - cum)).astype(BF16)
            dec = jnp.exp(cum[c - 1:c, :])
            _update_state(shg_c, bi, s_hg, dec, bd_ref, vh.astype(BF16), k_st)
            fill(1.0)

        stats = _seg_means([d["dev"] * d["dev"] for d in st] + [d["o_c"] * d["o_c"] for d in st], e64)
        fill(1.0)

        ys = []
        for ci, d in enumerate(st):
            rows, bi = d["rows"], d["bi"]
            h, h_last = _rg_scan(a[rows], b_in[rows], h_c[bi])
            h_c[bi] = h_last
            y_a = h * _silu(rgg_s[rows, :])
            y_b = d["dev"] * lax.rsqrt(stats[ci] + LN_EPS) * gn_g * _silu(rgate_s[rows, :])
            y_c = d["o_c"] * lax.rsqrt(stats[n_ch + ci] + LN_EPS) * hg_ng * _silu(hgate_s[rows, :])
            ys.append(jnp.concatenate([y_a.astype(BF16), y_b.astype(BF16), y_c.astype(BF16)], axis=1))
        y = ys[0] if n_ch == 1 else jnp.concatenate(ys, axis=0)

        assert abs(issued[0] - fill_total) < 1e-6 and issued[1] == len(filler)

        def tail():
            if bt == 1:
                xrows = _rows(base, blk)
                x_in = x_ref[0, xrows, :]
            else:
                x_in = x_ref[...].reshape(n, D_MODEL)
            xn = dn_alpha * x_in + _dot(y, wout_ref[...])
            mu = jnp.mean(xn, axis=-1, keepdims=True)
            dv = xn - mu
            var = jnp.mean(dv * dv, axis=-1, keepdims=True)
            out = dv * lax.rsqrt(var + LN_EPS) * lng_ref[...] + lnb_ref[...]
            if bt == 1:
                xo_ref[0, xrows, :] = out
            else:
                xo_ref[...] = out.reshape(bt, tile, D_MODEL)

        return tail

    def project(xblk, dst):
        for run in project_steps(xblk, dst):
            run()

    if not lookahead:
        project(x_ref[...].reshape(n, D_MODEL), sets[0])
        mix(sets[0], 0)()
    else:
        @pl.when((pl.program_id(0) == 0) & (pl.program_id(1) == 0))
        def _first_block():
            project(x_ref[0, 0:blk, :], sets[0])

        tail = None
        for r in range(nb):
            ahead = x_ref[0, (r + 1) * blk:(r + 2) * blk, :] if r + 1 < nb else xn_ref[0]
            tail = mix(sets[r % 2], r * blk, project_steps(ahead, sets[(r + 1) % 2]), deferred=tail)
        tail()

    @pl.when(pl.program_id(1) == pl.num_programs(1) - 1)
    def _store_state():
        conv_o[...] = conv_c[:, SUBLANES - (CONV_W - 1):SUBLANES, :]
        h_o[...] = h_c[:, 0:1, :]
        for bi in range(bt):
            sret_o[bi] = _compact_state(sret_c[bi])
            shg_o[bi] = _head_transpose(_compact_state(shg_c[bi]))


def _tiling(batch, seq):
    chunk = min(MAX_CHUNK, seq)
    tile = min(seq, MAX_TILE)
    bt = 1 if tile >= BLOCK_ROWS else min(batch, BLOCK_ROWS // tile)
    blk = min(BLOCK_ROWS, bt * tile)
    lookahead = bt == 1 and (tile // blk) >= 2
    assert seq % tile == 0 and batch % bt == 0 and tile % chunk == 0 and chunk % 16 == 0 and blk % chunk == 0
    assert lookahead or bt * tile == blk
    assert not lookahead or (tile // blk) % 2 == 0
    return bt, tile, chunk, blk, lookahead


def _layer_call(layer, depth, x, rope, states, params, tabs, n_lv, tiling):
    bt, tile, chunk, blk, lookahead = tiling
    batch, seq, _ = x.shape
    n_tiles = seq // tile
    grid = (batch // bt, n_tiles)
    n_steps = grid[0] * grid[1]
    zero_init = states is None
    kern = functools.partial(_layer_kernel, (layer, depth, bt, tile, chunk, n_lv, blk, lookahead, zero_init))

    def layer_const(arr):
        tail = arr.shape[1:]
        return pl.BlockSpec((None,) + tail, lambda b, i: (layer,) + (0,) * len(tail))

    def const(arr):
        zeros = (0,) * arr.ndim
        return pl.BlockSpec(arr.shape, lambda b, i: zeros)

    def per_b(tail):
        return pl.BlockSpec((bt,) + tail, lambda b, i: (b,) + (0,) * len(tail))

    def per_lb(tail):
        return pl.BlockSpec((None, bt) + tail, lambda b, i: (layer, b) + (0,) * len(tail))

    def next_block(b, i):
        f = jnp.minimum(b * n_tiles + i + 1, n_steps - 1)
        return (f // n_tiles, (f % n_tiles) * (tile // blk), 0)

    state_tails = ((CONV_W - 1, RG_WIDTH), (1, RG_WIDTH), (HW, HEAD_DIM), (HW, HEAD_DIM))
    x_spec = pl.BlockSpec((bt, tile, D_MODEL), lambda b, i: (b, i, 0))
    rope_spec = pl.BlockSpec((tile, LANES), lambda b, i: (i, 0))
    args, specs = [x], [x_spec]
    if lookahead:
        args.append(x)
        specs.append(pl.BlockSpec((1, blk, D_MODEL), next_block))
    args += list(rope)
    specs += [rope_spec] * 2
    if not zero_init:
        args += list(states)
        specs += [per_lb(t) for t in state_tails]
    args += list(params)
    specs += [const(p) if i == 12 else layer_const(p) for i, p in enumerate(params)]
    tab_args = [tabs[k] for k in _TABLE_ORDER]
    args += tab_args
    specs += [const(t) for t in tab_args]

    out_shape = [jax.ShapeDtypeStruct(x.shape, F32)] + [jax.ShapeDtypeStruct((batch,) + t, F32) for t in state_tails]
    out_specs = [x_spec] + [per_b(t) for t in state_tails]
    seg_rows = blk if bt == 1 else tile
    one_set = ([pltpu.VMEM((blk // seg_rows, SUBLANES + seg_rows, RG_WIDTH), F32), pltpu.VMEM((blk, RG_WIDTH), F32)]
               + [pltpu.VMEM((blk, HW), F32)] * N_NARROW + [pltpu.VMEM((blk, D_MODEL), BF16)] * 2)
    carries = [pltpu.VMEM((bt, SUBLANES, RG_WIDTH), F32)] * 2 + [pltpu.VMEM((bt, HW, HW), F32)] * 2
    gate_w = [pltpu.VMEM((2, 2, HW, HW), BF16)]
    return pl.pallas_call(
        kern, grid=grid, in_specs=specs, out_specs=out_specs, out_shape=out_shape,
        scratch_shapes=one_set * 2 + carries + gate_w,
        compiler_params=pltpu.CompilerParams(dimension_semantics=("arbitrary", "arbitrary"),
                                             vmem_limit_bytes=VMEM_LIMIT_BYTES),
        name=f"layer{layer}_t{tile}",
    )(*args)


def kernel(x_prompt, x_sample, cache_conv, state_rglru, state_ret, state_hgrn, w_in, conv_w, conv_b, rg_wa, rg_ba,
           rg_wx, rg_bx, rg_lambda, ret_gn_g, hg_bf, hg_lb_logits, hg_norm_g, w_out, ln_g, ln_b):
    depth = w_in.shape[0]
    bp, lp, _ = x_prompt.shape
    bs, ls, _ = x_sample.shape

    streams = []
    for batch, seq, pos0 in ((bp, lp, 0), (bs, ls, PAST_LEN)):
        tiling = _tiling(batch, seq)
        tabs_np, n_lv = _mixer_tables(tiling[2])
        tabs = {k: jnp.asarray(v, BF16 if k in _BF16_TABLES else F32) for k, v in tabs_np.items()}
        rope = tuple(jnp.asarray(t, F32) for t in _rope_tables(pos0 + np.arange(seq)))
        streams.append((tiling, n_lv, tabs, rope))

    row = lambda p: p[:, None, :]
    halves = lambda w: w.reshape(depth, 2, HW, RG_BLOCK)
    params = (w_in.astype(BF16), halves(rg_wa), halves(rg_wx), w_out.astype(BF16), conv_w, row(conv_b), row(rg_ba),
              row(rg_bx), row(rg_lambda), row(ret_gn_g), row(hg_bf), row(hg_norm_g), hg_lb_logits, row(ln_g),
              row(ln_b))
    sample_states = (cache_conv, state_rglru[:, :, None, :], state_ret.reshape(depth, bs, HW, HEAD_DIM),
                     state_hgrn.reshape(depth, bs, HW, HEAD_DIM))

    xp, xs = x_prompt, x_sample
    outs_p, outs_s = [], []
    for l in range(depth):
        tiling, n_lv, tabs, rope = streams[0]
        res = _layer_call(l, depth, xp, rope, None, params, tabs, n_lv, tiling)
        xp = res[0]
        outs_p.append(res[1:])
        tiling, n_lv, tabs, rope = streams[1]
        res = _layer_call(l, depth, xs, rope, sample_states, params, tabs, n_lv, tiling)
        xs = res[0]
        outs_s.append(res[1:])

    def collect(outs, batch):
        conv = jnp.stack([o[0] for o in outs])
        h = jnp.stack([o[1][:, 0, :] for o in outs])
        ret = jnp.stack([o[2] for o in outs]).reshape(depth, batch, HEADS, HEAD_DIM, HEAD_DIM)
        hg = jnp.stack([o[3] for o in outs]).reshape(depth, batch, HEADS, HEAD_DIM, HEAD_DIM)
        return conv, h, ret, hg

    return (xp, xs) + collect(outs_p, bp) + collect(outs_s, bs)
```

```python
import functools
import math

import jax
import jax.numpy as jnp
import numpy as np
from jax import lax
from jax.experimental import pallas as pl
from jax.experimental.pallas import tpu as pltpu

F32 = jnp.float32
BF16 = jnp.bfloat16

D_MODEL = 1024
RG_WIDTH = 512
RG_BLOCKS = 8
RG_BLOCK = RG_WIDTH // RG_BLOCKS
CONV_W = 4
RG_C = 8.0
HEADS = 4
HEAD_DIM = 64
HW = HEADS * HEAD_DIM
ROPE_BASE = 10000.0
LN_EPS = 1e-5
F_EPS = 1e-6
PAST_LEN = 1024
SPLITS = (RG_WIDTH, RG_WIDTH, HW, HW, HW, HW, HW, HW, HW, HW)
D_IN = sum(SPLITS)
SEG = tuple(int(v) for v in np.cumsum((0,) + SPLITS))
N_NARROW = 8

SUBLANES = 8
LANES = 128
VMEM_LIMIT_BYTES = 56 * 1024 * 1024

MAX_CHUNK = 64
BLOCK_ROWS = 256
PROJ_COLS = 256
MAX_TILE = 1024
LOCKSTEP = 2
LEVEL0_ROWS = SUBLANES

QUADS = ((slice(0, 128), slice(0, 128)), (slice(128, 256), slice(128, 256)))


def _mixer_tables(chunk):
    c = chunk
    t = np.arange(c)
    hid = np.repeat(np.arange(HEADS), HEAD_DIM)
    row_h = np.repeat(np.arange(HEADS), c)
    scale = HEAD_DIM ** -0.5

    log_g = np.log1p(-np.exp2(-5.0 - np.arange(HEADS)))
    rel = t[:, None] - t[None, :]
    dmat = np.where(rel >= 0, np.exp(np.maximum(rel, 0)[None] * log_g[:, None, None]), 0.0)
    bd = (hid[:, None] == hid[None, :]).astype(np.float64)
    tabs = {
        "dall": scale * dmat.transpose(1, 0, 2).reshape(c, HEADS * c),
        "gq": scale * np.repeat(np.exp((t[:, None] + 1.0) * log_g[None, :]), HEAD_DIM, 1),
        "gk": np.exp((c - 1.0 - t)[:, None] * log_g[None, :])[:, hid],
        "gs": np.exp(c * log_g)[hid][:, None] * np.ones((1, HW)),
        "bd": bd,
        "hm": (row_h[:, None] == hid[None, :]).astype(np.float64),
        "e64": bd / HEAD_DIM,
    }

    n_lv = int(round(math.log2(c // LEVEL0_ROWS)))
    tri = (t[None, :] <= t[:, None]).astype(np.float64)
    signs = []
    lvl = np.full((c, c), -1.0)
    causal = t[None, :] <= t[:, None]
    same0 = (t[:, None] // LEVEL0_ROWS) == (t[None, :] // LEVEL0_ROWS)
    lvl[causal & same0] = 0.0
    assigned = same0.copy()
    for lv in range(1, n_lv + 1):
        g = LEVEL0_ROWS * 2 ** lv
        upper = (t % g) >= g // 2
        signs.append(np.where(upper, 1.0, -1.0)[:, None] * np.ones((1, HW)))
        same = (t[:, None] // g) == (t[None, :] // g)
        lvl[causal & same & ~assigned] = float(lv)
        assigned |= same
    tabs["tri3"] = np.concatenate([tri, tri, tri], axis=1)
    tabs["lsgn"] = np.concatenate(signs, axis=0)
    tabs["lvl"] = np.tile(lvl, (1, HEADS))
    return tabs, n_lv


_BF16_TABLES = ("hm", "tri3", "e64")
_TABLE_ORDER = ("dall", "gq", "gk", "gs", "bd", "hm", "tri3", "lsgn", "lvl", "e64")


def _rope_tables(pos):
    half = HEAD_DIM // 2
    inv = ROPE_BASE ** (-np.arange(half, dtype=np.float64) / half)
    ang = pos.astype(np.float64)[:, None] * inv[None, :]
    cos = np.tile(np.cos(ang), (1, 2 * LANES // HEAD_DIM))
    sin = np.tile(np.concatenate([-np.sin(ang), np.sin(ang)], axis=1), (1, LANES // HEAD_DIM))
    return cos, sin


def _silu(x):
    return x * jax.nn.sigmoid(x)


def _dot(a, b):
    return jnp.dot(a, b, preferred_element_type=F32)


def _dot_t(a, b):
    return lax.dot_general(a, b, (((1,), (1,)), ((), ())), preferred_element_type=F32)


def _tdot(a, b):
    return lax.dot_general(a, b, (((0,), (0,)), ((), ())), preferred_element_type=F32)


def _seg_means(xs, e64):
    rows = xs[0].shape[0]
    x = xs[0] if len(xs) == 1 else jnp.concatenate(xs, axis=0)
    m = _dot(x.astype(BF16), e64)
    return [m[i * rows:(i + 1) * rows] for i in range(len(xs))]


def _stack_heads(x, mask):
    xb = x.astype(BF16)
    return jnp.concatenate([xb] * HEADS, axis=0) * mask


def _rotary(x, cos, sin, first_half):
    partner = jnp.where(first_half, pltpu.roll(x, HW - HEAD_DIM // 2, axis=1), pltpu.roll(x, HEAD_DIM // 2, axis=1))
    return x * cos + partner * sin


def _rg_scan(a, b, h_prev):
    c, w = a.shape
    g = c // SUBLANES
    a3 = a.reshape(g, SUBLANES, w)
    b3 = b.reshape(g, SUBLANES, w)
    sub = lax.broadcasted_iota(jnp.int32, a3.shape, 1)
    shift = 1
    while shift < SUBLANES:
        keep = sub >= shift
        a_sh = pltpu.roll(a3, shift, axis=1)
        b_sh = pltpu.roll(b3, shift, axis=1)
        b3 = jnp.where(keep, a3 * b_sh + b3, b3)
        a3 = jnp.where(keep, a3 * a_sh, a3)
        shift *= 2
    outs = []
    hb = h_prev
    for gi in range(g):
        hg = a3[gi] * hb + b3[gi]
        outs.append(hg)
        hb = jnp.broadcast_to(hg[SUBLANES - 1:SUBLANES, :], (SUBLANES, w))
    return jnp.concatenate(outs, axis=0), hb


def _group_row(x, g, r):
    return jnp.concatenate([jnp.broadcast_to(x[g0 + r:g0 + r + 1, :], (g, x.shape[1]))
                            for g0 in range(0, x.shape[0], g)], axis=0)


def _rows(start, size):
    return pl.ds(start if isinstance(start, int) else pl.multiple_of(start, size), size)


def _update_state(ref, bi, old, scale, mask_ref, row_op, col_op):
    for rs, cs in QUADS:
        sc = scale[rs, cs] if scale.shape[0] > 1 else scale[:, cs]
        ref[bi, rs, cs] = sc * old[rs, cs] + mask_ref[rs, cs] * _tdot(row_op[:, rs], col_op[:, cs])


def _head_transpose(s):
    return jnp.concatenate([s[h * HEAD_DIM:(h + 1) * HEAD_DIM, :].T for h in range(HEADS)], axis=0)


def _expand_state(s, bd):
    return jnp.concatenate([s] * HEADS, axis=1) * bd


def _compact_state(s):
    out = s[:, 0:HEAD_DIM]
    for h in range(1, HEADS):
        out = out + s[:, h * HEAD_DIM:(h + 1) * HEAD_DIM]
    return out


def _layer_kernel(cfg, *refs):
    layer, depth, bt, tile, c, n_lv, blk, lookahead, zero_init = cfg
    n = bt * tile
    nb = n // blk
    it = iter(refs)
    x_ref = next(it)
    xn_ref = next(it) if lookahead else None
    cos_ref, sin_ref = next(it), next(it)
    conv_in, h_in, sret_in, shg_in = (None,) * 4 if zero_init else (next(it), next(it), next(it), next(it))
    (win_ref, wa_ref, wx_ref, wout_ref, convw_ref, convb_ref, rgba_ref, rgbx_ref, rglam_ref, gng_ref, hgbf_ref,
     hgng_ref, lbl_ref, lng_ref, lnb_ref) = [next(it) for _ in range(15)]
    (dall_ref, gq_ref, gk_ref, gs_ref, bd_ref, hm_ref, tri3_ref, lsgn_ref, lvl_ref,
     e64_ref) = [next(it) for _ in range(10)]
    xo_ref, conv_o, h_o, sret_o, shg_o = [next(it) for _ in range(5)]
    scr = list(it)
    sets = (scr[0:11], scr[11:22])
    conv_c, h_c, sret_c, shg_c = scr[22:26]
    wg_s = scr[26]
    seg_rows = blk if bt == 1 else tile
    dn_alpha = (2 * depth) ** 0.25

    @pl.when(pl.program_id(1) == 0)
    def _load_state():
        if zero_init:
            for ref in (conv_c, h_c, sret_c, shg_c):
                ref[...] = jnp.zeros(ref.shape, F32)
        else:
            conv_c[:, 0:SUBLANES - (CONV_W - 1), :] = jnp.zeros((bt, SUBLANES - (CONV_W - 1), RG_WIDTH), F32)
            conv_c[:, SUBLANES - (CONV_W - 1):SUBLANES, :] = conv_in[...]
            h_c[...] = jnp.broadcast_to(h_in[...], (bt, SUBLANES, RG_WIDTH))
            for bi in range(bt):
                sret_c[bi] = _expand_state(sret_in[bi], bd_ref[...])
                shg_c[bi] = _expand_state(_head_transpose(shg_in[bi]), bd_ref[...])

    @pl.when((pl.program_id(0) == 0) & (pl.program_id(1) == 0))
    def _expand_gate_weights():
        for gi, ref in enumerate((wa_ref, wx_ref)):
            for hf in range(2):
                wg_s[gi, hf] = _expand_state(ref[hf], bd_ref[...]).astype(BF16)

    conv_b = convb_ref[...]
    conv_w = [convw_ref[j:j + 1, :] for j in range(CONV_W)]
    rg_ba = rgba_ref[...]
    rg_bx = rgbx_ref[...]
    c_lam = RG_C * jax.nn.log_sigmoid(rglam_ref[...])
    gn_g = gng_ref[...]
    hg_bf = hgbf_ref[...]
    hg_ng = hgng_ref[...]
    logits = [lbl_ref[li:li + 1, :] for li in range(depth)]
    mx = functools.reduce(jnp.maximum, logits)
    ex = [jnp.exp(v - mx) for v in logits]
    den = functools.reduce(lambda s, v: s + v, ex)
    probs = [v / den for v in ex]
    lb = functools.reduce(lambda s, v: s + v, probs[:layer + 1]) - probs[0]
    one_m_lb = 1.0 - lb

    def project_steps(xblk, dst):
        xb_s = dst[10]

        def stage():
            xb_s[...] = xblk.astype(BF16)

        def step(lo):
            def run():
                val = _dot(xb_s[...], win_ref[:, lo:lo + PROJ_COLS])
                for gi in range(2 + N_NARROW):
                    g_lo, g_hi = max(SEG[gi], lo), min(SEG[gi + 1], lo + PROJ_COLS)
                    if g_lo >= g_hi:
                        continue
                    part = val[:, g_lo - lo:g_hi - lo]
                    if gi == 0:
                        for si in range(blk // seg_rows):
                            dst[0][si, SUBLANES:SUBLANES + seg_rows, g_lo:g_hi] = part[si * seg_rows:(si + 1) * seg_rows]
                    else:
                        dst[gi][:, g_lo - SEG[gi]:g_hi - SEG[gi]] = part
            return run

        return [stage] + [step(lo) for lo in range(0, D_IN, PROJ_COLS)]

    def mix(src, base, filler=()):
        filler = list(filler)
        n_ch = blk // c
        lockstep = n_ch if bt > 1 else min(n_ch, LOCKSTEP)
        n_grp = -(-n_ch // lockstep)
        fill_total = 2.0 + 4 * n_ch + 2 * n_grp
        issued = [0.0, 0]

        def fill(weight):
            issued[0] += weight
            due = math.ceil(len(filler) * min(issued[0] / fill_total, 1.0) - 1e-9)
            while issued[1] < due:
                filler[issued[1]]()
                issued[1] += 1

        rgx_s, rgg_s, q_s, k_s, v_s, rgate_s, hq_s, hf_s, hi_s, hgate_s = src[:10]
        rgx_s[:, 0:SUBLANES, :] = conv_c[...]
        us = []
        for si in range(blk // seg_rows):
            win = rgx_s[si]
            u_seg = conv_b + win[SUBLANES:, :] * conv_w[CONV_W - 1]
            for back in range(1, CONV_W):
                u_seg = u_seg + pltpu.roll(win, back, axis=0)[SUBLANES:, :] * conv_w[CONV_W - 1 - back]
            us.append(u_seg)
        conv_c[...] = rgx_s[:, seg_rows:seg_rows + SUBLANES, :]
        u = us[0] if len(us) == 1 else jnp.concatenate(us, axis=0)
        fill(1.0)
        ub = u.astype(BF16)
        half = RG_WIDTH // 2
        r_pre = jnp.concatenate([_dot(ub[:, :half], wg_s[0, 0]), _dot(ub[:, half:], wg_s[0, 1])], axis=1)
        i_pre = jnp.concatenate([_dot(ub[:, :half], wg_s[1, 0]), _dot(ub[:, half:], wg_s[1, 1])], axis=1)
        log_a = c_lam * jax.nn.sigmoid(r_pre + rg_ba)
        a = jnp.exp(log_a)
        b_in = jnp.sqrt(-jnp.tanh(log_a) * (a * a + 1.0)) * (jax.nn.sigmoid(i_pre + rg_bx) * u)
        fill(1.0)
        e64 = e64_ref[...]
        hm = hm_ref[...]
        lvl = lvl_ref[...]
        first_half = (lax.broadcasted_iota(jnp.int32, (c, HW), 1) % HEAD_DIM) < HEAD_DIM // 2

        per_seq = seg_rows // c
        st_all = [dict(rows=slice(ci * c, (ci + 1) * c), bi=ci // per_seq,
                       trow=_rows(base + (ci % per_seq) * c, c)) for ci in range(n_ch)]

        for g0 in range(0, n_ch, lockstep):
            st = st_all[g0:g0 + lockstep]
            for d in st:
                rows, trow = d["rows"], d["trow"]
                cos = jnp.concatenate([cos_ref[trow, :]] * (HW // LANES), axis=1)
                sin = jnp.concatenate([sin_ref[trow, :]] * (HW // LANES), axis=1)
                d["kr"] = _rotary(k_s[rows, :], cos, sin, first_half)
                d["qb"] = _rotary(q_s[rows, :], cos, sin, first_half).astype(BF16)
                d["scores"] = _dot_t(d["qb"], _stack_heads(d["kr"], hm))
                z = hf_s[rows, :] + hg_bf
                ez = jnp.exp(-jnp.abs(z))
                inv = 1.0 / (1.0 + ez)
                pos = z >= 0.0
                sig_p = jnp.where(pos, inv, ez * inv)
                sig_n = jnp.where(pos, ez * inv, inv)
                log_f = jnp.log(jnp.maximum(lb + one_m_lb * sig_p, F_EPS))
                d["kc"] = one_m_lb * sig_n
                f_hi = log_f.astype(BF16)
                res = log_f - f_hi.astype(F32)
                f_mid = res.astype(BF16)
                f_lo = (res - f_mid.astype(F32)).astype(BF16)
                d["cum"] = _dot(tri3_ref[...], jnp.concatenate([f_hi, f_mid, f_lo], axis=0))
                fill(1.0)

            for d in st:
                bi, kr, qb = d["bi"], d["kr"], d["qb"]
                v = v_s[d["rows"], :]
                p = (d["scores"] * dall_ref[...]).astype(BF16)
                s_ret = sret_c[bi]
                d["o_b"] = _dot(p, _stack_heads(v, hm)) + _dot(qb, s_ret.astype(BF16)) * gq_ref[...]
                _update_state(sret_c, bi, s_ret, gs_ref, bd_ref, (kr * gk_ref[...]).astype(BF16), v.astype(BF16))
                fill(1.0)
                cum, kc = d["cum"], d["kc"]
                qh = hq_s[d["rows"], :]
                e0 = jnp.exp(cum - _group_row(cum, LEVEL0_ROWS, LEVEL0_ROWS // 2 - 1))
                s0 = _dot_t((qh * e0).astype(BF16), _stack_heads(kc / e0, hm))
                pm = jnp.where(lvl == 0.0, s0, 0.0)
                for lv in range(1, n_lv + 1):
                    g = LEVEL0_ROWS * 2 ** lv
                    el = jnp.exp(lsgn_ref[(lv - 1) * c:lv * c, :] * (cum - _group_row(cum, g, g // 2 - 1)))
                    sl_ = _dot_t((qh * el).astype(BF16), _stack_heads(kc * el, hm))
                    pm = jnp.where(lvl == float(lv), sl_, pm)
                d["pm"] = pm
                fill(1.0)

            means = _seg_means([d["o_b"] for d in st], e64)
            for d, m in zip(st, means):
                d["dev"] = d["o_b"] - m
            fill(1.0)
            for d in st:
                bi, cum = d["bi"], d["cum"]
                vh = hi_s[d["rows"], :]
                qh = hq_s[d["rows"], :]
                s_hg = shg_c[bi]
                d["o_c"] = (_dot(d["pm"].astype(BF16), _stack_heads(vh, hm))
                            + _dot_t((qh * jnp.exp(cum)).astype(BF16), s_hg.astype(BF16)))
                k_st = (d["kc"] * jnp.exp(cum[c - 1:c, :] - cum)).astype(BF16)
                dec = jnp.exp(cum[c - 1:c, :])
                _update_state(shg_c, bi, s_hg, dec, bd_ref, vh.astype(BF16), k_st)
                fill(1.0)

            stats = _seg_means([d["dev"] * d["dev"] for d in st] + [d["o_c"] * d["o_c"] for d in st], e64)
            fill(1.0)
            for d, var_b, ms_c in zip(st, stats[:len(st)], stats[len(st):]):
                d["var_b"], d["ms_c"] = var_b, ms_c

        ys = []
        for d in st_all:
            rows, bi = d["rows"], d["bi"]
            h, h_last = _rg_scan(a[rows], b_in[rows], h_c[bi])
            h_c[bi] = h_last
            y_a = h * _silu(rgg_s[rows, :])
            y_b = d["dev"] * lax.rsqrt(d["var_b"] + LN_EPS) * gn_g * _silu(rgate_s[rows, :])
            y_c = d["o_c"] * lax.rsqrt(d["ms_c"] + LN_EPS) * hg_ng * _silu(hgate_s[rows, :])
            ys.append(jnp.concatenate([y_a.astype(BF16), y_b.astype(BF16), y_c.astype(BF16)], axis=1))
        y = ys[0] if n_ch == 1 else jnp.concatenate(ys, axis=0)

        assert abs(issued[0] - fill_total) < 1e-6 and issued[1] == len(filler)

        if bt == 1:
            xrows = _rows(base, blk)
            x_in = x_ref[0, xrows, :]
        else:
            x_in = x_ref[...].reshape(n, D_MODEL)
        xn = dn_alpha * x_in + _dot(y, wout_ref[...])
        mu = jnp.mean(xn, axis=-1, keepdims=True)
        dv = xn - mu
        var = jnp.mean(dv * dv, axis=-1, keepdims=True)
        out = dv * lax.rsqrt(var + LN_EPS) * lng_ref[...] + lnb_ref[...]
        if bt == 1:
            xo_ref[0, xrows, :] = out
        else:
            xo_ref[...] = out.reshape(bt, tile, D_MODEL)

    def project(xblk, dst):
        for run in project_steps(xblk, dst):
            run()

    if not lookahead:
        project(x_ref[...].reshape(n, D_MODEL), sets[0])
        mix(sets[0], 0)
    else:
        @pl.when((pl.program_id(0) == 0) & (pl.program_id(1) == 0))
        def _first_block():
            project(x_ref[0, 0:blk, :], sets[0])

        def pair_body(k, carry):
            base0 = pl.multiple_of(2 * k * blk, 2 * blk)
            mix(sets[0], base0, project_steps(x_ref[0, _rows(base0 + blk, blk), :], sets[1]))
            in_tile = 2 * k + 2 < nb
            nxt = jnp.minimum(2 * k + 2, nb - 1) * blk
            ahead = jnp.where(in_tile, x_ref[0, _rows(nxt, blk), :], xn_ref[0])
            mix(sets[1], base0 + blk, project_steps(ahead, sets[0]))
            return carry

        lax.fori_loop(0, nb // 2, pair_body, 0)

    @pl.when(pl.program_id(1) == pl.num_programs(1) - 1)
    def _store_state():
        conv_o[...] = conv_c[:, SUBLANES - (CONV_W - 1):SUBLANES, :]
        h_o[...] = h_c[:, 0:1, :]
        for bi in range(bt):
            sret_o[bi] = _compact_state(sret_c[bi])
            shg_o[bi] = _head_transpose(_compact_state(shg_c[bi]))


def _tiling(batch, seq):
    chunk = min(MAX_CHUNK, seq)
    tile = min(seq, MAX_TILE)
    bt = 1 if tile >= BLOCK_ROWS else min(batch, BLOCK_ROWS // tile)
    blk = min(BLOCK_ROWS, bt * tile)
    lookahead = bt == 1 and (tile // blk) >= 2
    assert seq % tile == 0 and batch % bt == 0 and tile % chunk == 0 and chunk % 16 == 0 and blk % chunk == 0
    assert lookahead or bt * tile == blk
    assert not lookahead or (tile // blk) % 2 == 0
    return bt, tile, chunk, blk, lookahead


def _layer_call(layer, depth, x, rope, states, params, tabs, n_lv, tiling):
    bt, tile, chunk, blk, lookahead = tiling
    batch, seq, _ = x.shape
    n_tiles = seq // tile
    grid = (batch // bt, n_tiles)
    n_steps = grid[0] * grid[1]
    zero_init = states is None
    kern = functools.partial(_layer_kernel, (layer, depth, bt, tile, chunk, n_lv, blk, lookahead, zero_init))

    def layer_const(arr):
        tail = arr.shape[1:]
        return pl.BlockSpec((None,) + tail, lambda b, i: (layer,) + (0,) * len(tail))

    def const(arr):
        zeros = (0,) * arr.ndim
        return pl.BlockSpec(arr.shape, lambda b, i: zeros)

    def per_b(tail):
        return pl.BlockSpec((bt,) + tail, lambda b, i: (b,) + (0,) * len(tail))

    def per_lb(tail):
        return pl.BlockSpec((None, bt) + tail, lambda b, i: (layer, b) + (0,) * len(tail))

    def next_block(b, i):
        f = jnp.minimum(b * n_tiles + i + 1, n_steps - 1)
        return (f // n_tiles, (f % n_tiles) * (tile // blk), 0)

    state_tails = ((CONV_W - 1, RG_WIDTH), (1, RG_WIDTH), (HW, HEAD_DIM), (HW, HEAD_DIM))
    x_spec = pl.BlockSpec((bt, tile, D_MODEL), lambda b, i: (b, i, 0))
    rope_spec = pl.BlockSpec((tile, LANES), lambda b, i: (i, 0))
    args, specs = [x], [x_spec]
    if lookahead:
        args.append(x)
        specs.append(pl.BlockSpec((1, blk, D_MODEL), next_block))
    args += list(rope)
    specs += [rope_spec] * 2
    if not zero_init:
        args += list(states)
        specs += [per_lb(t) for t in state_tails]
    args += list(params)
    specs += [const(p) if i == 12 else layer_const(p) for i, p in enumerate(params)]
    tab_args = [tabs[k] for k in _TABLE_ORDER]
    args += tab_args
    specs += [const(t) for t in tab_args]

    out_shape = [jax.ShapeDtypeStruct(x.shape, F32)] + [jax.ShapeDtypeStruct((batch,) + t, F32) for t in state_tails]
    out_specs = [x_spec] + [per_b(t) for t in state_tails]
    seg_rows = blk if bt == 1 else tile
    one_set = ([pltpu.VMEM((blk // seg_rows, SUBLANES + seg_rows, RG_WIDTH), F32), pltpu.VMEM((blk, RG_WIDTH), F32)]
               + [pltpu.VMEM((blk, HW), F32)] * N_NARROW + [pltpu.VMEM((blk, D_MODEL), BF16)])
    carries = [pltpu.VMEM((bt, SUBLANES, RG_WIDTH), F32)] * 2 + [pltpu.VMEM((bt, HW, HW), F32)] * 2
    gate_w = [pltpu.VMEM((2, 2, HW, HW), BF16)]
    return pl.pallas_call(
        kern, grid=grid, in_specs=specs, out_specs=out_specs, out_shape=out_shape,
        scratch_shapes=one_set * 2 + carries + gate_w,
        compiler_params=pltpu.CompilerParams(dimension_semantics=("arbitrary", "arbitrary"),
                                             vmem_limit_bytes=VMEM_LIMIT_BYTES),
        name=f"layer{layer}_t{tile}",
    )(*args)


def kernel(x_prompt, x_sample, cache_conv, state_rglru, state_ret, state_hgrn, w_in, conv_w, conv_b, rg_wa, rg_ba,
           rg_wx, rg_bx, rg_lambda, ret_gn_g, hg_bf, hg_lb_logits, hg_norm_g, w_out, ln_g, ln_b):
    depth = w_in.shape[0]
    bp, lp, _ = x_prompt.shape
    bs, ls, _ = x_sample.shape

    streams = []
    for batch, seq, pos0 in ((bp, lp, 0), (bs, ls, PAST_LEN)):
        tiling = _tiling(batch, seq)
        tabs_np, n_lv = _mixer_tables(tiling[2])
        tabs = {k: jnp.asarray(v, BF16 if k in _BF16_TABLES else F32) for k, v in tabs_np.items()}
        rope = tuple(jnp.asarray(t, F32) for t in _rope_tables(pos0 + np.arange(seq)))
        streams.append((tiling, n_lv, tabs, rope))

    row = lambda p: p[:, None, :]
    halves = lambda w: w.reshape(depth, 2, HW, RG_BLOCK)
    params = (w_in.astype(BF16), halves(rg_wa), halves(rg_wx), w_out.astype(BF16), conv_w, row(conv_b), row(rg_ba),
              row(rg_bx), row(rg_lambda), row(ret_gn_g), row(hg_bf), row(hg_norm_g), hg_lb_logits, row(ln_g),
              row(ln_b))
    sample_states = (cache_conv, state_rglru[:, :, None, :], state_ret.reshape(depth, bs, HW, HEAD_DIM),
                     state_hgrn.reshape(depth, bs, HW, HEAD_DIM))

    xp, xs = x_prompt, x_sample
    outs_p, outs_s = [], []
    for l in range(depth):
        tiling, n_lv, tabs, rope = streams[0]
        res = _layer_call(l, depth, xp, rope, None, params, tabs, n_lv, tiling)
        xp = res[0]
        outs_p.append(res[1:])
        tiling, n_lv, tabs, rope = streams[1]
        res = _layer_call(l, depth, xs, rope, sample_states, params, tabs, n_lv, tiling)
        xs = res[0]
        outs_s.append(res[1:])

    def collect(outs, batch):
        conv = jnp.stack([o[0] for o in outs])
        h = jnp.stack([o[1][:, 0, :] for o in outs])
        ret = jnp.stack([o[2] for o in outs]).reshape(depth, batch, HEADS, HEAD_DIM, HEAD_DIM)
        hg = jnp.stack([o[3] for o in outs]).reshape(depth, batch, HEADS, HEAD_DIM, HEAD_DIM)
        return conv, h, ret, hg

    return (xp, xs) + collect(outs_p, bp) + collect(outs_s, bs)
```

```python
import functools
import math

import jax
import jax.numpy as jnp
import numpy as np
from jax import lax
from jax.experimental import pallas as pl
from jax.experimental.pallas import tpu as pltpu

F32 = jnp.float32
BF16 = jnp.bfloat16

D_MODEL = 1024
RG_WIDTH = 512
RG_BLOCKS = 8
RG_BLOCK = RG_WIDTH // RG_BLOCKS
CONV_W = 4
RG_C = 8.0
HEADS = 4
HEAD_DIM = 64
HW = HEADS * HEAD_DIM
ROPE_BASE = 10000.0
LN_EPS = 1e-5
F_EPS = 1e-6
PAST_LEN = 1024
SPLITS = (RG_WIDTH, RG_WIDTH, HW, HW, HW, HW, HW, HW, HW, HW)
D_IN = sum(SPLITS)
SEG = tuple(int(v) for v in np.cumsum((0,) + SPLITS))
N_NARROW = 8

SUBLANES = 8
LANES = 128
VMEM_LIMIT_BYTES = 56 * 1024 * 1024

MAX_CHUNK = 64
BLOCK_ROWS = 256
PROJ_COLS = 256
MAX_TILE = 1024
LOCKSTEP = 1
LEVEL0_ROWS = SUBLANES

QUADS = ((slice(0, 128), slice(0, 128)), (slice(128, 256), slice(128, 256)))


def _mixer_tables(chunk):
    c = chunk
    t = np.arange(c)
    hid = np.repeat(np.arange(HEADS), HEAD_DIM)
    row_h = np.repeat(np.arange(HEADS), c)
    scale = HEAD_DIM ** -0.5

    log_g = np.log1p(-np.exp2(-5.0 - np.arange(HEADS)))
    rel = t[:, None] - t[None, :]
    dmat = np.where(rel >= 0, np.exp(np.maximum(rel, 0)[None] * log_g[:, None, None]), 0.0)
    bd = (hid[:, None] == hid[None, :]).astype(np.float64)
    tabs = {
        "dall": scale * dmat.transpose(1, 0, 2).reshape(c, HEADS * c),
        "gq": scale * np.repeat(np.exp((t[:, None] + 1.0) * log_g[None, :]), HEAD_DIM, 1),
        "gk": np.exp((c - 1.0 - t)[:, None] * log_g[None, :])[:, hid],
        "gs": np.exp(c * log_g)[hid][:, None] * np.ones((1, HW)),
        "bd": bd,
        "hm": (row_h[:, None] == hid[None, :]).astype(np.float64),
        "e64": bd / HEAD_DIM,
    }

    n_lv = int(round(math.log2(c // LEVEL0_ROWS)))
    tri = (t[None, :] <= t[:, None]).astype(np.float64)
    signs = []
    lvl = np.full((c, c), -1.0)
    causal = t[None, :] <= t[:, None]
    same0 = (t[:, None] // LEVEL0_ROWS) == (t[None, :] // LEVEL0_ROWS)
    lvl[causal & same0] = 0.0
    assigned = same0.copy()
    for lv in range(1, n_lv + 1):
        g = LEVEL0_ROWS * 2 ** lv
        upper = (t % g) >= g // 2
        signs.append(np.where(upper, 1.0, -1.0)[:, None] * np.ones((1, HW)))
        same = (t[:, None] // g) == (t[None, :] // g)
        lvl[causal & same & ~assigned] = float(lv)
        assigned |= same
    tabs["tri3"] = np.concatenate([tri, tri, tri], axis=1)
    tabs["lsgn"] = np.concatenate(signs, axis=0)
    tabs["lvl"] = np.tile(lvl, (1, HEADS))
    return tabs, n_lv


_BF16_TABLES = ("hm", "tri3", "e64")
_TABLE_ORDER = ("dall", "gq", "gk", "gs", "bd", "hm", "tri3", "lsgn", "lvl", "e64")


def _rope_tables(pos):
    half = HEAD_DIM // 2
    inv = ROPE_BASE ** (-np.arange(half, dtype=np.float64) / half)
    ang = pos.astype(np.float64)[:, None] * inv[None, :]
    cos = np.tile(np.cos(ang), (1, 2 * LANES // HEAD_DIM))
    sin = np.tile(np.concatenate([-np.sin(ang), np.sin(ang)], axis=1), (1, LANES // HEAD_DIM))
    return cos, sin


def _silu(x):
    return x * jax.nn.sigmoid(x)


def _dot(a, b):
    return jnp.dot(a, b, preferred_element_type=F32)


def _dot_t(a, b):
    return lax.dot_general(a, b, (((1,), (1,)), ((), ())), preferred_element_type=F32)


def _tdot(a, b):
    return lax.dot_general(a, b, (((0,), (0,)), ((), ())), preferred_element_type=F32)


def _seg_means(xs, e64):
    rows = xs[0].shape[0]
    x = xs[0] if len(xs) == 1 else jnp.concatenate(xs, axis=0)
    m = _dot(x.astype(BF16), e64)
    return [m[i * rows:(i + 1) * rows] for i in range(len(xs))]


def _stack_heads(x, mask):
    xb = x.astype(BF16)
    return jnp.concatenate([xb] * HEADS, axis=0) * mask


def _rotary(x, cos, sin, first_half):
    partner = jnp.where(first_half, pltpu.roll(x, HW - HEAD_DIM // 2, axis=1), pltpu.roll(x, HEAD_DIM // 2, axis=1))
    return x * cos + partner * sin


def _rg_scan(a, b, h_prev):
    c, w = a.shape
    g = c // SUBLANES
    a3 = a.reshape(g, SUBLANES, w)
    b3 = b.reshape(g, SUBLANES, w)
    sub = lax.broadcasted_iota(jnp.int32, a3.shape, 1)
    shift = 1
    while shift < SUBLANES:
        keep = sub >= shift
        a_sh = pltpu.roll(a3, shift, axis=1)
        b_sh = pltpu.roll(b3, shift, axis=1)
        b3 = jnp.where(keep, a3 * b_sh + b3, b3)
        a3 = jnp.where(keep, a3 * a_sh, a3)
        shift *= 2
    outs = []
    hb = h_prev
    for gi in range(g):
        hg = a3[gi] * hb + b3[gi]
        outs.append(hg)
        hb = jnp.broadcast_to(hg[SUBLANES - 1:SUBLANES, :], (SUBLANES, w))
    return jnp.concatenate(outs, axis=0), hb


def _group_row(x, g, r):
    return jnp.concatenate([jnp.broadcast_to(x[g0 + r:g0 + r + 1, :], (g, x.shape[1]))
                            for g0 in range(0, x.shape[0], g)], axis=0)


def _rows(start, size):
    return pl.ds(start if isinstance(start, int) else pl.multiple_of(start, size), size)


def _update_state(ref, bi, old, scale, mask_ref, row_op, col_op):
    for rs, cs in QUADS:
        sc = scale[rs, cs] if scale.shape[0] > 1 else scale[:, cs]
        ref[bi, rs, cs] = sc * old[rs, cs] + mask_ref[rs, cs] * _tdot(row_op[:, rs], col_op[:, cs])


def _head_transpose(s):
    return jnp.concatenate([s[h * HEAD_DIM:(h + 1) * HEAD_DIM, :].T for h in range(HEADS)], axis=0)


def _expand_state(s, bd):
    return jnp.concatenate([s] * HEADS, axis=1) * bd


def _compact_state(s):
    out = s[:, 0:HEAD_DIM]
    for h in range(1, HEADS):
        out = out + s[:, h * HEAD_DIM:(h + 1) * HEAD_DIM]
    return out


def _layer_kernel(cfg, *refs):
    layer, depth, bt, tile, c, n_lv, blk, lookahead, zero_init = cfg
    n = bt * tile
    nb = n // blk
    it = iter(refs)
    x_ref = next(it)
    xn_ref = next(it) if lookahead else None
    cos_ref, sin_ref = next(it), next(it)
    conv_in, h_in, sret_in, shg_in = (None,) * 4 if zero_init else (next(it), next(it), next(it), next(it))
    (win_ref, wa_ref, wx_ref, wout_ref, convw_ref, convb_ref, rgba_ref, rgbx_ref, rglam_ref, gng_ref, hgbf_ref,
     hgng_ref, lbl_ref, lng_ref, lnb_ref) = [next(it) for _ in range(15)]
    (dall_ref, gq_ref, gk_ref, gs_ref, bd_ref, hm_ref, tri3_ref, lsgn_ref, lvl_ref,
     e64_ref) = [next(it) for _ in range(10)]
    xo_ref, conv_o, h_o, sret_o, shg_o = [next(it) for _ in range(5)]
    scr = list(it)
    sets = (scr[0:11], scr[11:22])
    conv_c, h_c, sret_c, shg_c = scr[22:26]
    wg_s = scr[26]
    seg_rows = blk if bt == 1 else tile
    dn_alpha = (2 * depth) ** 0.25

    @pl.when(pl.program_id(1) == 0)
    def _load_state():
        if zero_init:
            for ref in (conv_c, h_c, sret_c, shg_c):
                ref[...] = jnp.zeros(ref.shape, F32)
        else:
            conv_c[:, 0:SUBLANES - (CONV_W - 1), :] = jnp.zeros((bt, SUBLANES - (CONV_W - 1), RG_WIDTH), F32)
            conv_c[:, SUBLANES - (CONV_W - 1):SUBLANES, :] = conv_in[...]
            h_c[...] = jnp.broadcast_to(h_in[...], (bt, SUBLANES, RG_WIDTH))
            for bi in range(bt):
                sret_c[bi] = _expand_state(sret_in[bi], bd_ref[...])
                shg_c[bi] = _expand_state(_head_transpose(shg_in[bi]), bd_ref[...])

    @pl.when((pl.program_id(0) == 0) & (pl.program_id(1) == 0))
    def _expand_gate_weights():
        for gi, ref in enumerate((wa_ref, wx_ref)):
            for hf in range(2):
                wg_s[gi, hf] = _expand_state(ref[hf], bd_ref[...]).astype(BF16)

    conv_b = convb_ref[...]
    conv_w = [convw_ref[j:j + 1, :] for j in range(CONV_W)]
    rg_ba = rgba_ref[...]
    rg_bx = rgbx_ref[...]
    c_lam = RG_C * jax.nn.log_sigmoid(rglam_ref[...])
    gn_g = gng_ref[...]
    hg_bf = hgbf_ref[...]
    hg_ng = hgng_ref[...]
    logits = [lbl_ref[li:li + 1, :] for li in range(depth)]
    mx = functools.reduce(jnp.maximum, logits)
    ex = [jnp.exp(v - mx) for v in logits]
    den = functools.reduce(lambda s, v: s + v, ex)
    probs = [v / den for v in ex]
    lb = functools.reduce(lambda s, v: s + v, probs[:layer + 1]) - probs[0]
    one_m_lb = 1.0 - lb

    def project_steps(xblk, dst):
        xb_s = dst[10]

        def stage():
            xb_s[...] = xblk.astype(BF16)

        def step(lo):
            def run():
                val = _dot(xb_s[...], win_ref[:, lo:lo + PROJ_COLS])
                for gi in range(2 + N_NARROW):
                    g_lo, g_hi = max(SEG[gi], lo), min(SEG[gi + 1], lo + PROJ_COLS)
                    if g_lo >= g_hi:
                        continue
                    part = val[:, g_lo - lo:g_hi - lo]
                    if gi == 0:
                        for si in range(blk // seg_rows):
                            dst[0][si, SUBLANES:SUBLANES + seg_rows, g_lo:g_hi] = part[si * seg_rows:(si + 1) * seg_rows]
                    else:
                        dst[gi][:, g_lo - SEG[gi]:g_hi - SEG[gi]] = part
            return run

        return [stage] + [step(lo) for lo in range(0, D_IN, PROJ_COLS)]

    def mix(src, base, filler=()):
        filler = list(filler)
        n_ch = blk // c
        lockstep = n_ch if bt > 1 else min(n_ch, LOCKSTEP)
        n_grp = -(-n_ch // lockstep)
        fill_total = 2.0 + 4 * n_ch + 2 * n_grp
        issued = [0.0, 0]

        def fill(weight):
            issued[0] += weight
            due = math.ceil(len(filler) * min(issued[0] / fill_total, 1.0) - 1e-9)
            while issued[1] < due:
                filler[issued[1]]()
                issued[1] += 1

        rgx_s, rgg_s, q_s, k_s, v_s, rgate_s, hq_s, hf_s, hi_s, hgate_s = src[:10]
        rgx_s[:, 0:SUBLANES, :] = conv_c[...]
        us = []
        for si in range(blk // seg_rows):
            win = rgx_s[si]
            u_seg = conv_b + win[SUBLANES:, :] * conv_w[CONV_W - 1]
            for back in range(1, CONV_W):
                u_seg = u_seg + pltpu.roll(win, back, axis=0)[SUBLANES:, :] * conv_w[CONV_W - 1 - back]
            us.append(u_seg)
        conv_c[...] = rgx_s[:, seg_rows:seg_rows + SUBLANES, :]
        u = us[0] if len(us) == 1 else jnp.concatenate(us, axis=0)
        fill(1.0)
        ub = u.astype(BF16)
        half = RG_WIDTH // 2
        r_pre = jnp.concatenate([_dot(ub[:, :half], wg_s[0, 0]), _dot(ub[:, half:], wg_s[0, 1])], axis=1)
        i_pre = jnp.concatenate([_dot(ub[:, :half], wg_s[1, 0]), _dot(ub[:, half:], wg_s[1, 1])], axis=1)
        log_a = c_lam * jax.nn.sigmoid(r_pre + rg_ba)
        a = jnp.exp(log_a)
        b_in = jnp.sqrt(-jnp.tanh(log_a) * (a * a + 1.0)) * (jax.nn.sigmoid(i_pre + rg_bx) * u)
        fill(1.0)
        e64 = e64_ref[...]
        hm = hm_ref[...]
        lvl = lvl_ref[...]
        first_half = (lax.broadcasted_iota(jnp.int32, (c, HW), 1) % HEAD_DIM) < HEAD_DIM // 2

        per_seq = seg_rows // c
        st_all = [dict(rows=slice(ci * c, (ci + 1) * c), bi=ci // per_seq,
                       trow=_rows(base + (ci % per_seq) * c, c)) for ci in range(n_ch)]

        for g0 in range(0, n_ch, lockstep):
            st = st_all[g0:g0 + lockstep]
            for d in st:
                rows, trow = d["rows"], d["trow"]
                cos = jnp.concatenate([cos_ref[trow, :]] * (HW // LANES), axis=1)
                sin = jnp.concatenate([sin_ref[trow, :]] * (HW // LANES), axis=1)
                d["kr"] = _rotary(k_s[rows, :], cos, sin, first_half)
                d["qb"] = _rotary(q_s[rows, :], cos, sin, first_half).astype(BF16)
                d["scores"] = _dot_t(d["qb"], _stack_heads(d["kr"], hm))
                z = hf_s[rows, :] + hg_bf
                ez = jnp.exp(-jnp.abs(z))
                inv = 1.0 / (1.0 + ez)
                pos = z >= 0.0
                sig_p = jnp.where(pos, inv, ez * inv)
                sig_n = jnp.where(pos, ez * inv, inv)
                log_f = jnp.log(jnp.maximum(lb + one_m_lb * sig_p, F_EPS))
                d["kc"] = one_m_lb * sig_n
                f_hi = log_f.astype(BF16)
                res = log_f - f_hi.astype(F32)
                f_mid = res.astype(BF16)
                f_lo = (res - f_mid.astype(F32)).astype(BF16)
                d["cum"] = _dot(tri3_ref[...], jnp.concatenate([f_hi, f_mid, f_lo], axis=0))
                fill(1.0)

            for d in st:
                bi, kr, qb = d["bi"], d["kr"], d["qb"]
                v = v_s[d["rows"], :]
                p = (d["scores"] * dall_ref[...]).astype(BF16)
                s_ret = sret_c[bi]
                d["o_b"] = _dot(p, _stack_heads(v, hm)) + _dot(qb, s_ret.astype(BF16)) * gq_ref[...]
                _update_state(sret_c, bi, s_ret, gs_ref, bd_ref, (kr * gk_ref[...]).astype(BF16), v.astype(BF16))
                fill(1.0)
                cum, kc = d["cum"], d["kc"]
                qh = hq_s[d["rows"], :]
                e0 = jnp.exp(cum - _group_row(cum, LEVEL0_ROWS, LEVEL0_ROWS // 2 - 1))
                s0 = _dot_t((qh * e0).astype(BF16), _stack_heads(kc / e0, hm))
                pm = jnp.where(lvl == 0.0, s0, 0.0)
                for lv in range(1, n_lv + 1):
                    g = LEVEL0_ROWS * 2 ** lv
                    el = jnp.exp(lsgn_ref[(lv - 1) * c:lv * c, :] * (cum - _group_row(cum, g, g // 2 - 1)))
                    sl_ = _dot_t((qh * el).astype(BF16), _stack_heads(kc * el, hm))
                    pm = jnp.where(lvl == float(lv), sl_, pm)
                d["pm"] = pm
                fill(1.0)

            means = _seg_means([d["o_b"] for d in st], e64)
            for d, m in zip(st, means):
                d["dev"] = d["o_b"] - m
            fill(1.0)
            for d in st:
                bi, cum = d["bi"], d["cum"]
                vh = hi_s[d["rows"], :]
                qh = hq_s[d["rows"], :]
                s_hg = shg_c[bi]
                d["o_c"] = (_dot(d["pm"].astype(BF16), _stack_heads(vh, hm))
                            + _dot_t((qh * jnp.exp(cum)).astype(BF16), s_hg.astype(BF16)))
                k_st = (d["kc"] * jnp.exp(cum[c - 1:c, :] - cum)).astype(BF16)
                dec = jnp.exp(cum[c - 1:c, :])
                _update_state(shg_c, bi, s_hg, dec, bd_ref, vh.astype(BF16), k_st)
                fill(1.0)

            stats = _seg_means([d["dev"] * d["dev"] for d in st] + [d["o_c"] * d["o_c"] for d in st], e64)
            fill(1.0)
            for d, var_b, ms_c in zip(st, stats[:len(st)], stats[len(st):]):
                d["var_b"], d["ms_c"] = var_b, ms_c

        ys = []
        for d in st_all:
            rows, bi = d["rows"], d["bi"]
            h, h_last = _rg_scan(a[rows], b_in[rows], h_c[bi])
            h_c[bi] = h_last
            y_a = h * _silu(rgg_s[rows, :])
            y_b = d["dev"] * lax.rsqrt(d["var_b"] + LN_EPS) * gn_g * _silu(rgate_s[rows, :])
            y_c = d["o_c"] * lax.rsqrt(d["ms_c"] + LN_EPS) * hg_ng * _silu(hgate_s[rows, :])
            ys.append(jnp.concatenate([y_a.astype(BF16), y_b.astype(BF16), y_c.astype(BF16)], axis=1))
        y = ys[0] if n_ch == 1 else jnp.concatenate(ys, axis=0)

        assert abs(issued[0] - fill_total) < 1e-6 and issued[1] == len(filler)

        if bt == 1:
            xrows = _rows(base, blk)
            x_in = x_ref[0, xrows, :]
        else:
            x_in = x_ref[...].reshape(n, D_MODEL)
        xn = dn_alpha * x_in + _dot(y, wout_ref[...])
        mu = jnp.mean(xn, axis=-1, keepdims=True)
        dv = xn - mu
        var = jnp.mean(dv * dv, axis=-1, keepdims=True)
        out = dv * lax.rsqrt(var + LN_EPS) * lng_ref[...] + lnb_ref[...]
        if bt == 1:
            xo_ref[0, xrows, :] = out
        else:
            xo_ref[...] = out.reshape(bt, tile, D_MODEL)

    def project(xblk, dst):
        for run in project_steps(xblk, dst):
            run()

    if not lookahead:
        project(x_ref[...].reshape(n, D_MODEL), sets[0])
        mix(sets[0], 0)
    else:
        @pl.when((pl.program_id(0) == 0) & (pl.program_id(1) == 0))
        def _first_block():
            project(x_ref[0, 0:blk, :], sets[0])

        def pair_body(k, carry):
            base0 = pl.multiple_of(2 * k * blk, 2 * blk)
            mix(sets[0], base0, project_steps(x_ref[0, _rows(base0 + blk, blk), :], sets[1]))
            in_tile = 2 * k + 2 < nb
            nxt = jnp.minimum(2 * k + 2, nb - 1) * blk
            ahead = jnp.where(in_tile, x_ref[0, _rows(nxt, blk), :], xn_ref[0])
            mix(sets[1], base0 + blk, project_steps(ahead, sets[0]))
            return carry

        lax.fori_loop(0, nb // 2, pair_body, 0)

    @pl.when(pl.program_id(1) == pl.num_programs(1) - 1)
    def _store_state():
        conv_o[...] = conv_c[:, SUBLANES - (CONV_W - 1):SUBLANES, :]
        h_o[...] = h_c[:, 0:1, :]
        for bi in range(bt):
            sret_o[bi] = _compact_state(sret_c[bi])
            shg_o[bi] = _head_transpose(_compact_state(shg_c[bi]))


def _tiling(batch, seq):
    chunk = min(MAX_CHUNK, seq)
    tile = min(seq, MAX_TILE)
    bt = 1 if tile >= BLOCK_ROWS else min(batch, BLOCK_ROWS // tile)
    blk = min(BLOCK_ROWS, bt * tile)
    lookahead = bt == 1 and (tile // blk) >= 2
    assert seq % tile == 0 and batch % bt == 0 and tile % chunk == 0 and chunk % 16 == 0 and blk % chunk == 0
    assert lookahead or bt * tile == blk
    assert not lookahead or (tile // blk) % 2 == 0
    return bt, tile, chunk, blk, lookahead


def _layer_call(layer, depth, x, rope, states, params, tabs, n_lv, tiling):
    bt, tile, chunk, blk, lookahead = tiling
    batch, seq, _ = x.shape
    n_tiles = seq // tile
    grid = (batch // bt, n_tiles)
    n_steps = grid[0] * grid[1]
    zero_init = states is None
    kern = functools.partial(_layer_kernel, (layer, depth, bt, tile, chunk, n_lv, blk, lookahead, zero_init))

    def layer_const(arr):
        tail = arr.shape[1:]
        return pl.BlockSpec((None,) + tail, lambda b, i: (layer,) + (0,) * len(tail))

    def const(arr):
        zeros = (0,) * arr.ndim
        return pl.BlockSpec(arr.shape, lambda b, i: zeros)

    def per_b(tail):
        return pl.BlockSpec((bt,) + tail, lambda b, i: (b,) + (0,) * len(tail))

    def per_lb(tail):
        return pl.BlockSpec((None, bt) + tail, lambda b, i: (layer, b) + (0,) * len(tail))

    def next_block(b, i):
        f = jnp.minimum(b * n_tiles + i + 1, n_steps - 1)
        return (f // n_tiles, (f % n_tiles) * (tile // blk), 0)

    state_tails = ((CONV_W - 1, RG_WIDTH), (1, RG_WIDTH), (HW, HEAD_DIM), (HW, HEAD_DIM))
    x_spec = pl.BlockSpec((bt, tile, D_MODEL), lambda b, i: (b, i, 0))
    rope_spec = pl.BlockSpec((tile, LANES), lambda b, i: (i, 0))
    args, specs = [x], [x_spec]
    if lookahead:
        args.append(x)
        specs.append(pl.BlockSpec((1, blk, D_MODEL), next_block))
    args += list(rope)
    specs += [rope_spec] * 2
    if not zero_init:
        args += list(states)
        specs += [per_lb(t) for t in state_tails]
    args += list(params)
    specs += [const(p) if i == 12 else layer_const(p) for i, p in enumerate(params)]
    tab_args = [tabs[k] for k in _TABLE_ORDER]
    args += tab_args
    specs += [const(t) for t in tab_args]

    out_shape = [jax.ShapeDtypeStruct(x.shape, F32)] + [jax.ShapeDtypeStruct((batch,) + t, F32) for t in state_tails]
    out_specs = [x_spec] + [per_b(t) for t in state_tails]
    seg_rows = blk if bt == 1 else tile
    one_set = ([pltpu.VMEM((blk // seg_rows, SUBLANES + seg_rows, RG_WIDTH), F32), pltpu.VMEM((blk, RG_WIDTH), F32)]
               + [pltpu.VMEM((blk, HW), F32)] * N_NARROW + [pltpu.VMEM((blk, D_MODEL), BF16)])
    carries = [pltpu.VMEM((bt, SUBLANES, RG_WIDTH), F32)] * 2 + [pltpu.VMEM((bt, HW, HW), F32)] * 2
    gate_w = [pltpu.VMEM((2, 2, HW, HW), BF16)]
    return pl.pallas_call(
        kern, grid=grid, in_specs=specs, out_specs=out_specs, out_shape=out_shape,
        scratch_shapes=one_set * 2 + carries + gate_w,
        compiler_params=pltpu.CompilerParams(dimension_semantics=("arbitrary", "arbitrary"),
                                             vmem_limit_bytes=VMEM_LIMIT_BYTES),
        name=f"layer{layer}_t{tile}",
    )(*args)


def kernel(x_prompt, x_sample, cache_conv, state_rglru, state_ret, state_hgrn, w_in, conv_w, conv_b, rg_wa, rg_ba,
           rg_wx, rg_bx, rg_lambda, ret_gn_g, hg_bf, hg_lb_logits, hg_norm_g, w_out, ln_g, ln_b):
    depth = w_in.shape[0]
    bp, lp, _ = x_prompt.shape
    bs, ls, _ = x_sample.shape

    streams = []
    for batch, seq, pos0 in ((bp, lp, 0), (bs, ls, PAST_LEN)):
        tiling = _tiling(batch, seq)
        tabs_np, n_lv = _mixer_tables(tiling[2])
        tabs = {k: jnp.asarray(v, BF16 if k in _BF16_TABLES else F32) for k, v in tabs_np.items()}
        rope = tuple(jnp.asarray(t, F32) for t in _rope_tables(pos0 + np.arange(seq)))
        streams.append((tiling, n_lv, tabs, rope))

    row = lambda p: p[:, None, :]
    halves = lambda w: w.reshape(depth, 2, HW, RG_BLOCK)
    params = (w_in.astype(BF16), halves(rg_wa), halves(rg_wx), w_out.astype(BF16), conv_w, row(conv_b), row(rg_ba),
              row(rg_bx), row(rg_lambda), row(ret_gn_g), row(hg_bf), row(hg_norm_g), hg_lb_logits, row(ln_g),
              row(ln_b))
    sample_states = (cache_conv, state_rglru[:, :, None, :], state_ret.reshape(depth, bs, HW, HEAD_DIM),
                     state_hgrn.reshape(depth, bs, HW, HEAD_DIM))

    xp, xs = x_prompt, x_sample
    outs_p, outs_s = [], []
    for l in range(depth):
        tiling, n_lv, tabs, rope = streams[0]
        res = _layer_call(l, depth, xp, rope, None, params, tabs, n_lv, tiling)
        xp = res[0]
        outs_p.append(res[1:])
        tiling, n_lv, tabs, rope = streams[1]
        res = _layer_call(l, depth, xs, rope, sample_states, params, tabs, n_lv, tiling)
        xs = res[0]
        outs_s.append(res[1:])

    def collect(outs, batch):
        conv = jnp.stack([o[0] for o in outs])
        h = jnp.stack([o[1][:, 0, :] for o in outs])
        ret = jnp.stack([o[2] for o in outs]).reshape(depth, batch, HEADS, HEAD_DIM, HEAD_DIM)
        hg = jnp.stack([o[3] for o in outs]).reshape(depth, batch, HEADS, HEAD_DIM, HEAD_DIM)
        return conv, h, ret, hg

    return (xp, xs) + collect(outs_p, bp) + collect(outs_s, bs)
```

```python
import functools
import math

import jax
import jax.numpy as jnp
import numpy as np
from jax import lax
from jax.experimental import pallas as pl
from jax.experimental.pallas import tpu as pltpu

F32 = jnp.float32
BF16 = jnp.bfloat16

D_MODEL = 1024
RG_WIDTH = 512
RG_BLOCKS = 8
RG_BLOCK = RG_WIDTH // RG_BLOCKS
CONV_W = 4
RG_C = 8.0
HEADS = 4
HEAD_DIM = 64
HW = HEADS * HEAD_DIM
ROPE_BASE = 10000.0
LN_EPS = 1e-5
F_EPS = 1e-6
PAST_LEN = 1024
SPLITS = (RG_WIDTH, RG_WIDTH, HW, HW, HW, HW, HW, HW, HW, HW)
D_IN = sum(SPLITS)
SEG = tuple(int(v) for v in np.cumsum((0,) + SPLITS))
N_NARROW = 8

SUBLANES = 8
LANES = 128
VMEM_LIMIT_BYTES = 56 * 1024 * 1024

MAX_CHUNK = 64
BLOCK_ROWS = 256
PROJ_COLS = 256
MAX_TILE = 1024
LOCKSTEP = 1
LEVEL0_ROWS = SUBLANES

QUADS = ((slice(0, 128), slice(0, 128)), (slice(128, 256), slice(128, 256)))


def _mixer_tables(chunk):
    c = chunk
    t = np.arange(c)
    hid = np.repeat(np.arange(HEADS), HEAD_DIM)
    row_h = np.repeat(np.arange(HEADS), c)
    scale = HEAD_DIM ** -0.5

    log_g = np.log1p(-np.exp2(-5.0 - np.arange(HEADS)))
    rel = t[:, None] - t[None, :]
    dmat = np.where(rel >= 0, np.exp(np.maximum(rel, 0)[None] * log_g[:, None, None]), 0.0)
    bd = (hid[:, None] == hid[None, :]).astype(np.float64)
    tabs = {
        "dall": scale * dmat.transpose(1, 0, 2).reshape(c, HEADS * c),
        "gq": scale * np.repeat(np.exp((t[:, None] + 1.0) * log_g[None, :]), HEAD_DIM, 1),
        "gk": np.exp((c - 1.0 - t)[:, None] * log_g[None, :])[:, hid],
        "gs": np.exp(c * log_g)[hid][:, None] * np.ones((1, HW)),
        "bd": bd,
        "hm": (row_h[:, None] == hid[None, :]).astype(np.float64),
        "e64": bd / HEAD_DIM,
    }

    n_lv = int(round(math.log2(c // LEVEL0_ROWS)))
    tri = (t[None, :] <= t[:, None]).astype(np.float64)
    signs = []
    lvl = np.full((c, c), -1.0)
    causal = t[None, :] <= t[:, None]
    same0 = (t[:, None] // LEVEL0_ROWS) == (t[None, :] // LEVEL0_ROWS)
    lvl[causal & same0] = 0.0
    assigned = same0.copy()
    for lv in range(1, n_lv + 1):
        g = LEVEL0_ROWS * 2 ** lv
        upper = (t % g) >= g // 2
        signs.append(np.where(upper, 1.0, -1.0)[:, None] * np.ones((1, HW)))
        same = (t[:, None] // g) == (t[None, :] // g)
        lvl[causal & same & ~assigned] = float(lv)
        assigned |= same
    tabs["tri3"] = np.concatenate([tri, tri, tri], axis=1)
    tabs["lsgn"] = np.concatenate(signs, axis=0)
    tabs["lvl"] = np.tile(lvl, (1, HEADS))
    return tabs, n_lv


_F32_TABLES = ("dall", "gq", "gk", "lvl", "lsgn", "gs", "bd")
_BF16_TABLES = ("hm", "e64", "tri3")


def _pack_tables(tabs):
    packed, layout = [], {}
    for which, names in enumerate((_F32_TABLES, _BF16_TABLES)):
        row0, parts = 0, []
        for name in names:
            t = tabs[name]
            layout[name] = (which, row0, t.shape[0], t.shape[1])
            parts.append(np.pad(t, ((0, 0), (0, HW - t.shape[1]))))
            row0 += t.shape[0]
        packed.append(np.concatenate(parts, axis=0))
    return packed, tuple(sorted(layout.items()))


def _rope_tables(pos):
    half = HEAD_DIM // 2
    inv = ROPE_BASE ** (-np.arange(half, dtype=np.float64) / half)
    ang = pos.astype(np.float64)[:, None] * inv[None, :]
    cos = np.tile(np.cos(ang), (1, 2 * LANES // HEAD_DIM))
    sin = np.tile(np.concatenate([-np.sin(ang), np.sin(ang)], axis=1), (1, LANES // HEAD_DIM))
    return cos, sin


def _silu(x):
    return x * jax.nn.sigmoid(x)


def _dot(a, b):
    return jnp.dot(a, b, preferred_element_type=F32)


def _dot_t(a, b):
    return lax.dot_general(a, b, (((1,), (1,)), ((), ())), preferred_element_type=F32)


def _tdot(a, b):
    return lax.dot_general(a, b, (((0,), (0,)), ((), ())), preferred_element_type=F32)


def _seg_means(xs, e64):
    rows = xs[0].shape[0]
    x = xs[0] if len(xs) == 1 else jnp.concatenate(xs, axis=0)
    m = _dot(x.astype(BF16), e64)
    return [m[i * rows:(i + 1) * rows] for i in range(len(xs))]


def _stack_heads(x, mask):
    xb = x.astype(BF16)
    return jnp.concatenate([xb] * HEADS, axis=0) * mask


def _rotary(x, cos, sin, first_half):
    partner = jnp.where(first_half, pltpu.roll(x, HW - HEAD_DIM // 2, axis=1), pltpu.roll(x, HEAD_DIM // 2, axis=1))
    return x * cos + partner * sin


def _rg_scan(a, b, h_prev):
    c, w = a.shape
    g = c // SUBLANES
    a3 = a.reshape(g, SUBLANES, w)
    b3 = b.reshape(g, SUBLANES, w)
    sub = lax.broadcasted_iota(jnp.int32, a3.shape, 1)
    shift = 1
    while shift < SUBLANES:
        keep = sub >= shift
        a_sh = pltpu.roll(a3, shift, axis=1)
        b_sh = pltpu.roll(b3, shift, axis=1)
        b3 = jnp.where(keep, a3 * b_sh + b3, b3)
        a3 = jnp.where(keep, a3 * a_sh, a3)
        shift *= 2
    outs = []
    hb = h_prev
    for gi in range(g):
        hg = a3[gi] * hb + b3[gi]
        outs.append(hg)
        hb = jnp.broadcast_to(hg[SUBLANES - 1:SUBLANES, :], (SUBLANES, w))
    return jnp.concatenate(outs, axis=0), hb


def _group_row(x, g, r):
    return jnp.concatenate([jnp.broadcast_to(x[g0 + r:g0 + r + 1, :], (g, x.shape[1]))
                            for g0 in range(0, x.shape[0], g)], axis=0)


def _rows(start, size):
    return pl.ds(start if isinstance(start, int) else pl.multiple_of(start, size), size)


def _update_state(ref, bi, old, scale, mask_ref, row_op, col_op):
    for rs, cs in QUADS:
        sc = scale[rs, cs] if scale.shape[0] > 1 else scale[:, cs]
        ref[bi, rs, cs] = sc * old[rs, cs] + mask_ref[rs, cs] * _tdot(row_op[:, rs], col_op[:, cs])


def _head_transpose(s):
    return jnp.concatenate([s[h * HEAD_DIM:(h + 1) * HEAD_DIM, :].T for h in range(HEADS)], axis=0)


def _expand_state(s, bd):
    return jnp.concatenate([s] * HEADS, axis=1) * bd


def _compact_state(s):
    out = s[:, 0:HEAD_DIM]
    for h in range(1, HEADS):
        out = out + s[:, h * HEAD_DIM:(h + 1) * HEAD_DIM]
    return out


def _layer_kernel(cfg, *refs):
    layer, depth, bt, tile, c, n_lv, blk, lookahead, zero_init, layout = cfg
    n = bt * tile
    nb = n // blk
    it = iter(refs)
    x_ref = next(it)
    xn_ref = next(it) if lookahead else None
    cos_ref, sin_ref = next(it), next(it)
    conv_in, h_in, sret_in, shg_in = (None,) * 4 if zero_init else (next(it), next(it), next(it), next(it))
    (win_ref, wa_ref, wx_ref, wout_ref, convw_ref, convb_ref, rgba_ref, rgbx_ref, rglam_ref, gng_ref, hgbf_ref,
     hgng_ref, lbl_ref, lng_ref, lnb_ref) = [next(it) for _ in range(15)]
    packed = (next(it), next(it))
    tab = {name: packed[which].at[row0:row0 + rows, 0:lanes] for name, (which, row0, rows, lanes) in layout}
    dall_ref, gq_ref, gk_ref, gs_ref, bd_ref = tab["dall"], tab["gq"], tab["gk"], tab["gs"], tab["bd"]
    hm_ref, tri3_ref, lsgn_ref, lvl_ref, e64_ref = tab["hm"], tab["tri3"], tab["lsgn"], tab["lvl"], tab["e64"]
    xo_ref, conv_o, h_o, sret_o, shg_o = [next(it) for _ in range(5)]
    scr = list(it)
    sets = (scr[0:11], scr[11:22])
    conv_c, h_c, sret_c, shg_c = scr[22:26]
    wg_s = scr[26]
    seg_rows = blk if bt == 1 else tile
    dn_alpha = (2 * depth) ** 0.25

    @pl.when(pl.program_id(1) == 0)
    def _load_state():
        if zero_init:
            for ref in (conv_c, h_c, sret_c, shg_c):
                ref[...] = jnp.zeros(ref.shape, F32)
        else:
            conv_c[:, 0:SUBLANES - (CONV_W - 1), :] = jnp.zeros((bt, SUBLANES - (CONV_W - 1), RG_WIDTH), F32)
            conv_c[:, SUBLANES - (CONV_W - 1):SUBLANES, :] = conv_in[...]
            h_c[...] = jnp.broadcast_to(h_in[...], (bt, SUBLANES, RG_WIDTH))
            for bi in range(bt):
                sret_c[bi] = _expand_state(sret_in[bi], bd_ref[...])
                shg_c[bi] = _expand_state(_head_transpose(shg_in[bi]), bd_ref[...])

    @pl.when((pl.program_id(0) == 0) & (pl.program_id(1) == 0))
    def _expand_gate_weights():
        for gi, ref in enumerate((wa_ref, wx_ref)):
            for hf in range(2):
                wg_s[gi, hf] = _expand_state(ref[hf], bd_ref[...]).astype(BF16)

    conv_b = convb_ref[...]
    conv_w = [convw_ref[j:j + 1, :] for j in range(CONV_W)]
    rg_ba = rgba_ref[...]
    rg_bx = rgbx_ref[...]
    c_lam = RG_C * jax.nn.log_sigmoid(rglam_ref[...])
    gn_g = gng_ref[...]
    hg_bf = hgbf_ref[...]
    hg_ng = hgng_ref[...]
    logits = [lbl_ref[li:li + 1, :] for li in range(depth)]
    mx = functools.reduce(jnp.maximum, logits)
    ex = [jnp.exp(v - mx) for v in logits]
    den = functools.reduce(lambda s, v: s + v, ex)
    probs = [v / den for v in ex]
    lb = functools.reduce(lambda s, v: s + v, probs[:layer + 1]) - probs[0]
    one_m_lb = 1.0 - lb

    def project_steps(xblk, dst):
        xb_s = dst[10]

        def stage():
            xb_s[...] = xblk.astype(BF16)

        def step(lo):
            def run():
                val = _dot(xb_s[...], win_ref[:, lo:lo + PROJ_COLS])
                for gi in range(2 + N_NARROW):
                    g_lo, g_hi = max(SEG[gi], lo), min(SEG[gi + 1], lo + PROJ_COLS)
                    if g_lo >= g_hi:
                        continue
                    part = val[:, g_lo - lo:g_hi - lo]
                    if gi == 0:
                        for si in range(blk // seg_rows):
                            dst[0][si, SUBLANES:SUBLANES + seg_rows, g_lo:g_hi] = part[si * seg_rows:(si + 1) * seg_rows]
                    else:
                        dst[gi][:, g_lo - SEG[gi]:g_hi - SEG[gi]] = part
            return run

        return [stage] + [step(lo) for lo in range(0, D_IN, PROJ_COLS)]

    def mix(src, base, filler=()):
        filler = list(filler)
        n_ch = blk // c
        lockstep = n_ch if bt > 1 else min(n_ch, LOCKSTEP)
        n_grp = -(-n_ch // lockstep)
        fill_total = 2.0 + 4 * n_ch + 2 * n_grp
        issued = [0.0, 0]

        def fill(weight):
            issued[0] += weight
            due = math.ceil(len(filler) * min(issued[0] / fill_total, 1.0) - 1e-9)
            while issued[1] < due:
                filler[issued[1]]()
                issued[1] += 1

        rgx_s, rgg_s, q_s, k_s, v_s, rgate_s, hq_s, hf_s, hi_s, hgate_s = src[:10]
        rgx_s[:, 0:SUBLANES, :] = conv_c[...]
        us = []
        for si in range(blk // seg_rows):
            win = rgx_s[si]
            u_seg = conv_b + win[SUBLANES:, :] * conv_w[CONV_W - 1]
            for back in range(1, CONV_W):
                u_seg = u_seg + pltpu.roll(win, back, axis=0)[SUBLANES:, :] * conv_w[CONV_W - 1 - back]
            us.append(u_seg)
        conv_c[...] = rgx_s[:, seg_rows:seg_rows + SUBLANES, :]
        u = us[0] if len(us) == 1 else jnp.concatenate(us, axis=0)
        fill(1.0)
        ub = u.astype(BF16)
        half = RG_WIDTH // 2
        r_pre = jnp.concatenate([_dot(ub[:, :half], wg_s[0, 0]), _dot(ub[:, half:], wg_s[0, 1])], axis=1)
        i_pre = jnp.concatenate([_dot(ub[:, :half], wg_s[1, 0]), _dot(ub[:, half:], wg_s[1, 1])], axis=1)
        log_a = c_lam * jax.nn.sigmoid(r_pre + rg_ba)
        a = jnp.exp(log_a)
        b_in = jnp.sqrt(-jnp.tanh(log_a) * (a * a + 1.0)) * (jax.nn.sigmoid(i_pre + rg_bx) * u)
        fill(1.0)
        e64 = e64_ref[...]
        hm = hm_ref[...]
        lvl = lvl_ref[...]
        first_half = (lax.broadcasted_iota(jnp.int32, (c, HW), 1) % HEAD_DIM) < HEAD_DIM // 2

        per_seq = seg_rows // c
        st_all = [dict(rows=slice(ci * c, (ci + 1) * c), bi=ci // per_seq,
                       trow=_rows(base + (ci % per_seq) * c, c)) for ci in range(n_ch)]

        for g0 in range(0, n_ch, lockstep):
            st = st_all[g0:g0 + lockstep]
            for d in st:
                rows, trow = d["rows"], d["trow"]
                cos = jnp.concatenate([cos_ref[trow, :]] * (HW // LANES), axis=1)
                sin = jnp.concatenate([sin_ref[trow, :]] * (HW // LANES), axis=1)
                d["kr"] = _rotary(k_s[rows, :], cos, sin, first_half)
                d["qb"] = _rotary(q_s[rows, :], cos, sin, first_half).astype(BF16)
                d["scores"] = _dot_t(d["qb"], _stack_heads(d["kr"], hm))
                z = hf_s[rows, :] + hg_bf
                ez = jnp.exp(-jnp.abs(z))
                inv = 1.0 / (1.0 + ez)
                pos = z >= 0.0
                sig_p = jnp.where(pos, inv, ez * inv)
                sig_n = jnp.where(pos, ez * inv, inv)
                log_f = jnp.log(jnp.maximum(lb + one_m_lb * sig_p, F_EPS))
                d["kc"] = one_m_lb * sig_n
                f_hi = log_f.astype(BF16)
                res = log_f - f_hi.astype(F32)
                f_mid = res.astype(BF16)
                f_lo = (res - f_mid.astype(F32)).astype(BF16)
                d["cum"] = _dot(tri3_ref[...], jnp.concatenate([f_hi, f_mid, f_lo], axis=0))
                fill(1.0)

            for d in st:
                bi, kr, qb = d["bi"], d["kr"], d["qb"]
                v = v_s[d["rows"], :]
                p = (d["scores"] * dall_ref[...]).astype(BF16)
                s_ret = sret_c[bi]
                d["o_b"] = _dot(p, _stack_heads(v, hm)) + _dot(qb, s_ret.astype(BF16)) * gq_ref[...]
                _update_state(sret_c, bi, s_ret, gs_ref, bd_ref, (kr * gk_ref[...]).astype(BF16), v.astype(BF16))
                fill(1.0)
                cum, kc = d["cum"], d["kc"]
                qh = hq_s[d["rows"], :]
                e0 = jnp.exp(cum - _group_row(cum, LEVEL0_ROWS, LEVEL0_ROWS // 2 - 1))
                s0 = _dot_t((qh * e0).astype(BF16), _stack_heads(kc / e0, hm))
                pm = jnp.where(lvl == 0.0, s0, 0.0)
                for lv in range(1, n_lv + 1):
                    g = LEVEL0_ROWS * 2 ** lv
                    el = jnp.exp(lsgn_ref[(lv - 1) * c:lv * c, :] * (cum - _group_row(cum, g, g // 2 - 1)))
                    sl_ = _dot_t((qh * el).astype(BF16), _stack_heads(kc * el, hm))
                    pm = jnp.where(lvl == float(lv), sl_, pm)
                d["pm"] = pm
                fill(1.0)

            means = _seg_means([d["o_b"] for d in st], e64)
            for d, m in zip(st, means):
                d["dev"] = d["o_b"] - m
            fill(1.0)
            for d in st:
                bi, cum = d["bi"], d["cum"]
                vh = hi_s[d["rows"], :]
                qh = hq_s[d["rows"], :]
                s_hg = shg_c[bi]
                d["o_c"] = (_dot(d["pm"].astype(BF16), _stack_heads(vh, hm))
                            + _dot_t((qh * jnp.exp(cum)).astype(BF16), s_hg.astype(BF16)))
                k_st = (d["kc"] * jnp.exp(cum[c - 1:c, :] - cum)).astype(BF16)
                dec = jnp.exp(cum[c - 1:c, :])
                _update_state(shg_c, bi, s_hg, dec, bd_ref, vh.astype(BF16), k_st)
                fill(1.0)

            stats = _seg_means([d["dev"] * d["dev"] for d in st] + [d["o_c"] * d["o_c"] for d in st], e64)
            fill(1.0)
            for d, var_b, ms_c in zip(st, stats[:len(st)], stats[len(st):]):
                d["var_b"], d["ms_c"] = var_b, ms_c

        ys = []
        for d in st_all:
            rows, bi = d["rows"], d["bi"]
            h, h_last = _rg_scan(a[rows], b_in[rows], h_c[bi])
            h_c[bi] = h_last
            y_a = h * _silu(rgg_s[rows, :])
            y_b = d["dev"] * lax.rsqrt(d["var_b"] + LN_EPS) * gn_g * _silu(rgate_s[rows, :])
            y_c = d["o_c"] * lax.rsqrt(d["ms_c"] + LN_EPS) * hg_ng * _silu(hgate_s[rows, :])
            ys.append(jnp.concatenate([y_a.astype(BF16), y_b.astype(BF16), y_c.astype(BF16)], axis=1))
        y = ys[0] if n_ch == 1 else jnp.concatenate(ys, axis=0)

        assert abs(issued[0] - fill_total) < 1e-6 and issued[1] == len(filler)

        if bt == 1:
            xrows = _rows(base, blk)
            x_in = x_ref[0, xrows, :]
        else:
            x_in = x_ref[...].reshape(n, D_MODEL)
        xn = dn_alpha * x_in + _dot(y, wout_ref[...])
        mu = jnp.mean(xn, axis=-1, keepdims=True)
        dv = xn - mu
        var = jnp.mean(dv * dv, axis=-1, keepdims=True)
        out = dv * lax.rsqrt(var + LN_EPS) * lng_ref[...] + lnb_ref[...]
        if bt == 1:
            xo_ref[0, xrows, :] = out
        else:
            xo_ref[...] = out.reshape(bt, tile, D_MODEL)

    def project(xblk, dst):
        for run in project_steps(xblk, dst):
            run()

    if not lookahead:
        project(x_ref[...].reshape(n, D_MODEL), sets[0])
        mix(sets[0], 0)
    else:
        @pl.when((pl.program_id(0) == 0) & (pl.program_id(1) == 0))
        def _first_block():
            project(x_ref[0, 0:blk, :], sets[0])

        def pair_body(k, carry):
            base0 = pl.multiple_of(2 * k * blk, 2 * blk)
            mix(sets[0], base0, project_steps(x_ref[0, _rows(base0 + blk, blk), :], sets[1]))
            in_tile = 2 * k + 2 < nb
            nxt = jnp.minimum(2 * k + 2, nb - 1) * blk
            ahead = jnp.where(in_tile, x_ref[0, _rows(nxt, blk), :], xn_ref[0])
            mix(sets[1], base0 + blk, project_steps(ahead, sets[0]))
            return carry

        lax.fori_loop(0, nb // 2, pair_body, 0)

    @pl.when(pl.program_id(1) == pl.num_programs(1) - 1)
    def _store_state():
        conv_o[...] = conv_c[:, SUBLANES - (CONV_W - 1):SUBLANES, :]
        h_o[...] = h_c[:, 0:1, :]
        for bi in range(bt):
            sret_o[bi] = _compact_state(sret_c[bi])
            shg_o[bi] = _head_transpose(_compact_state(shg_c[bi]))


def _tiling(batch, seq):
    chunk = min(MAX_CHUNK, seq)
    tile = min(seq, MAX_TILE)
    bt = 1 if tile >= BLOCK_ROWS else min(batch, BLOCK_ROWS // tile)
    blk = min(BLOCK_ROWS, bt * tile)
    lookahead = bt == 1 and (tile // blk) >= 2
    assert seq % tile == 0 and batch % bt == 0 and tile % chunk == 0 and chunk % 16 == 0 and blk % chunk == 0
    assert lookahead or bt * tile == blk
    assert not lookahead or (tile // blk) % 2 == 0
    return bt, tile, chunk, blk, lookahead


def _layer_call(layer, depth, x, rope, states, params, tabs, layout, n_lv, tiling):
    bt, tile, chunk, blk, lookahead = tiling
    batch, seq, _ = x.shape
    n_tiles = seq // tile
    grid = (batch // bt, n_tiles)
    n_steps = grid[0] * grid[1]
    zero_init = states is None
    kern = functools.partial(_layer_kernel, (layer, depth, bt, tile, chunk, n_lv, blk, lookahead, zero_init, layout))

    def layer_const(arr):
        tail = arr.shape[1:]
        return pl.BlockSpec((None,) + tail, lambda b, i: (layer,) + (0,) * len(tail))

    def const(arr):
        zeros = (0,) * arr.ndim
        return pl.BlockSpec(arr.shape, lambda b, i: zeros)

    def per_b(tail):
        return pl.BlockSpec((bt,) + tail, lambda b, i: (b,) + (0,) * len(tail))

    def per_lb(tail):
        return pl.BlockSpec((None, bt) + tail, lambda b, i: (layer, b) + (0,) * len(tail))

    def next_block(b, i):
        f = jnp.minimum(b * n_tiles + i + 1, n_steps - 1)
        return (f // n_tiles, (f % n_tiles) * (tile // blk), 0)

    state_tails = ((CONV_W - 1, RG_WIDTH), (1, RG_WIDTH), (HW, HEAD_DIM), (HW, HEAD_DIM))
    x_spec = pl.BlockSpec((bt, tile, D_MODEL), lambda b, i: (b, i, 0))
    rope_spec = pl.BlockSpec((tile, LANES), lambda b, i: (i, 0))
    args, specs = [x], [x_spec]
    if lookahead:
        args.append(x)
        specs.append(pl.BlockSpec((1, blk, D_MODEL), next_block))
    args += list(rope)
    specs += [rope_spec] * 2
    if not zero_init:
        args += list(states)
        specs += [per_lb(t) for t in state_tails]
    args += list(params)
    specs += [const(p) if i == 12 else layer_const(p) for i, p in enumerate(params)]
    args += list(tabs)
    specs += [const(t) for t in tabs]

    out_shape = [jax.ShapeDtypeStruct(x.shape, F32)] + [jax.ShapeDtypeStruct((batch,) + t, F32) for t in state_tails]
    out_specs = [x_spec] + [per_b(t) for t in state_tails]
    seg_rows = blk if bt == 1 else tile
    one_set = ([pltpu.VMEM((blk // seg_rows, SUBLANES + seg_rows, RG_WIDTH), F32), pltpu.VMEM((blk, RG_WIDTH), F32)]
               + [pltpu.VMEM((blk, HW), F32)] * N_NARROW + [pltpu.VMEM((blk, D_MODEL), BF16)])
    carries = [pltpu.VMEM((bt, SUBLANES, RG_WIDTH), F32)] * 2 + [pltpu.VMEM((bt, HW, HW), F32)] * 2
    gate_w = [pltpu.VMEM((2, 2, HW, HW), BF16)]
    return pl.pallas_call(
        kern, grid=grid, in_specs=specs, out_specs=out_specs, out_shape=out_shape,
        scratch_shapes=one_set * 2 + carries + gate_w,
        compiler_params=pltpu.CompilerParams(dimension_semantics=("arbitrary", "arbitrary"),
                                             vmem_limit_bytes=VMEM_LIMIT_BYTES),
        name=f"layer{layer}_t{tile}",
    )(*args)


def kernel(x_prompt, x_sample, cache_conv, state_rglru, state_ret, state_hgrn, w_in, conv_w, conv_b, rg_wa, rg_ba,
           rg_wx, rg_bx, rg_lambda, ret_gn_g, hg_bf, hg_lb_logits, hg_norm_g, w_out, ln_g, ln_b):
    depth = w_in.shape[0]
    bp, lp, _ = x_prompt.shape
    bs, ls, _ = x_sample.shape

    streams = []
    for batch, seq, pos0 in ((bp, lp, 0), (bs, ls, PAST_LEN)):
        tiling = _tiling(batch, seq)
        tabs_np, n_lv = _mixer_tables(tiling[2])
        (tab_f32, tab_bf16), layout = _pack_tables(tabs_np)
        tabs = (jnp.asarray(tab_f32, F32), jnp.asarray(tab_bf16, BF16))
        rope = tuple(jnp.asarray(t, F32) for t in _rope_tables(pos0 + np.arange(seq)))
        streams.append((tiling, n_lv, tabs, layout, rope))

    row = lambda p: p[:, None, :]
    halves = lambda w: w.reshape(depth, 2, HW, RG_BLOCK)
    params = (w_in.astype(BF16), halves(rg_wa), halves(rg_wx), w_out.astype(BF16), conv_w, row(conv_b), row(rg_ba),
              row(rg_bx), row(rg_lambda), row(ret_gn_g), row(hg_bf), row(hg_norm_g), hg_lb_logits, row(ln_g),
              row(ln_b))
    sample_states = (cache_conv, state_rglru[:, :, None, :], state_ret.reshape(depth, bs, HW, HEAD_DIM),
                     state_hgrn.reshape(depth, bs, HW, HEAD_DIM))

    xp, xs = x_prompt, x_sample
    outs_p, outs_s = [], []
    for l in range(depth):
        tiling, n_lv, tabs, layout, rope = streams[0]
        res = _layer_call(l, depth, xp, rope, None, params, tabs, layout, n_lv, tiling)
        xp = res[0]
        outs_p.append(res[1:])
        tiling, n_lv, tabs, layout, rope = streams[1]
        res = _layer_call(l, depth, xs, rope, sample_states, params, tabs, layout, n_lv, tiling)
        xs = res[0]
        outs_s.append(res[1:])

    def collect(outs, batch):
        conv = jnp.stack([o[0] for o in outs])
        h = jnp.stack([o[1][:, 0, :] for o in outs])
        ret = jnp.stack([o[2] for o in outs]).reshape(depth, batch, HEADS, HEAD_DIM, HEAD_DIM)
        hg = jnp.stack([o[3] for o in outs]).reshape(depth, batch, HEADS, HEAD_DIM, HEAD_DIM)
        return conv, h, ret, hg

    return (xp, xs) + collect(outs_p, bp) + collect(outs_s, bs)
```

```python
import functools
import math

import jax
import jax.numpy as jnp
import numpy as np
from jax import lax
from jax.experimental import pallas as pl
from jax.experimental.pallas import tpu as pltpu

F32 = jnp.float32
BF16 = jnp.bfloat16

D_MODEL = 1024
RG_WIDTH = 512
RG_BLOCKS = 8
RG_BLOCK = RG_WIDTH // RG_BLOCKS
CONV_W = 4
RG_C = 8.0
HEADS = 4
HEAD_DIM = 64
HW = HEADS * HEAD_DIM
ROPE_BASE = 10000.0
LN_EPS = 1e-5
F_EPS = 1e-6
PAST_LEN = 1024
SPLITS = (RG_WIDTH, RG_WIDTH, HW, HW, HW, HW, HW, HW, HW, HW)
D_IN = sum(SPLITS)
SEG = tuple(int(v) for v in np.cumsum((0,) + SPLITS))
N_NARROW = 8

SUBLANES = 8
LANES = 128
VMEM_LIMIT_BYTES = 56 * 1024 * 1024

MAX_CHUNK = 64
BLOCK_ROWS = 256
PROJ_COLS = 256
MAX_TILE = 1024
LOCKSTEP = 1
LEVEL0_ROWS = SUBLANES

QUADS = ((slice(0, 128), slice(0, 128)), (slice(128, 256), slice(128, 256)))


def _mixer_tables(chunk):
    c = chunk
    t = np.arange(c)
    hid = np.repeat(np.arange(HEADS), HEAD_DIM)
    row_h = np.repeat(np.arange(HEADS), c)
    scale = HEAD_DIM ** -0.5

    log_g = np.log1p(-np.exp2(-5.0 - np.arange(HEADS)))
    rel = t[:, None] - t[None, :]
    dmat = np.where(rel >= 0, np.exp(np.maximum(rel, 0)[None] * log_g[:, None, None]), 0.0)
    bd = (hid[:, None] == hid[None, :]).astype(np.float64)
    tabs = {
        "dall": scale * dmat.transpose(1, 0, 2).reshape(c, HEADS * c),
        "gq": scale * np.repeat(np.exp((t[:, None] + 1.0) * log_g[None, :]), HEAD_DIM, 1),
        "gk": np.exp((c - 1.0 - t)[:, None] * log_g[None, :])[:, hid],
        "gs": np.exp(c * log_g)[hid][:, None] * np.ones((1, HW)),
        "bd": bd,
        "hm": (row_h[:, None] == hid[None, :]).astype(np.float64),
        "e64": bd / HEAD_DIM,
    }

    n_lv = int(round(math.log2(c // LEVEL0_ROWS)))
    tri = (t[None, :] <= t[:, None]).astype(np.float64)
    signs = []
    lvl = np.full((c, c), -1.0)
    causal = t[None, :] <= t[:, None]
    same0 = (t[:, None] // LEVEL0_ROWS) == (t[None, :] // LEVEL0_ROWS)
    lvl[causal & same0] = 0.0
    assigned = same0.copy()
    for lv in range(1, n_lv + 1):
        g = LEVEL0_ROWS * 2 ** lv
        upper = (t % g) >= g // 2
        signs.append(np.where(upper, 1.0, -1.0)[:, None] * np.ones((1, HW)))
        same = (t[:, None] // g) == (t[None, :] // g)
        lvl[causal & same & ~assigned] = float(lv)
        assigned |= same
    tabs["tri3"] = np.concatenate([tri, tri, tri], axis=1)
    tabs["lsgn"] = np.concatenate(signs, axis=0)
    tabs["lvl"] = np.tile(lvl, (1, HEADS))
    return tabs, n_lv


_F32_TABLES = ("dall", "gq", "gk", "lvl", "lsgn", "gs", "bd")
_BF16_TABLES = ("hm", "e64", "tri3")


def _pack_tables(tabs):
    packed, layout = [], {}
    for which, names in enumerate((_F32_TABLES, _BF16_TABLES)):
        row0, parts = 0, []
        for name in names:
            t = tabs[name]
            layout[name] = (which, row0, t.shape[0], t.shape[1])
            parts.append(np.pad(t, ((0, 0), (0, HW - t.shape[1]))))
            row0 += t.shape[0]
        packed.append(np.concatenate(parts, axis=0))
    return packed, tuple(sorted(layout.items()))


def _rope_tables(pos):
    half = HEAD_DIM // 2
    inv = ROPE_BASE ** (-np.arange(half, dtype=np.float64) / half)
    ang = pos.astype(np.float64)[:, None] * inv[None, :]
    cos = np.tile(np.cos(ang), (1, 2 * LANES // HEAD_DIM))
    sin = np.tile(np.concatenate([-np.sin(ang), np.sin(ang)], axis=1), (1, LANES // HEAD_DIM))
    return cos, sin


def _silu(x):
    return x * jax.nn.sigmoid(x)


def _dot(a, b):
    return jnp.dot(a, b, preferred_element_type=F32)


def _dot_t(a, b):
    return lax.dot_general(a, b, (((1,), (1,)), ((), ())), preferred_element_type=F32)


def _tdot(a, b):
    return lax.dot_general(a, b, (((0,), (0,)), ((), ())), preferred_element_type=F32)


def _seg_means(xs, e64):
    rows = xs[0].shape[0]
    x = xs[0] if len(xs) == 1 else jnp.concatenate(xs, axis=0)
    m = _dot(x.astype(BF16), e64)
    return [m[i * rows:(i + 1) * rows] for i in range(len(xs))]


def _stack_heads(x, mask):
    xb = x.astype(BF16)
    return jnp.concatenate([xb] * HEADS, axis=0) * mask


def _rotary(x, cos, sin, first_half):
    partner = jnp.where(first_half, pltpu.roll(x, HW - HEAD_DIM // 2, axis=1), pltpu.roll(x, HEAD_DIM // 2, axis=1))
    return x * cos + partner * sin


def _rg_scan(a, b, h_prev):
    c, w = a.shape
    g = c // SUBLANES
    a3 = a.reshape(g, SUBLANES, w)
    b3 = b.reshape(g, SUBLANES, w)
    sub = lax.broadcasted_iota(jnp.int32, a3.shape, 1)
    shift = 1
    while shift < SUBLANES:
        keep = sub >= shift
        a_sh = pltpu.roll(a3, shift, axis=1)
        b_sh = pltpu.roll(b3, shift, axis=1)
        b3 = jnp.where(keep, a3 * b_sh + b3, b3)
        a3 = jnp.where(keep, a3 * a_sh, a3)
        shift *= 2
    outs = []
    hb = h_prev
    for gi in range(g):
        hg = a3[gi] * hb + b3[gi]
        outs.append(hg)
        hb = jnp.broadcast_to(hg[SUBLANES - 1:SUBLANES, :], (SUBLANES, w))
    return jnp.concatenate(outs, axis=0), hb


def _group_row(x, g, r):
    return jnp.concatenate([jnp.broadcast_to(x[g0 + r:g0 + r + 1, :], (g, x.shape[1]))
                            for g0 in range(0, x.shape[0], g)], axis=0)


def _rows(start, size):
    return pl.ds(start if isinstance(start, int) else pl.multiple_of(start, size), size)


def _update_state(ref, bi, old, scale, mask_ref, row_op, col_op):
    for rs, cs in QUADS:
        sc = scale[rs, cs] if scale.shape[0] > 1 else scale[:, cs]
        ref[bi, rs, cs] = sc * old[rs, cs] + mask_ref[rs, cs] * _tdot(row_op[:, rs], col_op[:, cs])


def _head_transpose(s):
    return jnp.concatenate([s[h * HEAD_DIM:(h + 1) * HEAD_DIM, :].T for h in range(HEADS)], axis=0)


def _expand_state(s, bd):
    return jnp.concatenate([s] * HEADS, axis=1) * bd


def _compact_state(s):
    out = s[:, 0:HEAD_DIM]
    for h in range(1, HEADS):
        out = out + s[:, h * HEAD_DIM:(h + 1) * HEAD_DIM]
    return out


def _layer_kernel(cfg, *refs):
    layer, depth, bt, tile, c, n_lv, blk, lookahead, zero_init, layout = cfg
    n = bt * tile
    nb = n // blk
    it = iter(refs)
    x_ref = next(it)
    xn_ref = next(it) if lookahead else None
    cos_ref, sin_ref = next(it), next(it)
    conv_in, h_in, sret_in, shg_in = (None,) * 4 if zero_init else (next(it), next(it), next(it), next(it))
    (win_ref, wa_ref, wx_ref, wout_ref, convw_ref, convb_ref, rgba_ref, rgbx_ref, rglam_ref, gng_ref, hgbf_ref,
     hgng_ref, lbl_ref, lng_ref, lnb_ref) = [next(it) for _ in range(15)]
    packed = (next(it), next(it))
    tab = {name: packed[which].at[row0:row0 + rows, 0:lanes] for name, (which, row0, rows, lanes) in layout}
    dall_ref, gq_ref, gk_ref, gs_ref, bd_ref = tab["dall"], tab["gq"], tab["gk"], tab["gs"], tab["bd"]
    hm_ref, tri3_ref, lsgn_ref, lvl_ref, e64_ref = tab["hm"], tab["tri3"], tab["lsgn"], tab["lvl"], tab["e64"]
    xo_ref, conv_o, h_o, sret_o, shg_o = [next(it) for _ in range(5)]
    scr = list(it)
    sets = (scr[0:11], scr[11:22])
    conv_c, h_c, sret_c, shg_c = scr[22:26]
    wg_s = scr[26]
    seg_rows = blk if bt == 1 else tile
    dn_alpha = (2 * depth) ** 0.25

    @pl.when(pl.program_id(1) == 0)
    def _load_state():
        if zero_init:
            for ref in (conv_c, h_c, sret_c, shg_c):
                ref[...] = jnp.zeros(ref.shape, F32)
        else:
            conv_c[:, 0:SUBLANES - (CONV_W - 1), :] = jnp.zeros((bt, SUBLANES - (CONV_W - 1), RG_WIDTH), F32)
            conv_c[:, SUBLANES - (CONV_W - 1):SUBLANES, :] = conv_in[...]
            h_c[...] = jnp.broadcast_to(h_in[...], (bt, SUBLANES, RG_WIDTH))
            for bi in range(bt):
                sret_c[bi] = _expand_state(sret_in[bi], bd_ref[...])
                shg_c[bi] = _expand_state(_head_transpose(shg_in[bi]), bd_ref[...])

    @pl.when((pl.program_id(0) == 0) & (pl.program_id(1) == 0))
    def _expand_gate_weights():
        for gi, ref in enumerate((wa_ref, wx_ref)):
            for hf in range(2):
                wg_s[gi, hf] = _expand_state(ref[hf], bd_ref[...]).astype(BF16)

    row = slice(layer, layer + 1)
    conv_b = convb_ref[row, :]
    conv_w = [convw_ref[j:j + 1, :] for j in range(CONV_W)]
    rg_ba = rgba_ref[row, :]
    rg_bx = rgbx_ref[row, :]
    c_lam = RG_C * jax.nn.log_sigmoid(rglam_ref[row, :])
    gn_g = gng_ref[row, :]
    hg_bf = hgbf_ref[row, :]
    hg_ng = hgng_ref[row, :]
    logits = [lbl_ref[li:li + 1, :] for li in range(depth)]
    mx = functools.reduce(jnp.maximum, logits)
    ex = [jnp.exp(v - mx) for v in logits]
    den = functools.reduce(lambda s, v: s + v, ex)
    probs = [v / den for v in ex]
    lb = functools.reduce(lambda s, v: s + v, probs[:layer + 1]) - probs[0]
    one_m_lb = 1.0 - lb

    def project_steps(xblk, dst):
        xb_s = dst[10]

        def stage():
            xb_s[...] = xblk.astype(BF16)

        def step(lo):
            def run():
                val = _dot(xb_s[...], win_ref[:, lo:lo + PROJ_COLS])
                for gi in range(2 + N_NARROW):
                    g_lo, g_hi = max(SEG[gi], lo), min(SEG[gi + 1], lo + PROJ_COLS)
                    if g_lo >= g_hi:
                        continue
                    part = val[:, g_lo - lo:g_hi - lo]
                    if gi == 0:
                        for si in range(blk // seg_rows):
                            dst[0][si, SUBLANES:SUBLANES + seg_rows, g_lo:g_hi] = part[si * seg_rows:(si + 1) * seg_rows]
                    else:
                        dst[gi][:, g_lo - SEG[gi]:g_hi - SEG[gi]] = part
            return run

        return [stage] + [step(lo) for lo in range(0, D_IN, PROJ_COLS)]

    def mix(src, base, filler=()):
        filler = list(filler)
        n_ch = blk // c
        lockstep = n_ch if bt > 1 else min(n_ch, LOCKSTEP)
        n_grp = -(-n_ch // lockstep)
        fill_total = 2.0 + 4 * n_ch + 2 * n_grp
        issued = [0.0, 0]

        def fill(weight):
            issued[0] += weight
            due = math.ceil(len(filler) * min(issued[0] / fill_total, 1.0) - 1e-9)
            while issued[1] < due:
                filler[issued[1]]()
                issued[1] += 1

        rgx_s, rgg_s, q_s, k_s, v_s, rgate_s, hq_s, hf_s, hi_s, hgate_s = src[:10]
        rgx_s[:, 0:SUBLANES, :] = conv_c[...]
        us = []
        for si in range(blk // seg_rows):
            win = rgx_s[si]
            u_seg = conv_b + win[SUBLANES:, :] * conv_w[CONV_W - 1]
            for back in range(1, CONV_W):
                u_seg = u_seg + pltpu.roll(win, back, axis=0)[SUBLANES:, :] * conv_w[CONV_W - 1 - back]
            us.append(u_seg)
        conv_c[...] = rgx_s[:, seg_rows:seg_rows + SUBLANES, :]
        u = us[0] if len(us) == 1 else jnp.concatenate(us, axis=0)
        fill(1.0)
        ub = u.astype(BF16)
        half = RG_WIDTH // 2
        r_pre = jnp.concatenate([_dot(ub[:, :half], wg_s[0, 0]), _dot(ub[:, half:], wg_s[0, 1])], axis=1)
        i_pre = jnp.concatenate([_dot(ub[:, :half], wg_s[1, 0]), _dot(ub[:, half:], wg_s[1, 1])], axis=1)
        log_a = c_lam * jax.nn.sigmoid(r_pre + rg_ba)
        a = jnp.exp(log_a)
        b_in = jnp.sqrt(-jnp.tanh(log_a) * (a * a + 1.0)) * (jax.nn.sigmoid(i_pre + rg_bx) * u)
        fill(1.0)
        e64 = e64_ref[...]
        hm = hm_ref[...]
        lvl = lvl_ref[...]
        first_half = (lax.broadcasted_iota(jnp.int32, (c, HW), 1) % HEAD_DIM) < HEAD_DIM // 2

        per_seq = seg_rows // c
        st_all = [dict(rows=slice(ci * c, (ci + 1) * c), bi=ci // per_seq,
                       trow=_rows(base + (ci % per_seq) * c, c)) for ci in range(n_ch)]

        for g0 in range(0, n_ch, lockstep):
            st = st_all[g0:g0 + lockstep]
            for d in st:
                rows, trow = d["rows"], d["trow"]
                cos = jnp.concatenate([cos_ref[trow, :]] * (HW // LANES), axis=1)
                sin = jnp.concatenate([sin_ref[trow, :]] * (HW // LANES), axis=1)
                d["kr"] = _rotary(k_s[rows, :], cos, sin, first_half)
                d["qb"] = _rotary(q_s[rows, :], cos, sin, first_half).astype(BF16)
                d["scores"] = _dot_t(d["qb"], _stack_heads(d["kr"], hm))
                z = hf_s[rows, :] + hg_bf
                ez = jnp.exp(-jnp.abs(z))
                inv = 1.0 / (1.0 + ez)
                pos = z >= 0.0
                sig_p = jnp.where(pos, inv, ez * inv)
                sig_n = jnp.where(pos, ez * inv, inv)
                log_f = jnp.log(jnp.maximum(lb + one_m_lb * sig_p, F_EPS))
                d["kc"] = one_m_lb * sig_n
                f_hi = log_f.astype(BF16)
                res = log_f - f_hi.astype(F32)
                f_mid = res.astype(BF16)
                f_lo = (res - f_mid.astype(F32)).astype(BF16)
                d["cum"] = _dot(tri3_ref[...], jnp.concatenate([f_hi, f_mid, f_lo], axis=0))
                fill(1.0)

            for d in st:
                bi, kr, qb = d["bi"], d["kr"], d["qb"]
                v = v_s[d["rows"], :]
                p = (d["scores"] * dall_ref[...]).astype(BF16)
                s_ret = sret_c[bi]
                d["o_b"] = _dot(p, _stack_heads(v, hm)) + _dot(qb, s_ret.astype(BF16)) * gq_ref[...]
                _update_state(sret_c, bi, s_ret, gs_ref, bd_ref, (kr * gk_ref[...]).astype(BF16), v.astype(BF16))
                fill(1.0)
                cum, kc = d["cum"], d["kc"]
                qh = hq_s[d["rows"], :]
                e0 = jnp.exp(cum - _group_row(cum, LEVEL0_ROWS, LEVEL0_ROWS // 2 - 1))
                s0 = _dot_t((qh * e0).astype(BF16), _stack_heads(kc / e0, hm))
                pm = jnp.where(lvl == 0.0, s0, 0.0)
                for lv in range(1, n_lv + 1):
                    g = LEVEL0_ROWS * 2 ** lv
                    el = jnp.exp(lsgn_ref[(lv - 1) * c:lv * c, :] * (cum - _group_row(cum, g, g // 2 - 1)))
                    sl_ = _dot_t((qh * el).astype(BF16), _stack_heads(kc * el, hm))
                    pm = jnp.where(lvl == float(lv), sl_, pm)
                d["pm"] = pm
                fill(1.0)

            means = _seg_means([d["o_b"] for d in st], e64)
            for d, m in zip(st, means):
                d["dev"] = d["o_b"] - m
            fill(1.0)
            for d in st:
                bi, cum = d["bi"], d["cum"]
                vh = hi_s[d["rows"], :]
                qh = hq_s[d["rows"], :]
                s_hg = shg_c[bi]
                d["o_c"] = (_dot(d["pm"].astype(BF16), _stack_heads(vh, hm))
                            + _dot_t((qh * jnp.exp(cum)).astype(BF16), s_hg.astype(BF16)))
                k_st = (d["kc"] * jnp.exp(cum[c - 1:c, :] - cum)).astype(BF16)
                dec = jnp.exp(cum[c - 1:c, :])
                _update_state(shg_c, bi, s_hg, dec, bd_ref, vh.astype(BF16), k_st)
                fill(1.0)

            stats = _seg_means([d["dev"] * d["dev"] for d in st] + [d["o_c"] * d["o_c"] for d in st], e64)
            fill(1.0)
            for d, var_b, ms_c in zip(st, stats[:len(st)], stats[len(st):]):
                d["var_b"], d["ms_c"] = var_b, ms_c

        ys = []
        for d in st_all:
            rows, bi = d["rows"], d["bi"]
            h, h_last = _rg_scan(a[rows], b_in[rows], h_c[bi])
            h_c[bi] = h_last
            y_a = h * _silu(rgg_s[rows, :])
            y_b = d["dev"] * lax.rsqrt(d["var_b"] + LN_EPS) * gn_g * _silu(rgate_s[rows, :])
            y_c = d["o_c"] * lax.rsqrt(d["ms_c"] + LN_EPS) * hg_ng * _silu(hgate_s[rows, :])
            ys.append(jnp.concatenate([y_a.astype(BF16), y_b.astype(BF16), y_c.astype(BF16)], axis=1))
        y = ys[0] if n_ch == 1 else jnp.concatenate(ys, axis=0)

        assert abs(issued[0] - fill_total) < 1e-6 and issued[1] == len(filler)

        if bt == 1:
            xrows = _rows(base, blk)
            x_in = x_ref[0, xrows, :]
        else:
            x_in = x_ref[...].reshape(n, D_MODEL)
        xn = dn_alpha * x_in + _dot(y, wout_ref[...])
        mu = jnp.mean(xn, axis=-1, keepdims=True)
        dv = xn - mu
        var = jnp.mean(dv * dv, axis=-1, keepdims=True)
        out = dv * lax.rsqrt(var + LN_EPS) * lng_ref[row, :] + lnb_ref[row, :]
        if bt == 1:
            xo_ref[0, xrows, :] = out
        else:
            xo_ref[...] = out.reshape(bt, tile, D_MODEL)

    def project(xblk, dst):
        for run in project_steps(xblk, dst):
            run()

    if not lookahead:
        project(x_ref[...].reshape(n, D_MODEL), sets[0])
        mix(sets[0], 0)
    else:
        @pl.when((pl.program_id(0) == 0) & (pl.program_id(1) == 0))
        def _first_block():
            project(x_ref[0, 0:blk, :], sets[0])

        def pair_body(k, carry):
            base0 = pl.multiple_of(2 * k * blk, 2 * blk)
            mix(sets[0], base0, project_steps(x_ref[0, _rows(base0 + blk, blk), :], sets[1]))
            in_tile = 2 * k + 2 < nb
            nxt = jnp.minimum(2 * k + 2, nb - 1) * blk
            ahead = jnp.where(in_tile, x_ref[0, _rows(nxt, blk), :], xn_ref[0])
            mix(sets[1], base0 + blk, project_steps(ahead, sets[0]))
            return carry

        lax.fori_loop(0, nb // 2, pair_body, 0)

    @pl.when(pl.program_id(1) == pl.num_programs(1) - 1)
    def _store_state():
        conv_o[...] = conv_c[:, SUBLANES - (CONV_W - 1):SUBLANES, :]
        h_o[...] = h_c[:, 0:1, :]
        for bi in range(bt):
            sret_o[bi] = _compact_state(sret_c[bi])
            shg_o[bi] = _head_transpose(_compact_state(shg_c[bi]))


def _tiling(batch, seq):
    chunk = min(MAX_CHUNK, seq)
    tile = min(seq, MAX_TILE)
    bt = 1 if tile >= BLOCK_ROWS else min(batch, BLOCK_ROWS // tile)
    blk = min(BLOCK_ROWS, bt * tile)
    lookahead = bt == 1 and (tile // blk) >= 2
    assert seq % tile == 0 and batch % bt == 0 and tile % chunk == 0 and chunk % 16 == 0 and blk % chunk == 0
    assert lookahead or bt * tile == blk
    assert not lookahead or (tile // blk) % 2 == 0
    return bt, tile, chunk, blk, lookahead


def _layer_call(layer, depth, x, rope, states, params, tabs, layout, n_lv, tiling):
    bt, tile, chunk, blk, lookahead = tiling
    batch, seq, _ = x.shape
    n_tiles = seq // tile
    grid = (batch // bt, n_tiles)
    n_steps = grid[0] * grid[1]
    zero_init = states is None
    kern = functools.partial(_layer_kernel, (layer, depth, bt, tile, chunk, n_lv, blk, lookahead, zero_init, layout))

    def layer_const(arr):
        tail = arr.shape[1:]
        return pl.BlockSpec((None,) + tail, lambda b, i: (layer,) + (0,) * len(tail))

    def const(arr):
        zeros = (0,) * arr.ndim
        return pl.BlockSpec(arr.shape, lambda b, i: zeros)

    def per_b(tail):
        return pl.BlockSpec((bt,) + tail, lambda b, i: (b,) + (0,) * len(tail))

    def per_lb(tail):
        return pl.BlockSpec((None, bt) + tail, lambda b, i: (layer, b) + (0,) * len(tail))

    def next_block(b, i):
        f = jnp.minimum(b * n_tiles + i + 1, n_steps - 1)
        return (f // n_tiles, (f % n_tiles) * (tile // blk), 0)

    state_tails = ((CONV_W - 1, RG_WIDTH), (1, RG_WIDTH), (HW, HEAD_DIM), (HW, HEAD_DIM))
    x_spec = pl.BlockSpec((bt, tile, D_MODEL), lambda b, i: (b, i, 0))
    rope_spec = pl.BlockSpec((tile, LANES), lambda b, i: (i, 0))
    args, specs = [x], [x_spec]
    if lookahead:
        args.append(x)
        specs.append(pl.BlockSpec((1, blk, D_MODEL), next_block))
    args += list(rope)
    specs += [rope_spec] * 2
    if not zero_init:
        args += list(states)
        specs += [per_lb(t) for t in state_tails]
    args += list(params)
    specs += [layer_const(p) if p.ndim > 2 else const(p) for p in params]
    args += list(tabs)
    specs += [const(t) for t in tabs]

    out_shape = [jax.ShapeDtypeStruct(x.shape, F32)] + [jax.ShapeDtypeStruct((batch,) + t, F32) for t in state_tails]
    out_specs = [x_spec] + [per_b(t) for t in state_tails]
    seg_rows = blk if bt == 1 else tile
    one_set = ([pltpu.VMEM((blk // seg_rows, SUBLANES + seg_rows, RG_WIDTH), F32), pltpu.VMEM((blk, RG_WIDTH), F32)]
               + [pltpu.VMEM((blk, HW), F32)] * N_NARROW + [pltpu.VMEM((blk, D_MODEL), BF16)])
    carries = [pltpu.VMEM((bt, SUBLANES, RG_WIDTH), F32)] * 2 + [pltpu.VMEM((bt, HW, HW), F32)] * 2
    gate_w = [pltpu.VMEM((2, 2, HW, HW), BF16)]
    return pl.pallas_call(
        kern, grid=grid, in_specs=specs, out_specs=out_specs, out_shape=out_shape,
        scratch_shapes=one_set * 2 + carries + gate_w,
        compiler_params=pltpu.CompilerParams(dimension_semantics=("arbitrary", "arbitrary"),
                                             vmem_limit_bytes=VMEM_LIMIT_BYTES),
        name=f"layer{layer}_t{tile}",
    )(*args)


def kernel(x_prompt, x_sample, cache_conv, state_rglru, state_ret, state_hgrn, w_in, conv_w, conv_b, rg_wa, rg_ba,
           rg_wx, rg_bx, rg_lambda, ret_gn_g, hg_bf, hg_lb_logits, hg_norm_g, w_out, ln_g, ln_b):
    depth = w_in.shape[0]
    bp, lp, _ = x_prompt.shape
    bs, ls, _ = x_sample.shape

    streams = []
    for batch, seq, pos0 in ((bp, lp, 0), (bs, ls, PAST_LEN)):
        tiling = _tiling(batch, seq)
        tabs_np, n_lv = _mixer_tables(tiling[2])
        (tab_f32, tab_bf16), layout = _pack_tables(tabs_np)
        tabs = (jnp.asarray(tab_f32, F32), jnp.asarray(tab_bf16, BF16))
        rope = tuple(jnp.asarray(t, F32) for t in _rope_tables(pos0 + np.arange(seq)))
        streams.append((tiling, n_lv, tabs, layout, rope))

    halves = lambda w: w.reshape(depth, 2, HW, RG_BLOCK)
    params = (w_in.astype(BF16), halves(rg_wa), halves(rg_wx), w_out.astype(BF16), conv_w, conv_b, rg_ba, rg_bx,
              rg_lambda, ret_gn_g, hg_bf, hg_norm_g, hg_lb_logits, ln_g, ln_b)
    sample_states = (cache_conv, state_rglru[:, :, None, :], state_ret.reshape(depth, bs, HW, HEAD_DIM),
                     state_hgrn.reshape(depth, bs, HW, HEAD_DIM))

    xp, xs = x_prompt, x_sample
    outs_p, outs_s = [], []
    for l in range(depth):
        tiling, n_lv, tabs, layout, rope = streams[0]
        res = _layer_call(l, depth, xp, rope, None, params, tabs, layout, n_lv, tiling)
        xp = res[0]
        outs_p.append(res[1:])
        tiling, n_lv, tabs, layout, rope = streams[1]
        res = _layer_call(l, depth, xs, rope, sample_states, params, tabs, layout, n_lv, tiling)
        xs = res[0]
        outs_s.append(res[1:])

    def collect(outs, batch):
        conv = jnp.stack([o[0] for o in outs])
        h = jnp.stack([o[1][:, 0, :] for o in outs])
        ret = jnp.stack([o[2] for o in outs]).reshape(depth, batch, HEADS, HEAD_DIM, HEAD_DIM)
        hg = jnp.stack([o[3] for o in outs]).reshape(depth, batch, HEADS, HEAD_DIM, HEAD_DIM)
        return conv, h, ret, hg

    return (xp, xs) + collect(outs_p, bp) + collect(outs_s, bs)
```

```python
import functools
import math

import jax
import jax.numpy as jnp
import numpy as np
from jax import lax
from jax.experimental import pallas as pl
from jax.experimental.pallas import tpu as pltpu

F32 = jnp.float32
BF16 = jnp.bfloat16

D_MODEL = 1024
RG_WIDTH = 512
RG_BLOCKS = 8
RG_BLOCK = RG_WIDTH // RG_BLOCKS
CONV_W = 4
RG_C = 8.0
HEADS = 4
HEAD_DIM = 64
HW = HEADS * HEAD_DIM
ROPE_BASE = 10000.0
LN_EPS = 1e-5
F_EPS = 1e-6
PAST_LEN = 1024
SPLITS = (RG_WIDTH, RG_WIDTH, HW, HW, HW, HW, HW, HW, HW, HW)
D_IN = sum(SPLITS)
SEG = tuple(int(v) for v in np.cumsum((0,) + SPLITS))
N_NARROW = 8

SUBLANES = 8
LANES = 128
VMEM_LIMIT_BYTES = 56 * 1024 * 1024

MAX_CHUNK = 64
BLOCK_ROWS = 256
PROJ_COLS = 256
MAX_TILE = 1024
LOCKSTEP = 1
LEVEL0_ROWS = SUBLANES

QUADS = ((slice(0, 128), slice(0, 128)), (slice(128, 256), slice(128, 256)))


def _mixer_tables(chunk):
    c = chunk
    t = np.arange(c)
    hid = np.repeat(np.arange(HEADS), HEAD_DIM)
    row_h = np.repeat(np.arange(HEADS), c)
    scale = HEAD_DIM ** -0.5

    log_g = np.log1p(-np.exp2(-5.0 - np.arange(HEADS)))
    rel = t[:, None] - t[None, :]
    dmat = np.where(rel >= 0, np.exp(np.maximum(rel, 0)[None] * log_g[:, None, None]), 0.0)
    bd = (hid[:, None] == hid[None, :]).astype(np.float64)
    tabs = {
        "dall": scale * dmat.transpose(1, 0, 2).reshape(c, HEADS * c),
        "gq": scale * np.repeat(np.exp((t[:, None] + 1.0) * log_g[None, :]), HEAD_DIM, 1),
        "gk": np.exp((c - 1.0 - t)[:, None] * log_g[None, :])[:, hid],
        "gs": np.exp(c * log_g)[hid][:, None] * np.ones((1, HW)),
        "bd": bd,
        "hm": (row_h[:, None] == hid[None, :]).astype(np.float64),
        "e64": bd / HEAD_DIM,
    }

    n_lv = int(round(math.log2(c // LEVEL0_ROWS)))
    tri = (t[None, :] <= t[:, None]).astype(np.float64)
    signs = []
    lvl = np.full((c, c), -1.0)
    causal = t[None, :] <= t[:, None]
    same0 = (t[:, None] // LEVEL0_ROWS) == (t[None, :] // LEVEL0_ROWS)
    lvl[causal & same0] = 0.0
    assigned = same0.copy()
    for lv in range(1, n_lv + 1):
        g = LEVEL0_ROWS * 2 ** lv
        upper = (t % g) >= g // 2
        signs.append(np.where(upper, 1.0, -1.0)[:, None] * np.ones((1, HW)))
        same = (t[:, None] // g) == (t[None, :] // g)
        lvl[causal & same & ~assigned] = float(lv)
        assigned |= same
    tabs["tri3"] = np.concatenate([tri, tri, tri], axis=1)
    tabs["lsgn"] = np.concatenate(signs, axis=0)
    tabs["lvl"] = np.tile(lvl, (1, HEADS))
    return tabs, n_lv


_F32_TABLES = ("dall", "gq", "gk", "lvl", "lsgn", "gs", "bd")
_BF16_TABLES = ("hm", "e64", "tri3")


def _pack_tables(tabs):
    packed, layout = [], {}
    for which, names in enumerate((_F32_TABLES, _BF16_TABLES)):
        row0, parts = 0, []
        for name in names:
            t = tabs[name]
            layout[name] = (which, row0, t.shape[0], t.shape[1])
            parts.append(np.pad(t, ((0, 0), (0, HW - t.shape[1]))))
            row0 += t.shape[0]
        packed.append(np.concatenate(parts, axis=0))
    return packed, tuple(sorted(layout.items()))


def _rope_tables(pos):
    half = HEAD_DIM // 2
    inv = ROPE_BASE ** (-np.arange(half, dtype=np.float64) / half)
    ang = pos.astype(np.float64)[:, None] * inv[None, :]
    cos = np.tile(np.cos(ang), (1, 2 * LANES // HEAD_DIM))
    sin = np.tile(np.concatenate([-np.sin(ang), np.sin(ang)], axis=1), (1, LANES // HEAD_DIM))
    return cos, sin


def _silu(x):
    return x * jax.nn.sigmoid(x)


def _dot(a, b):
    return jnp.dot(a, b, preferred_element_type=F32)


def _dot_t(a, b):
    return lax.dot_general(a, b, (((1,), (1,)), ((), ())), preferred_element_type=F32)


def _tdot(a, b):
    return lax.dot_general(a, b, (((0,), (0,)), ((), ())), preferred_element_type=F32)


def _seg_means(xs, e64):
    rows = xs[0].shape[0]
    x = xs[0] if len(xs) == 1 else jnp.concatenate(xs, axis=0)
    m = _dot(x.astype(BF16), e64)
    return [m[i * rows:(i + 1) * rows] for i in range(len(xs))]


def _stack_heads(x, mask):
    xb = x.astype(BF16)
    return jnp.concatenate([xb] * HEADS, axis=0) * mask


def _rotary(x, cos, sin, first_half):
    partner = jnp.where(first_half, pltpu.roll(x, HW - HEAD_DIM // 2, axis=1), pltpu.roll(x, HEAD_DIM // 2, axis=1))
    return x * cos + partner * sin


def _rg_scan(a, b, h_prev):
    c, w = a.shape
    g = c // SUBLANES
    a3 = a.reshape(g, SUBLANES, w)
    b3 = b.reshape(g, SUBLANES, w)
    sub = lax.broadcasted_iota(jnp.int32, a3.shape, 1)
    shift = 1
    while shift < SUBLANES:
        keep = sub >= shift
        a_sh = pltpu.roll(a3, shift, axis=1)
        b_sh = pltpu.roll(b3, shift, axis=1)
        b3 = jnp.where(keep, a3 * b_sh + b3, b3)
        a3 = jnp.where(keep, a3 * a_sh, a3)
        shift *= 2
    outs = []
    hb = h_prev
    for gi in range(g):
        hg = a3[gi] * hb + b3[gi]
        outs.append(hg)
        hb = jnp.broadcast_to(hg[SUBLANES - 1:SUBLANES, :], (SUBLANES, w))
    return jnp.concatenate(outs, axis=0), hb


def _group_row(x, g, r):
    return jnp.concatenate([jnp.broadcast_to(x[g0 + r:g0 + r + 1, :], (g, x.shape[1]))
                            for g0 in range(0, x.shape[0], g)], axis=0)


def _rows(start, size):
    return pl.ds(start if isinstance(start, int) else pl.multiple_of(start, size), size)


def _update_state(ref, bi, old, scale, mask_ref, row_op, col_op):
    for rs, cs in QUADS:
        sc = scale[rs, cs] if scale.shape[0] > 1 else scale[:, cs]
        ref[bi, rs, cs] = sc * old[rs, cs] + mask_ref[rs, cs] * _tdot(row_op[:, rs], col_op[:, cs])


def _head_transpose(s):
    return jnp.concatenate([s[h * HEAD_DIM:(h + 1) * HEAD_DIM, :].T for h in range(HEADS)], axis=0)


def _expand_state(s, bd):
    return jnp.concatenate([s] * HEADS, axis=1) * bd


def _compact_state(s):
    out = s[:, 0:HEAD_DIM]
    for h in range(1, HEADS):
        out = out + s[:, h * HEAD_DIM:(h + 1) * HEAD_DIM]
    return out


def _layer_kernel(cfg, *refs):
    layer, depth, bt, tile, c, n_lv, blk, lookahead, zero_init, layout = cfg
    n = bt * tile
    nb = n // blk
    it = iter(refs)
    x_ref = next(it)
    xn_ref = next(it) if lookahead else None
    cos_ref, sin_ref = next(it), next(it)
    conv_in, h_in, sret_in, shg_in = (None,) * 4 if zero_init else (next(it), next(it), next(it), next(it))
    (win_ref, wa_ref, wx_ref, wout_ref, convw_ref, convb_ref, rgba_ref, rgbx_ref, rglam_ref, gng_ref, hgbf_ref,
     hgng_ref, lbl_ref, lng_ref, lnb_ref) = [next(it) for _ in range(15)]
    packed = (next(it), next(it))
    tab = {name: packed[which].at[row0:row0 + rows, 0:lanes] for name, (which, row0, rows, lanes) in layout}
    dall_ref, gq_ref, gk_ref, gs_ref, bd_ref = tab["dall"], tab["gq"], tab["gk"], tab["gs"], tab["bd"]
    hm_ref, tri3_ref, lsgn_ref, lvl_ref, e64_ref = tab["hm"], tab["tri3"], tab["lsgn"], tab["lvl"], tab["e64"]
    xo_ref, conv_o, h_o, sret_o, shg_o = [next(it) for _ in range(5)]
    scr = list(it)
    sets = (scr[0:11], scr[11:22])
    conv_c, h_c, sret_c, shg_c = scr[22:26]
    wg_s = scr[26]
    seg_rows = blk if bt == 1 else tile
    dn_alpha = (2 * depth) ** 0.25

    @pl.when(pl.program_id(1) == 0)
    def _load_state():
        if zero_init:
            for ref in (conv_c, h_c, sret_c, shg_c):
                ref[...] = jnp.zeros(ref.shape, F32)
        else:
            conv_c[:, 0:SUBLANES - (CONV_W - 1), :] = jnp.zeros((bt, SUBLANES - (CONV_W - 1), RG_WIDTH), F32)
            conv_c[:, SUBLANES - (CONV_W - 1):SUBLANES, :] = conv_in[...]
            h_c[...] = jnp.broadcast_to(h_in[...], (bt, SUBLANES, RG_WIDTH))
            for bi in range(bt):
                sret_c[bi] = _expand_state(sret_in[bi], bd_ref[...])
                shg_c[bi] = _expand_state(_head_transpose(shg_in[bi]), bd_ref[...])

    @pl.when((pl.program_id(0) == 0) & (pl.program_id(1) == 0))
    def _expand_gate_weights():
        for gi, ref in enumerate((wa_ref, wx_ref)):
            for hf in range(2):
                wg_s[gi, hf] = _expand_state(ref[hf], bd_ref[...]).astype(BF16)

    row = slice(layer, layer + 1)
    conv_b = convb_ref[row, :]
    conv_w = [convw_ref[j:j + 1, :] for j in range(CONV_W)]
    rg_ba = rgba_ref[row, :]
    rg_bx = rgbx_ref[row, :]
    c_lam = RG_C * jax.nn.log_sigmoid(rglam_ref[row, :])
    gn_g = gng_ref[row, :]
    hg_bf = hgbf_ref[row, :]
    hg_ng = hgng_ref[row, :]
    logits = [lbl_ref[li:li + 1, :] for li in range(depth)]
    mx = functools.reduce(jnp.maximum, logits)
    ex = [jnp.exp(v - mx) for v in logits]
    den = functools.reduce(lambda s, v: s + v, ex)
    probs = [v / den for v in ex]
    lb = functools.reduce(lambda s, v: s + v, probs[:layer + 1]) - probs[0]
    one_m_lb = 1.0 - lb

    def project_steps(xblk, dst):
        xb_s = dst[10]

        def stage():
            xb_s[...] = xblk.astype(BF16)

        def step(lo):
            def run():
                val = _dot(xb_s[...], win_ref[:, lo:lo + PROJ_COLS])
                for gi in range(2 + N_NARROW):
                    g_lo, g_hi = max(SEG[gi], lo), min(SEG[gi + 1], lo + PROJ_COLS)
                    if g_lo >= g_hi:
                        continue
                    part = val[:, g_lo - lo:g_hi - lo]
                    if gi == 0:
                        for si in range(blk // seg_rows):
                            dst[0][si, SUBLANES:SUBLANES + seg_rows, g_lo:g_hi] = part[si * seg_rows:(si + 1) * seg_rows]
                    else:
                        dst[gi][:, g_lo - SEG[gi]:g_hi - SEG[gi]] = part
            return run

        return [stage] + [step(lo) for lo in range(0, D_IN, PROJ_COLS)]

    def mix(src, base, filler=()):
        filler = list(filler)
        n_ch = blk // c
        lockstep = n_ch if bt > 1 else min(n_ch, LOCKSTEP)
        n_grp = -(-n_ch // lockstep)
        fill_total = 2.0 + 4 * n_ch + 2 * n_grp
        issued = [0.0, 0]

        def fill(weight):
            issued[0] += weight
            due = math.ceil(len(filler) * min(issued[0] / fill_total, 1.0) - 1e-9)
            while issued[1] < due:
                filler[issued[1]]()
                issued[1] += 1

        rgx_s, rgg_s, q_s, k_s, v_s, rgate_s, hq_s, hf_s, hi_s, hgate_s = src[:10]
        rgx_s[:, 0:SUBLANES, :] = conv_c[...]
        us = []
        for si in range(blk // seg_rows):
            win = rgx_s[si]
            u_seg = conv_b + win[SUBLANES:, :] * conv_w[CONV_W - 1]
            for back in range(1, CONV_W):
                u_seg = u_seg + pltpu.roll(win, back, axis=0)[SUBLANES:, :] * conv_w[CONV_W - 1 - back]
            us.append(u_seg)
        conv_c[...] = rgx_s[:, seg_rows:seg_rows + SUBLANES, :]
        u = us[0] if len(us) == 1 else jnp.concatenate(us, axis=0)
        fill(1.0)
        ub = u.astype(BF16)
        half = RG_WIDTH // 2
        r_pre = jnp.concatenate([_dot(ub[:, :half], wg_s[0, 0]), _dot(ub[:, half:], wg_s[0, 1])], axis=1)
        i_pre = jnp.concatenate([_dot(ub[:, :half], wg_s[1, 0]), _dot(ub[:, half:], wg_s[1, 1])], axis=1)
        log_a = c_lam * jax.nn.sigmoid(r_pre + rg_ba)
        a = jnp.exp(log_a)
        b_in = jnp.sqrt(-jnp.tanh(log_a) * (a * a + 1.0)) * (jax.nn.sigmoid(i_pre + rg_bx) * u)
        fill(1.0)
        e64 = e64_ref[...]
        hm = hm_ref[...]
        lvl = lvl_ref[...]
        first_half = (lax.broadcasted_iota(jnp.int32, (c, HW), 1) % HEAD_DIM) < HEAD_DIM // 2

        per_seq = seg_rows // c
        st_all = [dict(rows=slice(ci * c, (ci + 1) * c), bi=ci // per_seq,
                       trow=_rows(base + (ci % per_seq) * c, c)) for ci in range(n_ch)]

        for g0 in range(0, n_ch, lockstep):
            st = st_all[g0:g0 + lockstep]
            for d in st:
                rows, trow = d["rows"], d["trow"]
                cos = jnp.concatenate([cos_ref[trow, :]] * (HW // LANES), axis=1)
                sin = jnp.concatenate([sin_ref[trow, :]] * (HW // LANES), axis=1)
                d["kr"] = _rotary(k_s[rows, :], cos, sin, first_half)
                d["qb"] = _rotary(q_s[rows, :], cos, sin, first_half).astype(BF16)
                d["scores"] = _dot_t(d["qb"], _stack_heads(d["kr"], hm))
                z = hf_s[rows, :] + hg_bf
                ez = jnp.exp(-jnp.abs(z))
                inv = 1.0 / (1.0 + ez)
                pos = z >= 0.0
                sig_p = jnp.where(pos, inv, ez * inv)
                sig_n = jnp.where(pos, ez * inv, inv)
                log_f = jnp.log2(jnp.maximum(lb + one_m_lb * sig_p, F_EPS))
                d["kc"] = one_m_lb * sig_n
                f_hi = log_f.astype(BF16)
                res = log_f - f_hi.astype(F32)
                f_mid = res.astype(BF16)
                f_lo = (res - f_mid.astype(F32)).astype(BF16)
                d["cum"] = _dot(tri3_ref[...], jnp.concatenate([f_hi, f_mid, f_lo], axis=0))
                fill(1.0)

            for d in st:
                bi, kr, qb = d["bi"], d["kr"], d["qb"]
                v = v_s[d["rows"], :]
                p = (d["scores"] * dall_ref[...]).astype(BF16)
                s_ret = sret_c[bi]
                d["o_b"] = _dot(p, _stack_heads(v, hm)) + _dot(qb, s_ret.astype(BF16)) * gq_ref[...]
                _update_state(sret_c, bi, s_ret, gs_ref, bd_ref, (kr * gk_ref[...]).astype(BF16), v.astype(BF16))
                fill(1.0)
                cum, kc = d["cum"], d["kc"]
                qh = hq_s[d["rows"], :]
                e0 = jnp.exp2(cum - _group_row(cum, LEVEL0_ROWS, LEVEL0_ROWS // 2 - 1))
                s0 = _dot_t((qh * e0).astype(BF16), _stack_heads(kc / e0, hm))
                pm = jnp.where(lvl == 0.0, s0, 0.0)
                for lv in range(1, n_lv + 1):
                    g = LEVEL0_ROWS * 2 ** lv
                    el = jnp.exp2(lsgn_ref[(lv - 1) * c:lv * c, :] * (cum - _group_row(cum, g, g // 2 - 1)))
                    sl_ = _dot_t((qh * el).astype(BF16), _stack_heads(kc * el, hm))
                    pm = jnp.where(lvl == float(lv), sl_, pm)
                d["pm"] = pm
                fill(1.0)

            means = _seg_means([d["o_b"] for d in st], e64)
            for d, m in zip(st, means):
                d["dev"] = d["o_b"] - m
            fill(1.0)
            for d in st:
                bi, cum = d["bi"], d["cum"]
                vh = hi_s[d["rows"], :]
                qh = hq_s[d["rows"], :]
                s_hg = shg_c[bi]
                d["o_c"] = (_dot(d["pm"].astype(BF16), _stack_heads(vh, hm))
                            + _dot_t((qh * jnp.exp2(cum)).astype(BF16), s_hg.astype(BF16)))
                k_st = (d["kc"] * jnp.exp2(cum[c - 1:c, :] - cum)).astype(BF16)
                dec = jnp.exp2(cum[c - 1:c, :])
                _update_state(shg_c, bi, s_hg, dec, bd_ref, vh.astype(BF16), k_st)
                fill(1.0)

            stats = _seg_means([d["dev"] * d["dev"] for d in st] + [d["o_c"] * d["o_c"] for d in st], e64)
            fill(1.0)
            for d, var_b, ms_c in zip(st, stats[:len(st)], stats[len(st):]):
                d["var_b"], d["ms_c"] = var_b, ms_c

        ys = []
        for d in st_all:
            rows, bi = d["rows"], d["bi"]
            h, h_last = _rg_scan(a[rows], b_in[rows], h_c[bi])
            h_c[bi] = h_last
            y_a = h * _silu(rgg_s[rows, :])
            y_b = d["dev"] * lax.rsqrt(d["var_b"] + LN_EPS) * gn_g * _silu(rgate_s[rows, :])
            y_c = d["o_c"] * lax.rsqrt(d["ms_c"] + LN_EPS) * hg_ng * _silu(hgate_s[rows, :])
            ys.append(jnp.concatenate([y_a.astype(BF16), y_b.astype(BF16), y_c.astype(BF16)], axis=1))
        y = ys[0] if n_ch == 1 else jnp.concatenate(ys, axis=0)

        assert abs(issued[0] - fill_total) < 1e-6 and issued[1] == len(filler)

        if bt == 1:
            xrows = _rows(base, blk)
            x_in = x_ref[0, xrows, :]
        else:
            x_in = x_ref[...].reshape(n, D_MODEL)
        xn = dn_alpha * x_in + _dot(y, wout_ref[...])
        mu = jnp.mean(xn, axis=-1, keepdims=True)
        dv = xn - mu
        var = jnp.mean(dv * dv, axis=-1, keepdims=True)
        out = dv * lax.rsqrt(var + LN_EPS) * lng_ref[row, :] + lnb_ref[row, :]
        if bt == 1:
            xo_ref[0, xrows, :] = out
        else:
            xo_ref[...] = out.reshape(bt, tile, D_MODEL)

    def project(xblk, dst):
        for run in project_steps(xblk, dst):
            run()

    if not lookahead:
        project(x_ref[...].reshape(n, D_MODEL), sets[0])
        mix(sets[0], 0)
    else:
        @pl.when((pl.program_id(0) == 0) & (pl.program_id(1) == 0))
        def _first_block():
            project(x_ref[0, 0:blk, :], sets[0])

        def pair_body(k, carry):
            base0 = pl.multiple_of(2 * k * blk, 2 * blk)
            mix(sets[0], base0, project_steps(x_ref[0, _rows(base0 + blk, blk), :], sets[1]))
            in_tile = 2 * k + 2 < nb
            nxt = jnp.minimum(2 * k + 2, nb - 1) * blk
            ahead = jnp.where(in_tile, x_ref[0, _rows(nxt, blk), :], xn_ref[0])
            mix(sets[1], base0 + blk, project_steps(ahead, sets[0]))
            return carry

        lax.fori_loop(0, nb // 2, pair_body, 0)

    @pl.when(pl.program_id(1) == pl.num_programs(1) - 1)
    def _store_state():
        conv_o[...] = conv_c[:, SUBLANES - (CONV_W - 1):SUBLANES, :]
        h_o[...] = h_c[:, 0:1, :]
        for bi in range(bt):
            sret_o[bi] = _compact_state(sret_c[bi])
            shg_o[bi] = _head_transpose(_compact_state(shg_c[bi]))


def _tiling(batch, seq):
    chunk = min(MAX_CHUNK, seq)
    tile = min(seq, MAX_TILE)
    bt = 1 if tile >= BLOCK_ROWS else min(batch, BLOCK_ROWS // tile)
    blk = min(BLOCK_ROWS, bt * tile)
    lookahead = bt == 1 and (tile // blk) >= 2
    assert seq % tile == 0 and batch % bt == 0 and tile % chunk == 0 and chunk % 16 == 0 and blk % chunk == 0
    assert lookahead or bt * tile == blk
    assert not lookahead or (tile // blk) % 2 == 0
    return bt, tile, chunk, blk, lookahead


def _layer_call(layer, depth, x, rope, states, params, tabs, layout, n_lv, tiling):
    bt, tile, chunk, blk, lookahead = tiling
    batch, seq, _ = x.shape
    n_tiles = seq // tile
    grid = (batch // bt, n_tiles)
    n_steps = grid[0] * grid[1]
    zero_init = states is None
    kern = functools.partial(_layer_kernel, (layer, depth, bt, tile, chunk, n_lv, blk, lookahead, zero_init, layout))

    def layer_const(arr):
        tail = arr.shape[1:]
        return pl.BlockSpec((None,) + tail, lambda b, i: (layer,) + (0,) * len(tail))

    def const(arr):
        zeros = (0,) * arr.ndim
        return pl.BlockSpec(arr.shape, lambda b, i: zeros)

    def per_b(tail):
        return pl.BlockSpec((bt,) + tail, lambda b, i: (b,) + (0,) * len(tail))

    def per_lb(tail):
        return pl.BlockSpec((None, bt) + tail, lambda b, i: (layer, b) + (0,) * len(tail))

    def next_block(b, i):
        f = jnp.minimum(b * n_tiles + i + 1, n_steps - 1)
        return (f // n_tiles, (f % n_tiles) * (tile // blk), 0)

    state_tails = ((CONV_W - 1, RG_WIDTH), (1, RG_WIDTH), (HW, HEAD_DIM), (HW, HEAD_DIM))
    x_spec = pl.BlockSpec((bt, tile, D_MODEL), lambda b, i: (b, i, 0))
    rope_spec = pl.BlockSpec((tile, LANES), lambda b, i: (i, 0))
    args, specs = [x], [x_spec]
    if lookahead:
        args.append(x)
        specs.append(pl.BlockSpec((1, blk, D_MODEL), next_block))
    args += list(rope)
    specs += [rope_spec] * 2
    if not zero_init:
        args += list(states)
        specs += [per_lb(t) for t in state_tails]
    args += list(params)
    specs += [layer_const(p) if p.ndim > 2 else const(p) for p in params]
    args += list(tabs)
    specs += [const(t) for t in tabs]

    out_shape = [jax.ShapeDtypeStruct(x.shape, F32)] + [jax.ShapeDtypeStruct((batch,) + t, F32) for t in state_tails]
    out_specs = [x_spec] + [per_b(t) for t in state_tails]
    seg_rows = blk if bt == 1 else tile
    one_set = ([pltpu.VMEM((blk // seg_rows, SUBLANES + seg_rows, RG_WIDTH), F32), pltpu.VMEM((blk, RG_WIDTH), F32)]
               + [pltpu.VMEM((blk, HW), F32)] * N_NARROW + [pltpu.VMEM((blk, D_MODEL), BF16)])
    carries = [pltpu.VMEM((bt, SUBLANES, RG_WIDTH), F32)] * 2 + [pltpu.VMEM((bt, HW, HW), F32)] * 2
    gate_w = [pltpu.VMEM((2, 2, HW, HW), BF16)]
    return pl.pallas_call(
        kern, grid=grid, in_specs=specs, out_specs=out_specs, out_shape=out_shape,
        scratch_shapes=one_set * 2 + carries + gate_w,
        compiler_params=pltpu.CompilerParams(dimension_semantics=("arbitrary", "arbitrary"),
                                             vmem_limit_bytes=VMEM_LIMIT_BYTES),
        name=f"layer{layer}_t{tile}",
    )(*args)


def kernel(x_prompt, x_sample, cache_conv, state_rglru, state_ret, state_hgrn, w_in, conv_w, conv_b, rg_wa, rg_ba,
           rg_wx, rg_bx, rg_lambda, ret_gn_g, hg_bf, hg_lb_logits, hg_norm_g, w_out, ln_g, ln_b):
    depth = w_in.shape[0]
    bp, lp, _ = x_prompt.shape
    bs, ls, _ = x_sample.shape

    streams = []
    for batch, seq, pos0 in ((bp, lp, 0), (bs, ls, PAST_LEN)):
        tiling = _tiling(batch, seq)
        tabs_np, n_lv = _mixer_tables(tiling[2])
        (tab_f32, tab_bf16), layout = _pack_tables(tabs_np)
        tabs = (jnp.asarray(tab_f32, F32), jnp.asarray(tab_bf16, BF16))
        rope = tuple(jnp.asarray(t, F32) for t in _rope_tables(pos0 + np.arange(seq)))
        streams.append((tiling, n_lv, tabs, layout, rope))

    halves = lambda w: w.reshape(depth, 2, HW, RG_BLOCK)
    params = (w_in.astype(BF16), halves(rg_wa), halves(rg_wx), w_out.astype(BF16), conv_w, conv_b, rg_ba, rg_bx,
              rg_lambda, ret_gn_g, hg_bf, hg_norm_g, hg_lb_logits, ln_g, ln_b)
    sample_states = (cache_conv, state_rglru[:, :, None, :], state_ret.reshape(depth, bs, HW, HEAD_DIM),
                     state_hgrn.reshape(depth, bs, HW, HEAD_DIM))

    xp, xs = x_prompt, x_sample
    outs_p, outs_s = [], []
    for l in range(depth):
        tiling, n_lv, tabs, layout, rope = streams[0]
        res = _layer_call(l, depth, xp, rope, None, params, tabs, layout, n_lv, tiling)
        xp = res[0]
        outs_p.append(res[1:])
        tiling, n_lv, tabs, layout, rope = streams[1]
        res = _layer_call(l, depth, xs, rope, sample_states, params, tabs, layout, n_lv, tiling)
        xs = res[0]
        outs_s.append(res[1:])

    def collect(outs, batch):
        conv = jnp.stack([o[0] for o in outs])
        h = jnp.stack([o[1][:, 0, :] for o in outs])
        ret = jnp.stack([o[2] for o in outs]).reshape(depth, batch, HEADS, HEAD_DIM, HEAD_DIM)
        hg = jnp.stack([o[3] for o in outs]).reshape(depth, batch, HEADS, HEAD_DIM, HEAD_DIM)
        return conv, h, ret, hg

    return (xp, xs) + collect(outs_p, bp) + collect(outs_s, bs)
```

```python
import functools
import math

import jax
import jax.numpy as jnp
import numpy as np
from jax import lax
from jax.experimental import pallas as pl
from jax.experimental.pallas import tpu as pltpu

F32 = jnp.float32
BF16 = jnp.bfloat16

D_MODEL = 1024
RG_WIDTH = 512
RG_BLOCKS = 8
RG_BLOCK = RG_WIDTH // RG_BLOCKS
CONV_W = 4
RG_C = 8.0
HEADS = 4
HEAD_DIM = 64
HW = HEADS * HEAD_DIM
ROPE_BASE = 10000.0
LN_EPS = 1e-5
F_EPS = 1e-6
PAST_LEN = 1024
SPLITS = (RG_WIDTH, RG_WIDTH, HW, HW, HW, HW, HW, HW, HW, HW)
D_IN = sum(SPLITS)
SEG = tuple(int(v) for v in np.cumsum((0,) + SPLITS))
N_NARROW = 8

SUBLANES = 8
LANES = 128
VMEM_LIMIT_BYTES = 56 * 1024 * 1024

MAX_CHUNK = 64
BLOCK_ROWS = 256
PROJ_COLS = 256
MAX_TILE = 1024
LOCKSTEP = 1
LEVEL0_ROWS = SUBLANES

QUADS = ((slice(0, 128), slice(0, 128)), (slice(128, 256), slice(128, 256)))


def _mixer_tables(chunk):
    c = chunk
    t = np.arange(c)
    hid = np.repeat(np.arange(HEADS), HEAD_DIM)
    row_h = np.repeat(np.arange(HEADS), c)
    scale = HEAD_DIM ** -0.5

    log_g = np.log1p(-np.exp2(-5.0 - np.arange(HEADS)))
    rel = t[:, None] - t[None, :]
    dmat = np.where(rel >= 0, np.exp(np.maximum(rel, 0)[None] * log_g[:, None, None]), 0.0)
    bd = (hid[:, None] == hid[None, :]).astype(np.float64)
    tabs = {
        "dall": scale * dmat.transpose(1, 0, 2).reshape(c, HEADS * c),
        "gq": scale * np.repeat(np.exp((t[:, None] + 1.0) * log_g[None, :]), HEAD_DIM, 1),
        "gk": np.exp((c - 1.0 - t)[:, None] * log_g[None, :])[:, hid],
        "gs": np.exp(c * log_g)[hid][:, None] * np.ones((1, HW)),
        "bd": bd,
        "hm": (row_h[:, None] == hid[None, :]).astype(np.float64),
        "e64": bd / HEAD_DIM,
    }

    n_lv = int(round(math.log2(c // LEVEL0_ROWS)))
    tri = (t[None, :] <= t[:, None]).astype(np.float64)
    signs = []
    lvl = np.full((c, c), -1.0)
    causal = t[None, :] <= t[:, None]
    same0 = (t[:, None] // LEVEL0_ROWS) == (t[None, :] // LEVEL0_ROWS)
    lvl[causal & same0] = 0.0
    assigned = same0.copy()
    for lv in range(1, n_lv + 1):
        g = LEVEL0_ROWS * 2 ** lv
        upper = (t % g) >= g // 2
        signs.append(np.where(upper, 1.0, -1.0)[:, None] * np.ones((1, HW)))
        same = (t[:, None] // g) == (t[None, :] // g)
        lvl[causal & same & ~assigned] = float(lv)
        assigned |= same
    tabs["tri3"] = np.concatenate([tri, tri, tri], axis=1)
    tabs["lsgn"] = np.concatenate(signs, axis=0)
    tabs["lvl"] = np.tile(lvl, (1, HEADS))
    return tabs, n_lv


_F32_TABLES = ("dall", "gq", "gk", "lvl", "lsgn", "gs", "bd")
_BF16_TABLES = ("hm", "e64", "tri3")


def _pack_tables(tabs):
    packed, layout = [], {}
    for which, names in enumerate((_F32_TABLES, _BF16_TABLES)):
        row0, parts = 0, []
        for name in names:
            t = tabs[name]
            layout[name] = (which, row0, t.shape[0], t.shape[1])
            parts.append(np.pad(t, ((0, 0), (0, HW - t.shape[1]))))
            row0 += t.shape[0]
        packed.append(np.concatenate(parts, axis=0))
    return packed, tuple(sorted(layout.items()))


def _rope_tables(pos):
    half = HEAD_DIM // 2
    inv = ROPE_BASE ** (-np.arange(half, dtype=np.float64) / half)
    ang = pos.astype(np.float64)[:, None] * inv[None, :]
    cos = np.tile(np.cos(ang), (1, 2 * LANES // HEAD_DIM))
    sin = np.tile(np.concatenate([-np.sin(ang), np.sin(ang)], axis=1), (1, LANES // HEAD_DIM))
    return cos, sin


def _silu(x):
    return x * jax.nn.sigmoid(x)


def _dot(a, b):
    return jnp.dot(a, b, preferred_element_type=F32)


def _dot_t(a, b):
    return lax.dot_general(a, b, (((1,), (1,)), ((), ())), preferred_element_type=F32)


def _tdot(a, b):
    return lax.dot_general(a, b, (((0,), (0,)), ((), ())), preferred_element_type=F32)


def _seg_means(xs, e64):
    rows = xs[0].shape[0]
    x = xs[0] if len(xs) == 1 else jnp.concatenate(xs, axis=0)
    m = _dot(x.astype(BF16), e64)
    return [m[i * rows:(i + 1) * rows] for i in range(len(xs))]


def _stack_heads(x, mask):
    xb = x.astype(BF16)
    return jnp.concatenate([xb] * HEADS, axis=0) * mask


def _rotary(x, cos, sin, first_half):
    partner = jnp.where(first_half, pltpu.roll(x, HW - HEAD_DIM // 2, axis=1), pltpu.roll(x, HEAD_DIM // 2, axis=1))
    return x * cos + partner * sin


def _rg_scan(a, b, h_prev):
    c, w = a.shape
    g = c // SUBLANES
    a3 = a.reshape(g, SUBLANES, w)
    b3 = b.reshape(g, SUBLANES, w)
    sub = lax.broadcasted_iota(jnp.int32, a3.shape, 1)
    shift = 1
    while shift < SUBLANES:
        keep = sub >= shift
        a_sh = pltpu.roll(a3, shift, axis=1)
        b_sh = pltpu.roll(b3, shift, axis=1)
        b3 = jnp.where(keep, a3 * b_sh + b3, b3)
        a3 = jnp.where(keep, a3 * a_sh, a3)
        shift *= 2
    outs = []
    hb = h_prev
    for gi in range(g):
        hg = a3[gi] * hb + b3[gi]
        outs.append(hg)
        hb = jnp.broadcast_to(hg[SUBLANES - 1:SUBLANES, :], (SUBLANES, w))
    return jnp.concatenate(outs, axis=0), hb


def _group_row(x, g, r):
    return jnp.concatenate([jnp.broadcast_to(x[g0 + r:g0 + r + 1, :], (g, x.shape[1]))
                            for g0 in range(0, x.shape[0], g)], axis=0)


def _rows(start, size):
    return pl.ds(start if isinstance(start, int) else pl.multiple_of(start, size), size)


def _update_state(ref, bi, old, scale, mask_ref, row_op, col_op):
    for rs, cs in QUADS:
        sc = scale[rs, cs] if scale.shape[0] > 1 else scale[:, cs]
        ref[bi, rs, cs] = sc * old[rs, cs] + mask_ref[rs, cs] * _tdot(row_op[:, rs], col_op[:, cs])


def _head_transpose(s):
    return jnp.concatenate([s[h * HEAD_DIM:(h + 1) * HEAD_DIM, :].T for h in range(HEADS)], axis=0)


def _expand_state(s, bd):
    return jnp.concatenate([s] * HEADS, axis=1) * bd


def _compact_state(s):
    out = s[:, 0:HEAD_DIM]
    for h in range(1, HEADS):
        out = out + s[:, h * HEAD_DIM:(h + 1) * HEAD_DIM]
    return out


def _layer_kernel(cfg, *refs):
    layer, depth, bt, tile, c, n_lv, blk, lookahead, zero_init, layout = cfg
    n = bt * tile
    nb = n // blk
    it = iter(refs)
    x_ref = next(it)
    xn_ref = next(it) if lookahead else None
    cos_ref, sin_ref = next(it), next(it)
    conv_in, h_in, sret_in, shg_in = (None,) * 4 if zero_init else (next(it), next(it), next(it), next(it))
    (win_ref, wa_ref, wx_ref, wout_ref, convw_ref, convb_ref, rgba_ref, rgbx_ref, rglam_ref, gng_ref, hgbf_ref,
     hgng_ref, lbl_ref, lng_ref, lnb_ref) = [next(it) for _ in range(15)]
    packed = (next(it), next(it))
    tab = {name: packed[which].at[row0:row0 + rows, 0:lanes] for name, (which, row0, rows, lanes) in layout}
    dall_ref, gq_ref, gk_ref, gs_ref, bd_ref = tab["dall"], tab["gq"], tab["gk"], tab["gs"], tab["bd"]
    hm_ref, tri3_ref, lsgn_ref, lvl_ref, e64_ref = tab["hm"], tab["tri3"], tab["lsgn"], tab["lvl"], tab["e64"]
    xo_ref, conv_o, h_o, sret_o, shg_o = [next(it) for _ in range(5)]
    scr = list(it)
    sets = (scr[0:11], scr[11:22])
    conv_c, h_c, sret_c, shg_c = scr[22:26]
    wg_s = scr[26]
    seg_rows = blk if bt == 1 else tile
    dn_alpha = (2 * depth) ** 0.25
    all_layers = layer is None
    if all_layers:
        xs_s = scr[27]
        lyr, blk_id = pl.program_id(0), pl.program_id(1)
        first_tile = last_tile = True
        new_weights = blk_id == 0

        @pl.when(lyr == 0)
        def _take_input():
            xs_s[blk_id] = x_ref[...].reshape(n, D_MODEL)
    else:
        lyr = layer
        first_tile = pl.program_id(1) == 0
        last_tile = pl.program_id(1) == pl.num_programs(1) - 1
        new_weights = (pl.program_id(0) == 0) & (pl.program_id(1) == 0)

    @pl.when(first_tile)
    def _load_state():
        if zero_init:
            for ref in (conv_c, h_c, sret_c, shg_c):
                ref[...] = jnp.zeros(ref.shape, F32)
        else:
            conv_c[:, 0:SUBLANES - (CONV_W - 1), :] = jnp.zeros((bt, SUBLANES - (CONV_W - 1), RG_WIDTH), F32)
            conv_c[:, SUBLANES - (CONV_W - 1):SUBLANES, :] = conv_in[...]
            h_c[...] = jnp.broadcast_to(h_in[...], (bt, SUBLANES, RG_WIDTH))
            for bi in range(bt):
                sret_c[bi] = _expand_state(sret_in[bi], bd_ref[...])
                shg_c[bi] = _expand_state(_head_transpose(shg_in[bi]), bd_ref[...])

    @pl.when(new_weights)
    def _expand_gate_weights():
        for gi, ref in enumerate((wa_ref, wx_ref)):
            for hf in range(2):
                wg_s[gi, hf] = _expand_state(ref[hf], bd_ref[...]).astype(BF16)

    row = pl.ds(lyr, 1)
    conv_b = convb_ref[row, :]
    conv_w = [convw_ref[j:j + 1, :] for j in range(CONV_W)]
    rg_ba = rgba_ref[row, :]
    rg_bx = rgbx_ref[row, :]
    c_lam = RG_C * jax.nn.log_sigmoid(rglam_ref[row, :])
    gn_g = gng_ref[row, :]
    hg_bf = hgbf_ref[row, :]
    hg_ng = hgng_ref[row, :]
    logits = [lbl_ref[li:li + 1, :] for li in range(depth)]
    mx = functools.reduce(jnp.maximum, logits)
    ex = [jnp.exp(v - mx) for v in logits]
    den = functools.reduce(lambda s, v: s + v, ex)
    probs = [v / den for v in ex]
    lbs = [functools.reduce(lambda s, v: s + v, probs[:li + 1]) - probs[0] for li in range(depth)]
    if all_layers:
        lb = functools.reduce(lambda acc, li: jnp.where(lyr == li, lbs[li], acc), range(1, depth), lbs[0])
    else:
        lb = lbs[layer]
    one_m_lb = 1.0 - lb

    def project_steps(xblk, dst):
        xb_s = dst[10]

        def stage():
            xb_s[...] = xblk.astype(BF16)

        def step(lo):
            def run():
                val = _dot(xb_s[...], win_ref[:, lo:lo + PROJ_COLS])
                for gi in range(2 + N_NARROW):
                    g_lo, g_hi = max(SEG[gi], lo), min(SEG[gi + 1], lo + PROJ_COLS)
                    if g_lo >= g_hi:
                        continue
                    part = val[:, g_lo - lo:g_hi - lo]
                    if gi == 0:
                        for si in range(blk // seg_rows):
                            dst[0][si, SUBLANES:SUBLANES + seg_rows, g_lo:g_hi] = part[si * seg_rows:(si + 1) * seg_rows]
                    else:
                        dst[gi][:, g_lo - SEG[gi]:g_hi - SEG[gi]] = part
            return run

        return [stage] + [step(lo) for lo in range(0, D_IN, PROJ_COLS)]

    def mix(src, base, filler=()):
        filler = list(filler)
        n_ch = blk // c
        lockstep = n_ch if bt > 1 else min(n_ch, LOCKSTEP)
        n_grp = -(-n_ch // lockstep)
        fill_total = 2.0 + 4 * n_ch + 2 * n_grp
        issued = [0.0, 0]

        def fill(weight):
            issued[0] += weight
            due = math.ceil(len(filler) * min(issued[0] / fill_total, 1.0) - 1e-9)
            while issued[1] < due:
                filler[issued[1]]()
                issued[1] += 1

        rgx_s, rgg_s, q_s, k_s, v_s, rgate_s, hq_s, hf_s, hi_s, hgate_s = src[:10]
        rgx_s[:, 0:SUBLANES, :] = conv_c[...]
        us = []
        for si in range(blk // seg_rows):
            win = rgx_s[si]
            u_seg = conv_b + win[SUBLANES:, :] * conv_w[CONV_W - 1]
            for back in range(1, CONV_W):
                u_seg = u_seg + pltpu.roll(win, back, axis=0)[SUBLANES:, :] * conv_w[CONV_W - 1 - back]
            us.append(u_seg)
        conv_c[...] = rgx_s[:, seg_rows:seg_rows + SUBLANES, :]
        u = us[0] if len(us) == 1 else jnp.concatenate(us, axis=0)
        fill(1.0)
        ub = u.astype(BF16)
        half = RG_WIDTH // 2
        r_pre = jnp.concatenate([_dot(ub[:, :half], wg_s[0, 0]), _dot(ub[:, half:], wg_s[0, 1])], axis=1)
        i_pre = jnp.concatenate([_dot(ub[:, :half], wg_s[1, 0]), _dot(ub[:, half:], wg_s[1, 1])], axis=1)
        log_a = c_lam * jax.nn.sigmoid(r_pre + rg_ba)
        a = jnp.exp(log_a)
        b_in = jnp.sqrt(-jnp.tanh(log_a) * (a * a + 1.0)) * (jax.nn.sigmoid(i_pre + rg_bx) * u)
        fill(1.0)
        e64 = e64_ref[...]
        hm = hm_ref[...]
        lvl = lvl_ref[...]
        first_half = (lax.broadcasted_iota(jnp.int32, (c, HW), 1) % HEAD_DIM) < HEAD_DIM // 2

        per_seq = seg_rows // c
        st_all = [dict(rows=slice(ci * c, (ci + 1) * c), bi=ci // per_seq,
                       trow=_rows(base + (ci % per_seq) * c, c)) for ci in range(n_ch)]

        for g0 in range(0, n_ch, lockstep):
            st = st_all[g0:g0 + lockstep]
            for d in st:
                rows, trow = d["rows"], d["trow"]
                cos = jnp.concatenate([cos_ref[trow, :]] * (HW // LANES), axis=1)
                sin = jnp.concatenate([sin_ref[trow, :]] * (HW // LANES), axis=1)
                d["kr"] = _rotary(k_s[rows, :], cos, sin, first_half)
                d["qb"] = _rotary(q_s[rows, :], cos, sin, first_half).astype(BF16)
                d["scores"] = _dot_t(d["qb"], _stack_heads(d["kr"], hm))
                z = hf_s[rows, :] + hg_bf
                ez = jnp.exp(-jnp.abs(z))
                inv = 1.0 / (1.0 + ez)
                pos = z >= 0.0
                sig_p = jnp.where(pos, inv, ez * inv)
                sig_n = jnp.where(pos, ez * inv, inv)
                log_f = jnp.log(jnp.maximum(lb + one_m_lb * sig_p, F_EPS))
                d["kc"] = one_m_lb * sig_n
                f_hi = log_f.astype(BF16)
                res = log_f - f_hi.astype(F32)
                f_mid = res.astype(BF16)
                f_lo = (res - f_mid.astype(F32)).astype(BF16)
                d["cum"] = _dot(tri3_ref[...], jnp.concatenate([f_hi, f_mid, f_lo], axis=0))
                fill(1.0)

            for d in st:
                bi, kr, qb = d["bi"], d["kr"], d["qb"]
                v = v_s[d["rows"], :]
                p = (d["scores"] * dall_ref[...]).astype(BF16)
                s_ret = sret_c[bi]
                d["o_b"] = _dot(p, _stack_heads(v, hm)) + _dot(qb, s_ret.astype(BF16)) * gq_ref[...]
                _update_state(sret_c, bi, s_ret, gs_ref, bd_ref, (kr * gk_ref[...]).astype(BF16), v.astype(BF16))
                fill(1.0)
                cum, kc = d["cum"], d["kc"]
                qh = hq_s[d["rows"], :]
                e0 = jnp.exp(cum - _group_row(cum, LEVEL0_ROWS, LEVEL0_ROWS // 2 - 1))
                s0 = _dot_t((qh * e0).astype(BF16), _stack_heads(kc / e0, hm))
                pm = jnp.where(lvl == 0.0, s0, 0.0)
                for lv in range(1, n_lv + 1):
                    g = LEVEL0_ROWS * 2 ** lv
                    el = jnp.exp(lsgn_ref[(lv - 1) * c:lv * c, :] * (cum - _group_row(cum, g, g // 2 - 1)))
                    sl_ = _dot_t((qh * el).astype(BF16), _stack_heads(kc * el, hm))
                    pm = jnp.where(lvl == float(lv), sl_, pm)
                d["pm"] = pm
                fill(1.0)

            means = _seg_means([d["o_b"] for d in st], e64)
            for d, m in zip(st, means):
                d["dev"] = d["o_b"] - m
            fill(1.0)
            for d in st:
                bi, cum = d["bi"], d["cum"]
                vh = hi_s[d["rows"], :]
                qh = hq_s[d["rows"], :]
                s_hg = shg_c[bi]
                d["o_c"] = (_dot(d["pm"].astype(BF16), _stack_heads(vh, hm))
                            + _dot_t((qh * jnp.exp(cum)).astype(BF16), s_hg.astype(BF16)))
                k_st = (d["kc"] * jnp.exp(cum[c - 1:c, :] - cum)).astype(BF16)
                dec = jnp.exp(cum[c - 1:c, :])
                _update_state(shg_c, bi, s_hg, dec, bd_ref, vh.astype(BF16), k_st)
                fill(1.0)

            stats = _seg_means([d["dev"] * d["dev"] for d in st] + [d["o_c"] * d["o_c"] for d in st], e64)
            fill(1.0)
            for d, var_b, ms_c in zip(st, stats[:len(st)], stats[len(st):]):
                d["var_b"], d["ms_c"] = var_b, ms_c

        ys = []
        for d in st_all:
            rows, bi = d["rows"], d["bi"]
            h, h_last = _rg_scan(a[rows], b_in[rows], h_c[bi])
            h_c[bi] = h_last
            y_a = h * _silu(rgg_s[rows, :])
            y_b = d["dev"] * lax.rsqrt(d["var_b"] + LN_EPS) * gn_g * _silu(rgate_s[rows, :])
            y_c = d["o_c"] * lax.rsqrt(d["ms_c"] + LN_EPS) * hg_ng * _silu(hgate_s[rows, :])
            ys.append(jnp.concatenate([y_a.astype(BF16), y_b.astype(BF16), y_c.astype(BF16)], axis=1))
        y = ys[0] if n_ch == 1 else jnp.concatenate(ys, axis=0)

        assert abs(issued[0] - fill_total) < 1e-6 and issued[1] == len(filler)

        if bt == 1:
            xrows = _rows(base, blk)
            x_in = x_ref[0, xrows, :]
        elif all_layers:
            x_in = xs_s[blk_id]
        else:
            x_in = x_ref[...].reshape(n, D_MODEL)
        xn = dn_alpha * x_in + _dot(y, wout_ref[...])
        mu = jnp.mean(xn, axis=-1, keepdims=True)
        dv = xn - mu
        var = jnp.mean(dv * dv, axis=-1, keepdims=True)
        out = dv * lax.rsqrt(var + LN_EPS) * lng_ref[row, :] + lnb_ref[row, :]
        if bt == 1:
            xo_ref[0, xrows, :] = out
        else:
            xo_ref[...] = out.reshape(bt, tile, D_MODEL)
            if all_layers:
                xs_s[blk_id] = out

    def project(xblk, dst):
        for run in project_steps(xblk, dst):
            run()

    if not lookahead:
        project(xs_s[blk_id] if all_layers else x_ref[...].reshape(n, D_MODEL), sets[0])
        mix(sets[0], 0)
    else:
        @pl.when((pl.program_id(0) == 0) & (pl.program_id(1) == 0))
        def _first_block():
            project(x_ref[0, 0:blk, :], sets[0])

        def pair_body(k, carry):
            base0 = pl.multiple_of(2 * k * blk, 2 * blk)
            mix(sets[0], base0, project_steps(x_ref[0, _rows(base0 + blk, blk), :], sets[1]))
            in_tile = 2 * k + 2 < nb
            nxt = jnp.minimum(2 * k + 2, nb - 1) * blk
            ahead = jnp.where(in_tile, x_ref[0, _rows(nxt, blk), :], xn_ref[0])
            mix(sets[1], base0 + blk, project_steps(ahead, sets[0]))
            return carry

        lax.fori_loop(0, nb // 2, pair_body, 0)

    @pl.when(last_tile)
    def _store_state():
        conv_o[...] = conv_c[:, SUBLANES - (CONV_W - 1):SUBLANES, :]
        h_o[...] = h_c[:, 0:1, :]
        for bi in range(bt):
            sret_o[bi] = _compact_state(sret_c[bi])
            shg_o[bi] = _head_transpose(_compact_state(shg_c[bi]))


def _tiling(batch, seq):
    chunk = min(MAX_CHUNK, seq)
    tile = min(seq, MAX_TILE)
    bt = 1 if tile >= BLOCK_ROWS else min(batch, BLOCK_ROWS // tile)
    blk = min(BLOCK_ROWS, bt * tile)
    lookahead = bt == 1 and (tile // blk) >= 2
    assert seq % tile == 0 and batch % bt == 0 and tile % chunk == 0 and chunk % 16 == 0 and blk % chunk == 0
    assert lookahead or bt * tile == blk
    assert not lookahead or (tile // blk) % 2 == 0
    return bt, tile, chunk, blk, lookahead


def _scratch_shapes(bt, tile, blk):
    seg_rows = blk if bt == 1 else tile
    one_set = ([pltpu.VMEM((blk // seg_rows, SUBLANES + seg_rows, RG_WIDTH), F32), pltpu.VMEM((blk, RG_WIDTH), F32)]
               + [pltpu.VMEM((blk, HW), F32)] * N_NARROW + [pltpu.VMEM((blk, D_MODEL), BF16)])
    carries = [pltpu.VMEM((bt, SUBLANES, RG_WIDTH), F32)] * 2 + [pltpu.VMEM((bt, HW, HW), F32)] * 2
    gate_w = [pltpu.VMEM((2, 2, HW, HW), BF16)]
    return one_set * 2 + carries + gate_w


def _all_layers_call(kern, depth, x, rope, states, params, tabs, bt, tile, blk, grid):
    batch = x.shape[0]
    state_tails = ((CONV_W - 1, RG_WIDTH), (1, RG_WIDTH), (HW, HEAD_DIM), (HW, HEAD_DIM))

    def const(arr):
        zeros = (0,) * arr.ndim
        return pl.BlockSpec(arr.shape, lambda l, b: zeros)

    def per_layer(arr):
        tail = arr.shape[1:]
        return pl.BlockSpec((None,) + tail, lambda l, b: (l,) + (0,) * len(tail))

    def per_lb(tail):
        return pl.BlockSpec((None, bt) + tail, lambda l, b: (l, b) + (0,) * len(tail))

    x_spec = pl.BlockSpec((bt, tile, D_MODEL), lambda l, b: (b, 0, 0))
    rope_spec = pl.BlockSpec((tile, LANES), lambda l, b: (0, 0))
    args = [x] + list(rope) + list(states) + list(params) + list(tabs)
    specs = ([x_spec] + [rope_spec] * 2 + [per_lb(t) for t in state_tails]
             + [per_layer(p) if p.ndim > 2 else const(p) for p in params] + [const(t) for t in tabs])
    out_shape = ([jax.ShapeDtypeStruct(x.shape, F32)]
                 + [jax.ShapeDtypeStruct((depth, batch) + t, F32) for t in state_tails])
    out_specs = [x_spec] + [per_lb(t) for t in state_tails]
    stream = [pltpu.VMEM((batch // bt, bt * tile, D_MODEL), F32)]
    return pl.pallas_call(
        kern, grid=grid, in_specs=specs, out_specs=out_specs, out_shape=out_shape,
        scratch_shapes=_scratch_shapes(bt, tile, blk) + stream,
        compiler_params=pltpu.CompilerParams(dimension_semantics=("arbitrary", "arbitrary"),
                                             vmem_limit_bytes=VMEM_LIMIT_BYTES),
        name=f"layers_t{tile}",
    )(*args)


def _layer_call(layer, depth, x, rope, states, params, tabs, layout, n_lv, tiling):
    bt, tile, chunk, blk, lookahead = tiling
    batch, seq, _ = x.shape
    n_tiles = seq // tile
    all_layers = layer is None
    assert not all_layers or (n_tiles == 1 and not lookahead and states is not None)
    grid = (depth, batch // bt) if all_layers else (batch // bt, n_tiles)
    n_steps = grid[0] * grid[1]
    zero_init = states is None
    kern = functools.partial(_layer_kernel, (layer, depth, bt, tile, chunk, n_lv, blk, lookahead, zero_init, layout))
    if all_layers:
        return _all_layers_call(kern, depth, x, rope, states, params, tabs, bt, tile, blk, grid)

    def layer_const(arr):
        tail = arr.shape[1:]
        return pl.BlockSpec((None,) + tail, lambda b, i: (layer,) + (0,) * len(tail))

    def const(arr):
        zeros = (0,) * arr.ndim
        return pl.BlockSpec(arr.shape, lambda b, i: zeros)

    def per_b(tail):
        return pl.BlockSpec((bt,) + tail, lambda b, i: (b,) + (0,) * len(tail))

    def per_lb(tail):
        return pl.BlockSpec((None, bt) + tail, lambda b, i: (layer, b) + (0,) * len(tail))

    def next_block(b, i):
        f = jnp.minimum(b * n_tiles + i + 1, n_steps - 1)
        return (f // n_tiles, (f % n_tiles) * (tile // blk), 0)

    state_tails = ((CONV_W - 1, RG_WIDTH), (1, RG_WIDTH), (HW, HEAD_DIM), (HW, HEAD_DIM))
    x_spec = pl.BlockSpec((bt, tile, D_MODEL), lambda b, i: (b, i, 0))
    rope_spec = pl.BlockSpec((tile, LANES), lambda b, i: (i, 0))
    args, specs = [x], [x_spec]
    if lookahead:
        args.append(x)
        specs.append(pl.BlockSpec((1, blk, D_MODEL), next_block))
    args += list(rope)
    specs += [rope_spec] * 2
    if not zero_init:
        args += list(states)
        specs += [per_lb(t) for t in state_tails]
    args += list(params)
    specs += [layer_const(p) if p.ndim > 2 else const(p) for p in params]
    args += list(tabs)
    specs += [const(t) for t in tabs]

    out_shape = [jax.ShapeDtypeStruct(x.shape, F32)] + [jax.ShapeDtypeStruct((batch,) + t, F32) for t in state_tails]
    out_specs = [x_spec] + [per_b(t) for t in state_tails]
    return pl.pallas_call(
        kern, grid=grid, in_specs=specs, out_specs=out_specs, out_shape=out_shape,
        scratch_shapes=_scratch_shapes(bt, tile, blk),
        compiler_params=pltpu.CompilerParams(dimension_semantics=("arbitrary", "arbitrary"),
                                             vmem_limit_bytes=VMEM_LIMIT_BYTES),
        name=f"layer{layer}_t{tile}",
    )(*args)


def kernel(x_prompt, x_sample, cache_conv, state_rglru, state_ret, state_hgrn, w_in, conv_w, conv_b, rg_wa, rg_ba,
           rg_wx, rg_bx, rg_lambda, ret_gn_g, hg_bf, hg_lb_logits, hg_norm_g, w_out, ln_g, ln_b):
    depth = w_in.shape[0]
    bp, lp, _ = x_prompt.shape
    bs, ls, _ = x_sample.shape

    streams = []
    for batch, seq, pos0 in ((bp, lp, 0), (bs, ls, PAST_LEN)):
        tiling = _tiling(batch, seq)
        tabs_np, n_lv = _mixer_tables(tiling[2])
        (tab_f32, tab_bf16), layout = _pack_tables(tabs_np)
        tabs = (jnp.asarray(tab_f32, F32), jnp.asarray(tab_bf16, BF16))
        rope = tuple(jnp.asarray(t, F32) for t in _rope_tables(pos0 + np.arange(seq)))
        streams.append((tiling, n_lv, tabs, layout, rope))

    halves = lambda w: w.reshape(depth, 2, HW, RG_BLOCK)
    params = (w_in.astype(BF16), halves(rg_wa), halves(rg_wx), w_out.astype(BF16), conv_w, conv_b, rg_ba, rg_bx,
              rg_lambda, ret_gn_g, hg_bf, hg_norm_g, hg_lb_logits, ln_g, ln_b)
    sample_states = (cache_conv, state_rglru[:, :, None, :], state_ret.reshape(depth, bs, HW, HEAD_DIM),
                     state_hgrn.reshape(depth, bs, HW, HEAD_DIM))

    xp = x_prompt
    outs_p = []
    tiling, n_lv, tabs, layout, rope = streams[0]
    for l in range(depth):
        res = _layer_call(l, depth, xp, rope, None, params, tabs, layout, n_lv, tiling)
        xp = res[0]
        outs_p.append(res[1:])
    conv_p = jnp.stack([o[0] for o in outs_p])
    h_p = jnp.stack([o[1][:, 0, :] for o in outs_p])
    ret_p = jnp.stack([o[2] for o in outs_p]).reshape(depth, bp, HEADS, HEAD_DIM, HEAD_DIM)
    hg_p = jnp.stack([o[3] for o in outs_p]).reshape(depth, bp, HEADS, HEAD_DIM, HEAD_DIM)

    tiling, n_lv, tabs, layout, rope = streams[1]
    xs, conv_s, h_s, ret_s, hg_s = _layer_call(None, depth, x_sample, rope, sample_states, params, tabs, layout, n_lv,
                                               tiling)
    shape_s = (depth, bs, HEADS, HEAD_DIM, HEAD_DIM)
    return (xp, xs, conv_p, h_p, ret_p, hg_p,
            conv_s, h_s.reshape(depth, bs, RG_WIDTH), ret_s.reshape(shape_s), hg_s.reshape(shape_s))
```

```python
import functools
import math

import jax
import jax.numpy as jnp
import numpy as np
from jax import lax
from jax.experimental import pallas as pl
from jax.experimental.pallas import tpu as pltpu

F32 = jnp.float32
BF16 = jnp.bfloat16

D_MODEL = 1024
RG_WIDTH = 512
RG_BLOCKS = 8
RG_BLOCK = RG_WIDTH // RG_BLOCKS
CONV_W = 4
RG_C = 8.0
HEADS = 4
HEAD_DIM = 64
HW = HEADS * HEAD_DIM
ROPE_BASE = 10000.0
LN_EPS = 1e-5
F_EPS = 1e-6
PAST_LEN = 1024
SPLITS = (RG_WIDTH, RG_WIDTH, HW, HW, HW, HW, HW, HW, HW, HW)
D_IN = sum(SPLITS)
SEG = tuple(int(v) for v in np.cumsum((0,) + SPLITS))
N_NARROW = 8

SUBLANES = 8
LANES = 128
VMEM_LIMIT_BYTES = 56 * 1024 * 1024

MAX_CHUNK = 64
BLOCK_ROWS = 256
PROJ_COLS = 256
MAX_TILE = 1024
LOCKSTEP = 1
LEVEL0_ROWS = SUBLANES

QUADS = ((slice(0, 128), slice(0, 128)), (slice(128, 256), slice(128, 256)))


def _mixer_tables(chunk):
    c = chunk
    t = np.arange(c)
    hid = np.repeat(np.arange(HEADS), HEAD_DIM)
    row_h = np.repeat(np.arange(HEADS), c)
    scale = HEAD_DIM ** -0.5

    log_g = np.log1p(-np.exp2(-5.0 - np.arange(HEADS)))
    rel = t[:, None] - t[None, :]
    dmat = np.where(rel >= 0, np.exp(np.maximum(rel, 0)[None] * log_g[:, None, None]), 0.0)
    bd = (hid[:, None] == hid[None, :]).astype(np.float64)
    tabs = {
        "dall": scale * dmat.transpose(1, 0, 2).reshape(c, HEADS * c),
        "gq": scale * np.repeat(np.exp((t[:, None] + 1.0) * log_g[None, :]), HEAD_DIM, 1),
        "gk": np.exp((c - 1.0 - t)[:, None] * log_g[None, :])[:, hid],
        "gs": np.exp(c * log_g)[hid][:, None] * np.ones((1, HW)),
        "bd": bd,
        "hm": (row_h[:, None] == hid[None, :]).astype(np.float64),
        "e64": bd / HEAD_DIM,
    }

    n_lv = int(round(math.log2(c // LEVEL0_ROWS)))
    tri = (t[None, :] <= t[:, None]).astype(np.float64)
    signs = []
    lvl = np.full((c, c), -1.0)
    causal = t[None, :] <= t[:, None]
    same0 = (t[:, None] // LEVEL0_ROWS) == (t[None, :] // LEVEL0_ROWS)
    lvl[causal & same0] = 0.0
    assigned = same0.copy()
    for lv in range(1, n_lv + 1):
        g = LEVEL0_ROWS * 2 ** lv
        upper = (t % g) >= g // 2
        signs.append(np.where(upper, 1.0, -1.0)[:, None] * np.ones((1, HW)))
        same = (t[:, None] // g) == (t[None, :] // g)
        lvl[causal & same & ~assigned] = float(lv)
        assigned |= same
    tabs["tri3"] = np.concatenate([tri, tri, tri], axis=1)
    tabs["lsgn"] = np.concatenate(signs, axis=0)
    tabs["lvl"] = np.tile(lvl, (1, HEADS))
    return tabs, n_lv


_F32_TABLES = ("dall", "gq", "gk", "lvl", "lsgn", "gs", "bd")
_BF16_TABLES = ("hm", "e64", "tri3")


def _pack_tables(tabs):
    packed, layout = [], {}
    for which, names in enumerate((_F32_TABLES, _BF16_TABLES)):
        row0, parts = 0, []
        for name in names:
            t = tabs[name]
            layout[name] = (which, row0, t.shape[0], t.shape[1])
            parts.append(np.pad(t, ((0, 0), (0, HW - t.shape[1]))))
            row0 += t.shape[0]
        packed.append(np.concatenate(parts, axis=0))
    return packed, tuple(sorted(layout.items()))


def _rope_tables(pos):
    half = HEAD_DIM // 2
    inv = ROPE_BASE ** (-np.arange(half, dtype=np.float64) / half)
    ang = pos.astype(np.float64)[:, None] * inv[None, :]
    cos = np.tile(np.cos(ang), (1, 2 * LANES // HEAD_DIM))
    sin = np.tile(np.concatenate([-np.sin(ang), np.sin(ang)], axis=1), (1, LANES // HEAD_DIM))
    return cos, sin


def _silu(x):
    return x * jax.nn.sigmoid(x)


def _dot(a, b):
    return jnp.dot(a, b, preferred_element_type=F32)


def _dot_t(a, b):
    return lax.dot_general(a, b, (((1,), (1,)), ((), ())), preferred_element_type=F32)


def _tdot(a, b):
    return lax.dot_general(a, b, (((0,), (0,)), ((), ())), preferred_element_type=F32)


def _seg_means(xs, e64):
    rows = xs[0].shape[0]
    x = xs[0] if len(xs) == 1 else jnp.concatenate(xs, axis=0)
    m = _dot(x.astype(BF16), e64)
    return [m[i * rows:(i + 1) * rows] for i in range(len(xs))]


def _stack_heads(x, mask):
    xb = x.astype(BF16)
    return jnp.concatenate([xb] * HEADS, axis=0) * mask


def _rotary(x, cos, sin, first_half):
    partner = jnp.where(first_half, pltpu.roll(x, HW - HEAD_DIM // 2, axis=1), pltpu.roll(x, HEAD_DIM // 2, axis=1))
    return x * cos + partner * sin


def _rg_scan(a, b, h_prev):
    c, w = a.shape
    g = c // SUBLANES
    a3 = a.reshape(g, SUBLANES, w)
    b3 = b.reshape(g, SUBLANES, w)
    sub = lax.broadcasted_iota(jnp.int32, a3.shape, 1)
    shift = 1
    while shift < SUBLANES:
        keep = sub >= shift
        a_sh = pltpu.roll(a3, shift, axis=1)
        b_sh = pltpu.roll(b3, shift, axis=1)
        b3 = jnp.where(keep, a3 * b_sh + b3, b3)
        a3 = jnp.where(keep, a3 * a_sh, a3)
        shift *= 2
    outs = []
    hb = h_prev
    for gi in range(g):
        hg = a3[gi] * hb + b3[gi]
        outs.append(hg)
        hb = jnp.broadcast_to(hg[SUBLANES - 1:SUBLANES, :], (SUBLANES, w))
    return jnp.concatenate(outs, axis=0), hb


def _group_row(x, g, r):
    return jnp.concatenate([jnp.broadcast_to(x[g0 + r:g0 + r + 1, :], (g, x.shape[1]))
                            for g0 in range(0, x.shape[0], g)], axis=0)


def _rows(start, size):
    return pl.ds(start if isinstance(start, int) else pl.multiple_of(start, size), size)


def _update_state(ref, bi, old, scale, mask_ref, row_op, col_op):
    for rs, cs in QUADS:
        sc = scale[rs, cs] if scale.shape[0] > 1 else scale[:, cs]
        ref[bi, rs, cs] = sc * old[rs, cs] + mask_ref[rs, cs] * _tdot(row_op[:, rs], col_op[:, cs])


def _head_transpose(s):
    return jnp.concatenate([s[h * HEAD_DIM:(h + 1) * HEAD_DIM, :].T for h in range(HEADS)], axis=0)


def _expand_state(s, bd):
    return jnp.concatenate([s] * HEADS, axis=1) * bd


def _compact_state(s):
    out = s[:, 0:HEAD_DIM]
    for h in range(1, HEADS):
        out = out + s[:, h * HEAD_DIM:(h + 1) * HEAD_DIM]
    return out


def _layer_kernel(cfg, *refs):
    layer, depth, bt, tile, c, n_lv, blk, lookahead, zero_init, layout = cfg
    n = bt * tile
    nb = n // blk
    it = iter(refs)
    x_ref = next(it)
    xn_ref = next(it) if lookahead else None
    cos_ref, sin_ref = next(it), next(it)
    conv_in, h_in, sret_in, shg_in = (None,) * 4 if zero_init else (next(it), next(it), next(it), next(it))
    (win_ref, wa_ref, wx_ref, wout_ref, convw_ref, convb_ref, rgba_ref, rgbx_ref, rglam_ref, gng_ref, hgbf_ref,
     hgng_ref, lbl_ref, lng_ref, lnb_ref) = [next(it) for _ in range(15)]
    packed = (next(it), next(it))
    tab = {name: packed[which].at[row0:row0 + rows, 0:lanes] for name, (which, row0, rows, lanes) in layout}
    dall_ref, gq_ref, gk_ref, gs_ref, bd_ref = tab["dall"], tab["gq"], tab["gk"], tab["gs"], tab["bd"]
    hm_ref, tri3_ref, lsgn_ref, lvl_ref, e64_ref = tab["hm"], tab["tri3"], tab["lsgn"], tab["lvl"], tab["e64"]
    xo_ref, conv_o, h_o, sret_o, shg_o = [next(it) for _ in range(5)]
    scr = list(it)
    sets = (scr[0:11], scr[11:22])
    conv_c, h_c, sret_c, shg_c = scr[22:26]
    wg_s = scr[26]
    seg_rows = blk if bt == 1 else tile
    dn_alpha = (2 * depth) ** 0.25
    all_layers = layer is None
    if all_layers:
        xs_s = scr[27]
        lyr, blk_id = pl.program_id(0), pl.program_id(1)
        first_tile = last_tile = True
        new_weights = blk_id == 0

        @pl.when(lyr == 0)
        def _take_input():
            xs_s[blk_id] = x_ref[...].reshape(n, D_MODEL)
    else:
        lyr = layer
        first_tile = pl.program_id(1) == 0
        last_tile = pl.program_id(1) == pl.num_programs(1) - 1
        new_weights = (pl.program_id(0) == 0) & (pl.program_id(1) == 0)

    @pl.when(first_tile)
    def _load_state():
        if zero_init:
            for ref in (conv_c, h_c, sret_c, shg_c):
                ref[...] = jnp.zeros(ref.shape, F32)
        else:
            conv_c[:, 0:SUBLANES - (CONV_W - 1), :] = jnp.zeros((bt, SUBLANES - (CONV_W - 1), RG_WIDTH), F32)
            conv_c[:, SUBLANES - (CONV_W - 1):SUBLANES, :] = conv_in[...]
            h_c[...] = jnp.broadcast_to(h_in[...], (bt, SUBLANES, RG_WIDTH))
            for bi in range(bt):
                sret_c[bi] = _expand_state(sret_in[bi], bd_ref[...])
                shg_c[bi] = _expand_state(_head_transpose(shg_in[bi]), bd_ref[...])

    @pl.when(new_weights)
    def _expand_gate_weights():
        for gi, ref in enumerate((wa_ref, wx_ref)):
            for hf in range(2):
                wg_s[gi, hf] = _expand_state(ref[hf], bd_ref[...]).astype(BF16)

    row = pl.ds(lyr, 1)
    conv_b = convb_ref[row, :]
    conv_w = [convw_ref[j:j + 1, :] for j in range(CONV_W)]
    rg_ba = rgba_ref[row, :]
    rg_bx = rgbx_ref[row, :]
    c_lam = RG_C * jax.nn.log_sigmoid(rglam_ref[row, :])
    gn_g = gng_ref[row, :]
    hg_bf = hgbf_ref[row, :]
    hg_ng = hgng_ref[row, :]
    logits = [lbl_ref[li:li + 1, :] for li in range(depth)]
    mx = functools.reduce(jnp.maximum, logits)
    ex = [jnp.exp(v - mx) for v in logits]
    den = functools.reduce(lambda s, v: s + v, ex)
    probs = [v / den for v in ex]
    lbs = [functools.reduce(lambda s, v: s + v, probs[:li + 1]) - probs[0] for li in range(depth)]
    if all_layers:
        lb = functools.reduce(lambda acc, li: jnp.where(lyr == li, lbs[li], acc), range(1, depth), lbs[0])
    else:
        lb = lbs[layer]
    one_m_lb = 1.0 - lb

    def project_steps(xblk, dst):
        xb_s = dst[10]

        def stage():
            xb_s[...] = xblk.astype(BF16)

        def step(lo):
            def run():
                val = _dot(xb_s[...], win_ref[:, lo:lo + PROJ_COLS])
                for gi in range(2 + N_NARROW):
                    g_lo, g_hi = max(SEG[gi], lo), min(SEG[gi + 1], lo + PROJ_COLS)
                    if g_lo >= g_hi:
                        continue
                    part = val[:, g_lo - lo:g_hi - lo]
                    if gi == 0:
                        for si in range(blk // seg_rows):
                            dst[0][si, SUBLANES:SUBLANES + seg_rows, g_lo:g_hi] = part[si * seg_rows:(si + 1) * seg_rows]
                    else:
                        dst[gi][:, g_lo - SEG[gi]:g_hi - SEG[gi]] = part
            return run

        return [stage] + [step(lo) for lo in range(0, D_IN, PROJ_COLS)]

    def mix(src, base, filler=()):
        filler = list(filler)
        n_ch = blk // c
        lockstep = n_ch if bt > 1 else min(n_ch, LOCKSTEP)
        n_grp = -(-n_ch // lockstep)
        fill_total = 2.0 + 4 * n_ch + 2 * n_grp
        issued = [0.0, 0]

        def fill(weight):
            issued[0] += weight
            due = math.ceil(len(filler) * min(issued[0] / fill_total, 1.0) - 1e-9)
            while issued[1] < due:
                filler[issued[1]]()
                issued[1] += 1

        rgx_s, rgg_s, q_s, k_s, v_s, rgate_s, hq_s, hf_s, hi_s, hgate_s = src[:10]
        rgx_s[:, 0:SUBLANES, :] = conv_c[...]
        us = []
        for si in range(blk // seg_rows):
            win = rgx_s[si]
            u_seg = conv_b + win[SUBLANES:, :] * conv_w[CONV_W - 1]
            for back in range(1, CONV_W):
                u_seg = u_seg + pltpu.roll(win, back, axis=0)[SUBLANES:, :] * conv_w[CONV_W - 1 - back]
            us.append(u_seg)
        conv_c[...] = rgx_s[:, seg_rows:seg_rows + SUBLANES, :]
        u = us[0] if len(us) == 1 else jnp.concatenate(us, axis=0)
        fill(1.0)
        ub = u.astype(BF16)
        half = RG_WIDTH // 2
        r_pre = jnp.concatenate([_dot(ub[:, :half], wg_s[0, 0]), _dot(ub[:, half:], wg_s[0, 1])], axis=1)
        i_pre = jnp.concatenate([_dot(ub[:, :half], wg_s[1, 0]), _dot(ub[:, half:], wg_s[1, 1])], axis=1)
        log_a = c_lam * jax.nn.sigmoid(r_pre + rg_ba)
        a = jnp.exp(log_a)
        b_in = jnp.sqrt(-jnp.tanh(log_a) * (a * a + 1.0)) * (jax.nn.sigmoid(i_pre + rg_bx) * u)
        fill(1.0)
        e64 = e64_ref[...]
        hm = hm_ref[...]
        lvl = lvl_ref[...]
        first_half = (lax.broadcasted_iota(jnp.int32, (c, HW), 1) % HEAD_DIM) < HEAD_DIM // 2

        per_seq = seg_rows // c
        st_all = [dict(rows=slice(ci * c, (ci + 1) * c), bi=ci // per_seq,
                       trow=_rows(base + (ci % per_seq) * c, c)) for ci in range(n_ch)]

        for g0 in range(0, n_ch, lockstep):
            st = st_all[g0:g0 + lockstep]
            for d in st:
                rows, trow = d["rows"], d["trow"]
                cos = jnp.concatenate([cos_ref[trow, :]] * (HW // LANES), axis=1)
                sin = jnp.concatenate([sin_ref[trow, :]] * (HW // LANES), axis=1)
                d["kr"] = _rotary(k_s[rows, :], cos, sin, first_half)
                d["qb"] = _rotary(q_s[rows, :], cos, sin, first_half).astype(BF16)
                d["scores"] = _dot_t(d["qb"], _stack_heads(d["kr"], hm))
                z = hf_s[rows, :] + hg_bf
                ez = jnp.exp(-jnp.abs(z))
                inv = 1.0 / (1.0 + ez)
                pos = z >= 0.0
                sig_p = jnp.where(pos, inv, ez * inv)
                sig_n = jnp.where(pos, ez * inv, inv)
                log_f = jnp.log(jnp.maximum(lb + one_m_lb * sig_p, F_EPS))
                d["kc"] = one_m_lb * sig_n
                f_hi = log_f.astype(BF16)
                res = log_f - f_hi.astype(F32)
                f_mid = res.astype(BF16)
                f_lo = (res - f_mid.astype(F32)).astype(BF16)
                d["cum"] = _dot(tri3_ref[...], jnp.concatenate([f_hi, f_mid, f_lo], axis=0))
                fill(1.0)

            for d in st:
                bi, kr, qb = d["bi"], d["kr"], d["qb"]
                v = v_s[d["rows"], :]
                p = (d["scores"] * dall_ref[...]).astype(BF16)
                s_ret = sret_c[bi]
                d["o_b"] = _dot(p, _stack_heads(v, hm)) + _dot(qb, s_ret.astype(BF16)) * gq_ref[...]
                _update_state(sret_c, bi, s_ret, gs_ref, bd_ref, (kr * gk_ref[...]).astype(BF16), v.astype(BF16))
                fill(1.0)
                cum, kc = d["cum"], d["kc"]
                qh = hq_s[d["rows"], :]
                e0 = jnp.exp(cum - _group_row(cum, LEVEL0_ROWS, LEVEL0_ROWS // 2 - 1))
                s0 = _dot_t((qh * e0).astype(BF16), _stack_heads(kc / e0, hm))
                pm = jnp.where(lvl == 0.0, s0, 0.0)
                for lv in range(1, n_lv + 1):
                    g = LEVEL0_ROWS * 2 ** lv
                    el = jnp.exp(lsgn_ref[(lv - 1) * c:lv * c, :] * (cum - _group_row(cum, g, g // 2 - 1)))
                    sl_ = _dot_t((qh * el).astype(BF16), _stack_heads(kc * el, hm))
                    pm = jnp.where(lvl == float(lv), sl_, pm)
                d["pm"] = pm
                fill(1.0)

            means = _seg_means([d["o_b"] for d in st], e64)
            for d, m in zip(st, means):
                d["dev"] = d["o_b"] - m
            fill(1.0)
            for d in st:
                bi, cum = d["bi"], d["cum"]
                vh = hi_s[d["rows"], :]
                qh = hq_s[d["rows"], :]
                s_hg = shg_c[bi]
                d["o_c"] = (_dot(d["pm"].astype(BF16), _stack_heads(vh, hm))
                            + _dot_t((qh * jnp.exp(cum)).astype(BF16), s_hg.astype(BF16)))
                k_st = (d["kc"] * jnp.exp(cum[c - 1:c, :] - cum)).astype(BF16)
                dec = jnp.exp(cum[c - 1:c, :])
                _update_state(shg_c, bi, s_hg, dec, bd_ref, vh.astype(BF16), k_st)
                fill(1.0)

            stats = _seg_means([d["dev"] * d["dev"] for d in st] + [d["o_c"] * d["o_c"] for d in st], e64)
            fill(1.0)
            for d, var_b, ms_c in zip(st, stats[:len(st)], stats[len(st):]):
                d["var_b"], d["ms_c"] = var_b, ms_c

        ys = []
        for d in st_all:
            rows, bi = d["rows"], d["bi"]
            h, h_last = _rg_scan(a[rows], b_in[rows], h_c[bi])
            h_c[bi] = h_last
            y_a = h * _silu(rgg_s[rows, :])
            y_b = d["dev"] * lax.rsqrt(d["var_b"] + LN_EPS) * gn_g * _silu(rgate_s[rows, :])
            y_c = d["o_c"] * lax.rsqrt(d["ms_c"] + LN_EPS) * hg_ng * _silu(hgate_s[rows, :])
            ys.append(jnp.concatenate([y_a.astype(BF16), y_b.astype(BF16), y_c.astype(BF16)], axis=1))
        y = ys[0] if n_ch == 1 else jnp.concatenate(ys, axis=0)

        assert abs(issued[0] - fill_total) < 1e-6 and issued[1] == len(filler)

        if bt == 1:
            xrows = _rows(base, blk)
            x_in = x_ref[0, xrows, :]
        elif all_layers:
            x_in = xs_s[blk_id]
        else:
            x_in = x_ref[...].reshape(n, D_MODEL)
        xn = dn_alpha * x_in + _dot(y, wout_ref[...])
        mu = jnp.mean(xn, axis=-1, keepdims=True)
        dv = xn - mu
        var = jnp.mean(dv * dv, axis=-1, keepdims=True)
        out = dv * lax.rsqrt(var + LN_EPS) * lng_ref[row, :] + lnb_ref[row, :]
        if bt == 1:
            xo_ref[0, xrows, :] = out
        else:
            xo_ref[...] = out.reshape(bt, tile, D_MODEL)
            if all_layers:
                xs_s[blk_id] = out

    def project(xblk, dst):
        for run in project_steps(xblk, dst):
            run()

    if not lookahead:
        project(xs_s[blk_id] if all_layers else x_ref[...].reshape(n, D_MODEL), sets[0])
        mix(sets[0], 0)
    else:
        @pl.when((pl.program_id(0) == 0) & (pl.program_id(1) == 0))
        def _first_block():
            project(x_ref[0, 0:blk, :], sets[0])

        def pair_body(k, carry):
            base0 = pl.multiple_of(2 * k * blk, 2 * blk)
            mix(sets[0], base0, project_steps(x_ref[0, _rows(base0 + blk, blk), :], sets[1]))
            in_tile = 2 * k + 2 < nb
            nxt = jnp.minimum(2 * k + 2, nb - 1) * blk
            ahead = jnp.where(in_tile, x_ref[0, _rows(nxt, blk), :], xn_ref[0])
            mix(sets[1], base0 + blk, project_steps(ahead, sets[0]))
            return carry

        lax.fori_loop(0, nb // 2, pair_body, 0)

    @pl.when(last_tile)
    def _store_state():
        conv_o[...] = conv_c[:, SUBLANES - (CONV_W - 1):SUBLANES, :]
        h_o[...] = h_c[:, 0:1, :]
        for bi in range(bt):
            sret_o[bi] = _compact_state(sret_c[bi])
            shg_o[bi] = _head_transpose(_compact_state(shg_c[bi]))


def _tiling(batch, seq):
    chunk = min(MAX_CHUNK, seq)
    tile = min(seq, MAX_TILE)
    bt = 1 if tile >= BLOCK_ROWS else min(batch, BLOCK_ROWS // tile)
    blk = min(BLOCK_ROWS, bt * tile)
    lookahead = bt == 1 and (tile // blk) >= 2
    assert seq % tile == 0 and batch % bt == 0 and tile % chunk == 0 and chunk % 16 == 0 and blk % chunk == 0
    assert lookahead or bt * tile == blk
    assert not lookahead or (tile // blk) % 2 == 0
    return bt, tile, chunk, blk, lookahead


def _scratch_shapes(bt, tile, blk):
    seg_rows = blk if bt == 1 else tile
    one_set = ([pltpu.VMEM((blk // seg_rows, SUBLANES + seg_rows, RG_WIDTH), F32), pltpu.VMEM((blk, RG_WIDTH), F32)]
               + [pltpu.VMEM((blk, HW), F32)] * N_NARROW + [pltpu.VMEM((blk, D_MODEL), BF16)])
    carries = [pltpu.VMEM((bt, SUBLANES, RG_WIDTH), F32)] * 2 + [pltpu.VMEM((bt, HW, HW), F32)] * 2
    gate_w = [pltpu.VMEM((2, 2, HW, HW), BF16)]
    return one_set * 2 + carries + gate_w


def _all_layers_call(kern, depth, x, rope, states, params, tabs, bt, tile, blk, grid):
    batch = x.shape[0]
    state_tails = ((CONV_W - 1, RG_WIDTH), (1, RG_WIDTH), (HW, HEAD_DIM), (HW, HEAD_DIM))

    def const(arr):
        zeros = (0,) * arr.ndim
        return pl.BlockSpec(arr.shape, lambda l, b: zeros)

    def per_layer(arr):
        tail = arr.shape[1:]
        return pl.BlockSpec((None,) + tail, lambda l, b: (l,) + (0,) * len(tail))

    def per_lb(tail):
        return pl.BlockSpec((None, bt) + tail, lambda l, b: (l, b) + (0,) * len(tail))

    x_spec = pl.BlockSpec((bt, tile, D_MODEL), lambda l, b: (b, 0, 0))
    rope_spec = pl.BlockSpec((tile, LANES), lambda l, b: (0, 0))
    args = [x] + list(rope) + list(states) + list(params) + list(tabs)
    specs = ([x_spec] + [rope_spec] * 2 + [per_lb(t) for t in state_tails]
             + [per_layer(p) if p.ndim > 2 else const(p) for p in params] + [const(t) for t in tabs])
    out_shape = ([jax.ShapeDtypeStruct((depth,) + x.shape, F32)]
                 + [jax.ShapeDtypeStruct((depth, batch) + t, F32) for t in state_tails])
    out_specs = [per_lb((tile, D_MODEL))] + [per_lb(t) for t in state_tails]
    stream = [pltpu.VMEM((batch // bt, bt * tile, D_MODEL), F32)]
    return pl.pallas_call(
        kern, grid=grid, in_specs=specs, out_specs=out_specs, out_shape=out_shape,
        scratch_shapes=_scratch_shapes(bt, tile, blk) + stream,
        compiler_params=pltpu.CompilerParams(dimension_semantics=("arbitrary", "arbitrary"),
                                             vmem_limit_bytes=VMEM_LIMIT_BYTES),
        name=f"layers_t{tile}",
    )(*args)


def _layer_call(layer, depth, x, rope, states, params, tabs, layout, n_lv, tiling):
    bt, tile, chunk, blk, lookahead = tiling
    batch, seq, _ = x.shape
    n_tiles = seq // tile
    all_layers = layer is None
    assert not all_layers or (n_tiles == 1 and not lookahead and states is not None)
    grid = (depth, batch // bt) if all_layers else (batch // bt, n_tiles)
    n_steps = grid[0] * grid[1]
    zero_init = states is None
    kern = functools.partial(_layer_kernel, (layer, depth, bt, tile, chunk, n_lv, blk, lookahead, zero_init, layout))
    if all_layers:
        return _all_layers_call(kern, depth, x, rope, states, params, tabs, bt, tile, blk, grid)

    def layer_const(arr):
        tail = arr.shape[1:]
        return pl.BlockSpec((None,) + tail, lambda b, i: (layer,) + (0,) * len(tail))

    def const(arr):
        zeros = (0,) * arr.ndim
        return pl.BlockSpec(arr.shape, lambda b, i: zeros)

    def per_b(tail):
        return pl.BlockSpec((bt,) + tail, lambda b, i: (b,) + (0,) * len(tail))

    def per_lb(tail):
        return pl.BlockSpec((None, bt) + tail, lambda b, i: (layer, b) + (0,) * len(tail))

    def next_block(b, i):
        f = jnp.minimum(b * n_tiles + i + 1, n_steps - 1)
        return (f // n_tiles, (f % n_tiles) * (tile // blk), 0)

    state_tails = ((CONV_W - 1, RG_WIDTH), (1, RG_WIDTH), (HW, HEAD_DIM), (HW, HEAD_DIM))
    x_spec = pl.BlockSpec((bt, tile, D_MODEL), lambda b, i: (b, i, 0))
    rope_spec = pl.BlockSpec((tile, LANES), lambda b, i: (i, 0))
    args, specs = [x], [x_spec]
    if lookahead:
        args.append(x)
        specs.append(pl.BlockSpec((1, blk, D_MODEL), next_block))
    args += list(rope)
    specs += [rope_spec] * 2
    if not zero_init:
        args += list(states)
        specs += [per_lb(t) for t in state_tails]
    args += list(params)
    specs += [layer_const(p) if p.ndim > 2 else const(p) for p in params]
    args += list(tabs)
    specs += [const(t) for t in tabs]

    out_shape = [jax.ShapeDtypeStruct(x.shape, F32)] + [jax.ShapeDtypeStruct((batch,) + t, F32) for t in state_tails]
    out_specs = [x_spec] + [per_b(t) for t in state_tails]
    return pl.pallas_call(
        kern, grid=grid, in_specs=specs, out_specs=out_specs, out_shape=out_shape,
        scratch_shapes=_scratch_shapes(bt, tile, blk),
        compiler_params=pltpu.CompilerParams(dimension_semantics=("arbitrary", "arbitrary"),
                                             vmem_limit_bytes=VMEM_LIMIT_BYTES),
        name=f"layer{layer}_t{tile}",
    )(*args)


def kernel(x_prompt, x_sample, cache_conv, state_rglru, state_ret, state_hgrn, w_in, conv_w, conv_b, rg_wa, rg_ba,
           rg_wx, rg_bx, rg_lambda, ret_gn_g, hg_bf, hg_lb_logits, hg_norm_g, w_out, ln_g, ln_b):
    depth = w_in.shape[0]
    bp, lp, _ = x_prompt.shape
    bs, ls, _ = x_sample.shape

    streams = []
    for batch, seq, pos0 in ((bp, lp, 0), (bs, ls, PAST_LEN)):
        tiling = _tiling(batch, seq)
        tabs_np, n_lv = _mixer_tables(tiling[2])
        (tab_f32, tab_bf16), layout = _pack_tables(tabs_np)
        tabs = (jnp.asarray(tab_f32, F32), jnp.asarray(tab_bf16, BF16))
        rope = tuple(jnp.asarray(t, F32) for t in _rope_tables(pos0 + np.arange(seq)))
        streams.append((tiling, n_lv, tabs, layout, rope))

    halves = lambda w: w.reshape(depth, 2, HW, RG_BLOCK)
    params = (w_in.astype(BF16), halves(rg_wa), halves(rg_wx), w_out.astype(BF16), conv_w, conv_b, rg_ba, rg_bx,
              rg_lambda, ret_gn_g, hg_bf, hg_norm_g, hg_lb_logits, ln_g, ln_b)
    sample_states = (cache_conv, state_rglru[:, :, None, :], state_ret.reshape(depth, bs, HW, HEAD_DIM),
                     state_hgrn.reshape(depth, bs, HW, HEAD_DIM))

    xp = x_prompt
    outs_p = []
    tiling, n_lv, tabs, layout, rope = streams[0]
    for l in range(depth):
        res = _layer_call(l, depth, xp, rope, None, params, tabs, layout, n_lv, tiling)
        xp = res[0]
        outs_p.append(res[1:])
    conv_p = jnp.stack([o[0] for o in outs_p])
    h_p = jnp.stack([o[1][:, 0, :] for o in outs_p])
    ret_p = jnp.stack([o[2] for o in outs_p]).reshape(depth, bp, HEADS, HEAD_DIM, HEAD_DIM)
    hg_p = jnp.stack([o[3] for o in outs_p]).reshape(depth, bp, HEADS, HEAD_DIM, HEAD_DIM)

    tiling, n_lv, tabs, layout, rope = streams[1]
    xs, conv_s, h_s, ret_s, hg_s = _layer_call(None, depth, x_sample, rope, sample_states, params, tabs, layout, n_lv,
                                               tiling)
    shape_s = (depth, bs, HEADS, HEAD_DIM, HEAD_DIM)
    return (xp, xs[depth - 1], conv_p, h_p, ret_p, hg_p,
            conv_s, h_s.reshape(depth, bs, RG_WIDTH), ret_s.reshape(shape_s), hg_s.reshape(shape_s))
```

```python
import functools
import math

import jax
import jax.numpy as jnp
import numpy as np
from jax import lax
from jax.experimental import pallas as pl
from jax.experimental.pallas import tpu as pltpu

F32 = jnp.float32
BF16 = jnp.bfloat16

D_MODEL = 1024
RG_WIDTH = 512
RG_BLOCKS = 8
RG_BLOCK = RG_WIDTH // RG_BLOCKS
CONV_W = 4
RG_C = 8.0
HEADS = 4
HEAD_DIM = 64
HW = HEADS * HEAD_DIM
ROPE_BASE = 10000.0
LN_EPS = 1e-5
F_EPS = 1e-6
PAST_LEN = 1024
SPLITS = (RG_WIDTH, RG_WIDTH, HW, HW, HW, HW, HW, HW, HW, HW)
D_IN = sum(SPLITS)
SEG = tuple(int(v) for v in np.cumsum((0,) + SPLITS))
N_NARROW = 8

SUBLANES = 8
LANES = 128
VMEM_LIMIT_BYTES = 56 * 1024 * 1024

MAX_CHUNK = 64
BLOCK_ROWS = 256
PROJ_COLS = 256
MAX_TILE = 1024
LOCKSTEP = 1
LEVEL0_ROWS = SUBLANES

QUADS = ((slice(0, 128), slice(0, 128)), (slice(128, 256), slice(128, 256)))


def _mixer_tables(chunk):
    c = chunk
    t = np.arange(c)
    hid = np.repeat(np.arange(HEADS), HEAD_DIM)
    row_h = np.repeat(np.arange(HEADS), c)
    scale = HEAD_DIM ** -0.5

    log_g = np.log1p(-np.exp2(-5.0 - np.arange(HEADS)))
    rel = t[:, None] - t[None, :]
    dmat = np.where(rel >= 0, np.exp(np.maximum(rel, 0)[None] * log_g[:, None, None]), 0.0)
    bd = (hid[:, None] == hid[None, :]).astype(np.float64)
    tabs = {
        "dall": scale * dmat.transpose(1, 0, 2).reshape(c, HEADS * c),
        "gq": scale * np.repeat(np.exp((t[:, None] + 1.0) * log_g[None, :]), HEAD_DIM, 1),
        "gk": np.exp((c - 1.0 - t)[:, None] * log_g[None, :])[:, hid],
        "gs": np.exp(c * log_g)[hid][:, None] * np.ones((1, HW)),
        "bd": bd,
        "hm": (row_h[:, None] == hid[None, :]).astype(np.float64),
        "e64": bd / HEAD_DIM,
    }

    n_lv = int(round(math.log2(c // LEVEL0_ROWS)))
    tri = (t[None, :] <= t[:, None]).astype(np.float64)
    signs = []
    lvl = np.full((c, c), -1.0)
    causal = t[None, :] <= t[:, None]
    same0 = (t[:, None] // LEVEL0_ROWS) == (t[None, :] // LEVEL0_ROWS)
    lvl[causal & same0] = 0.0
    assigned = same0.copy()
    for lv in range(1, n_lv + 1):
        g = LEVEL0_ROWS * 2 ** lv
        upper = (t % g) >= g // 2
        signs.append(np.where(upper, 1.0, -1.0)[:, None] * np.ones((1, HW)))
        same = (t[:, None] // g) == (t[None, :] // g)
        lvl[causal & same & ~assigned] = float(lv)
        assigned |= same
    tabs["tri3"] = np.concatenate([tri, tri, tri], axis=1)
    tabs["lsgn"] = np.concatenate(signs, axis=0)
    tabs["lvl"] = np.tile(lvl, (1, HEADS))
    return tabs, n_lv


_F32_TABLES = ("dall", "gq", "gk", "lvl", "lsgn", "gs", "bd")
_BF16_TABLES = ("hm", "e64", "tri3")


def _pack_tables(tabs):
    packed, layout = [], {}
    for which, names in enumerate((_F32_TABLES, _BF16_TABLES)):
        row0, parts = 0, []
        for name in names:
            t = tabs[name]
            layout[name] = (which, row0, t.shape[0], t.shape[1])
            parts.append(np.pad(t, ((0, 0), (0, HW - t.shape[1]))))
            row0 += t.shape[0]
        packed.append(np.concatenate(parts, axis=0))
    return packed, tuple(sorted(layout.items()))


def _rope_tables(pos):
    half = HEAD_DIM // 2
    inv = ROPE_BASE ** (-np.arange(half, dtype=np.float64) / half)
    ang = pos.astype(np.float64)[:, None] * inv[None, :]
    cos = np.tile(np.cos(ang), (1, 2 * LANES // HEAD_DIM))
    sin = np.tile(np.concatenate([-np.sin(ang), np.sin(ang)], axis=1), (1, LANES // HEAD_DIM))
    return cos, sin


def _silu(x):
    return x * jax.nn.sigmoid(x)


def _dot(a, b):
    return jnp.dot(a, b, preferred_element_type=F32)


def _dot_t(a, b):
    return lax.dot_general(a, b, (((1,), (1,)), ((), ())), preferred_element_type=F32)


def _tdot(a, b):
    return lax.dot_general(a, b, (((0,), (0,)), ((), ())), preferred_element_type=F32)


def _seg_means(xs, e64):
    rows = xs[0].shape[0]
    x = xs[0] if len(xs) == 1 else jnp.concatenate(xs, axis=0)
    m = _dot(x.astype(BF16), e64)
    return [m[i * rows:(i + 1) * rows] for i in range(len(xs))]


def _stack_heads(x, mask):
    xb = x.astype(BF16)
    return jnp.concatenate([xb] * HEADS, axis=0) * mask


def _rotary(x, cos, sin, first_half):
    partner = jnp.where(first_half, pltpu.roll(x, HW - HEAD_DIM // 2, axis=1), pltpu.roll(x, HEAD_DIM // 2, axis=1))
    return x * cos + partner * sin


def _rg_scan(a, b, h_prev):
    c, w = a.shape
    g = c // SUBLANES
    a3 = a.reshape(g, SUBLANES, w)
    b3 = b.reshape(g, SUBLANES, w)
    sub = lax.broadcasted_iota(jnp.int32, a3.shape, 1)
    shift = 1
    while shift < SUBLANES:
        keep = sub >= shift
        a_sh = pltpu.roll(a3, shift, axis=1)
        b_sh = pltpu.roll(b3, shift, axis=1)
        b3 = jnp.where(keep, a3 * b_sh + b3, b3)
        a3 = jnp.where(keep, a3 * a_sh, a3)
        shift *= 2
    outs = []
    hb = h_prev
    for gi in range(g):
        hg = a3[gi] * hb + b3[gi]
        outs.append(hg)
        hb = jnp.broadcast_to(hg[SUBLANES - 1:SUBLANES, :], (SUBLANES, w))
    return jnp.concatenate(outs, axis=0), hb


def _group_row(x, g, r):
    return jnp.concatenate([jnp.broadcast_to(x[g0 + r:g0 + r + 1, :], (g, x.shape[1]))
                            for g0 in range(0, x.shape[0], g)], axis=0)


def _rows(start, size):
    return pl.ds(start if isinstance(start, int) else pl.multiple_of(start, size), size)


def _update_state(ref, bi, old, scale, mask_ref, row_op, col_op):
    for rs, cs in QUADS:
        sc = scale[rs, cs] if scale.shape[0] > 1 else scale[:, cs]
        ref[bi, rs, cs] = sc * old[rs, cs] + mask_ref[rs, cs] * _tdot(row_op[:, rs], col_op[:, cs])


def _head_transpose(s):
    return jnp.concatenate([s[h * HEAD_DIM:(h + 1) * HEAD_DIM, :].T for h in range(HEADS)], axis=0)


def _expand_state(s, bd):
    return jnp.concatenate([s] * HEADS, axis=1) * bd


def _compact_state(s):
    out = s[:, 0:HEAD_DIM]
    for h in range(1, HEADS):
        out = out + s[:, h * HEAD_DIM:(h + 1) * HEAD_DIM]
    return out


def _layer_kernel(cfg, *refs):
    layer, depth, bt, tile, c, n_lv, blk, lookahead, zero_init, layout = cfg
    n = bt * tile
    nb = n // blk
    it = iter(refs)
    x_ref = next(it)
    xn_ref = next(it) if lookahead else None
    cos_ref, sin_ref = next(it), next(it)
    conv_in, h_in, sret_in, shg_in = (None,) * 4 if zero_init else (next(it), next(it), next(it), next(it))
    (win_ref, wa_ref, wx_ref, wout_ref, convw_ref, convb_ref, rgba_ref, rgbx_ref, rglam_ref, gng_ref, hgbf_ref,
     hgng_ref, lbl_ref, lng_ref, lnb_ref) = [next(it) for _ in range(15)]
    packed = (next(it), next(it))
    tab = {name: packed[which].at[row0:row0 + rows, 0:lanes] for name, (which, row0, rows, lanes) in layout}
    dall_ref, gq_ref, gk_ref, gs_ref, bd_ref = tab["dall"], tab["gq"], tab["gk"], tab["gs"], tab["bd"]
    hm_ref, tri3_ref, lsgn_ref, lvl_ref, e64_ref = tab["hm"], tab["tri3"], tab["lsgn"], tab["lvl"], tab["e64"]
    xo_ref, conv_o, h_o, sret_o, shg_o = [next(it) for _ in range(5)]
    scr = list(it)
    sets = (scr[0:11], scr[11:22])
    conv_c, h_c, sret_c, shg_c = scr[22:26]
    wg_s = scr[26]
    seg_rows = blk if bt == 1 else tile
    dn_alpha = (2 * depth) ** 0.25
    all_layers = layer is None
    if all_layers:
        xs_s = scr[27]
        lyr, blk_id = pl.program_id(0), pl.program_id(1)
        first_tile = last_tile = True
        new_weights = blk_id == 0

        @pl.when(lyr == 0)
        def _take_input():
            xs_s[blk_id] = x_ref[...].reshape(n, D_MODEL)
    else:
        lyr = layer
        first_tile = pl.program_id(1) == 0
        last_tile = pl.program_id(1) == pl.num_programs(1) - 1
        new_weights = (pl.program_id(0) == 0) & (pl.program_id(1) == 0)

    @pl.when(first_tile)
    def _load_state():
        if zero_init:
            for ref in (conv_c, h_c, sret_c, shg_c):
                ref[...] = jnp.zeros(ref.shape, F32)
        else:
            conv_c[:, 0:SUBLANES - (CONV_W - 1), :] = jnp.zeros((bt, SUBLANES - (CONV_W - 1), RG_WIDTH), F32)
            conv_c[:, SUBLANES - (CONV_W - 1):SUBLANES, :] = conv_in[...]
            h_c[...] = jnp.broadcast_to(h_in[...], (bt, SUBLANES, RG_WIDTH))
            for bi in range(bt):
                sret_c[bi] = _expand_state(sret_in[bi], bd_ref[...])
                shg_c[bi] = _expand_state(_head_transpose(shg_in[bi]), bd_ref[...])

    @pl.when(new_weights)
    def _expand_gate_weights():
        for gi, ref in enumerate((wa_ref, wx_ref)):
            for hf in range(2):
                wg_s[gi, hf] = _expand_state(ref[hf], bd_ref[...]).astype(BF16)

    row = pl.ds(lyr, 1)
    conv_b = convb_ref[row, :]
    conv_w = [convw_ref[j:j + 1, :] for j in range(CONV_W)]
    rg_ba = rgba_ref[row, :]
    rg_bx = rgbx_ref[row, :]
    c_lam = RG_C * jax.nn.log_sigmoid(rglam_ref[row, :])
    gn_g = gng_ref[row, :]
    hg_bf = hgbf_ref[row, :]
    hg_ng = hgng_ref[row, :]
    logits = [lbl_ref[li:li + 1, :] for li in range(depth)]
    mx = functools.reduce(jnp.maximum, logits)
    ex = [jnp.exp(v - mx) for v in logits]
    den = functools.reduce(lambda s, v: s + v, ex)
    probs = [v / den for v in ex]
    lbs = [functools.reduce(lambda s, v: s + v, probs[:li + 1]) - probs[0] for li in range(depth)]
    if all_layers:
        lb = functools.reduce(lambda acc, li: jnp.where(lyr == li, lbs[li], acc), range(1, depth), lbs[0])
    else:
        lb = lbs[layer]
    one_m_lb = 1.0 - lb

    def project_steps(xblk, dst):
        xb_s = dst[10]

        def stage():
            xb_s[...] = xblk.astype(BF16)

        def step(lo):
            def run():
                val = _dot(xb_s[...], win_ref[:, lo:lo + PROJ_COLS])
                for gi in range(2 + N_NARROW):
                    g_lo, g_hi = max(SEG[gi], lo), min(SEG[gi + 1], lo + PROJ_COLS)
                    if g_lo >= g_hi:
                        continue
                    part = val[:, g_lo - lo:g_hi - lo]
                    if gi == 0:
                        for si in range(blk // seg_rows):
                            dst[0][si, SUBLANES:SUBLANES + seg_rows, g_lo:g_hi] = part[si * seg_rows:(si + 1) * seg_rows]
                    else:
                        dst[gi][:, g_lo - SEG[gi]:g_hi - SEG[gi]] = part
            return run

        return [stage] + [step(lo) for lo in range(0, D_IN, PROJ_COLS)]

    def mix(src, base, filler=()):
        filler = list(filler)
        n_ch = blk // c
        lockstep = n_ch if bt > 1 else min(n_ch, LOCKSTEP)
        n_grp = -(-n_ch // lockstep)
        fill_total = 2.0 + 4 * n_ch + 2 * n_grp
        issued = [0.0, 0]

        def fill(weight):
            issued[0] += weight
            due = math.ceil(len(filler) * min(issued[0] / fill_total, 1.0) - 1e-9)
            while issued[1] < due:
                filler[issued[1]]()
                issued[1] += 1

        rgx_s, rgg_s, q_s, k_s, v_s, rgate_s, hq_s, hf_s, hi_s, hgate_s = src[:10]
        rgx_s[:, 0:SUBLANES, :] = conv_c[...]
        us = []
        for si in range(blk // seg_rows):
            win = rgx_s[si]
            u_seg = conv_b + win[SUBLANES:, :] * conv_w[CONV_W - 1]
            for back in range(1, CONV_W):
                u_seg = u_seg + pltpu.roll(win, back, axis=0)[SUBLANES:, :] * conv_w[CONV_W - 1 - back]
            us.append(u_seg)
        conv_c[...] = rgx_s[:, seg_rows:seg_rows + SUBLANES, :]
        u = us[0] if len(us) == 1 else jnp.concatenate(us, axis=0)
        fill(1.0)
        ub = u.astype(BF16)
        half = RG_WIDTH // 2
        r_pre = jnp.concatenate([_dot(ub[:, :half], wg_s[0, 0]), _dot(ub[:, half:], wg_s[0, 1])], axis=1)
        i_pre = jnp.concatenate([_dot(ub[:, :half], wg_s[1, 0]), _dot(ub[:, half:], wg_s[1, 1])], axis=1)
        log_a = c_lam * jax.nn.sigmoid(r_pre + rg_ba)
        a = jnp.exp(log_a)
        b_in = jnp.sqrt(-jnp.tanh(log_a) * (a * a + 1.0)) * (jax.nn.sigmoid(i_pre + rg_bx) * u)
        fill(1.0)
        e64 = e64_ref[...]
        hm = hm_ref[...]
        lvl = lvl_ref[...]
        first_half = (lax.broadcasted_iota(jnp.int32, (c, HW), 1) % HEAD_DIM) < HEAD_DIM // 2

        per_seq = seg_rows // c
        st_all = [dict(rows=slice(ci * c, (ci + 1) * c), bi=ci // per_seq,
                       trow=_rows(base + (ci % per_seq) * c, c)) for ci in range(n_ch)]

        for g0 in range(0, n_ch, lockstep):
            st = st_all[g0:g0 + lockstep]
            for d in st:
                rows, trow = d["rows"], d["trow"]
                cos = jnp.concatenate([cos_ref[trow, :]] * (HW // LANES), axis=1)
                sin = jnp.concatenate([sin_ref[trow, :]] * (HW // LANES), axis=1)
                d["kr"] = _rotary(k_s[rows, :], cos, sin, first_half)
                d["qb"] = _rotary(q_s[rows, :], cos, sin, first_half).astype(BF16)
                d["scores"] = _dot_t(d["qb"], _stack_heads(d["kr"], hm))
                z = hf_s[rows, :] + hg_bf
                ez = jnp.exp(-jnp.abs(z))
                inv = 1.0 / (1.0 + ez)
                pos = z >= 0.0
                sig_p = jnp.where(pos, inv, ez * inv)
                sig_n = jnp.where(pos, ez * inv, inv)
                log_f = jnp.log(jnp.maximum(lb + one_m_lb * sig_p, F_EPS))
                d["kc"] = one_m_lb * sig_n
                f_hi = log_f.astype(BF16)
                res = log_f - f_hi.astype(F32)
                f_mid = res.astype(BF16)
                f_lo = (res - f_mid.astype(F32)).astype(BF16)
                d["cum"] = _dot(tri3_ref[...], jnp.concatenate([f_hi, f_mid, f_lo], axis=0))
                fill(1.0)

            for d in st:
                bi, kr, qb = d["bi"], d["kr"], d["qb"]
                v = v_s[d["rows"], :]
                p = (d["scores"] * dall_ref[...]).astype(BF16)
                s_ret = sret_c[bi]
                d["o_b"] = _dot(p, _stack_heads(v, hm)) + _dot(qb, s_ret.astype(BF16)) * gq_ref[...]
                _update_state(sret_c, bi, s_ret, gs_ref, bd_ref, (kr * gk_ref[...]).astype(BF16), v.astype(BF16))
                fill(1.0)
                cum, kc = d["cum"], d["kc"]
                qh = hq_s[d["rows"], :]
                e0 = jnp.exp(cum - _group_row(cum, LEVEL0_ROWS, LEVEL0_ROWS // 2 - 1))
                s0 = _dot_t((qh * e0).astype(BF16), _stack_heads(kc / e0, hm))
                pm = jnp.where(lvl == 0.0, s0, 0.0)
                for lv in range(1, n_lv + 1):
                    g = LEVEL0_ROWS * 2 ** lv
                    el = jnp.exp(lsgn_ref[(lv - 1) * c:lv * c, :] * (cum - _group_row(cum, g, g // 2 - 1)))
                    sl_ = _dot_t((qh * el).astype(BF16), _stack_heads(kc * el, hm))
                    pm = jnp.where(lvl == float(lv), sl_, pm)
                d["pm"] = pm
                fill(1.0)

            means = _seg_means([d["o_b"] for d in st], e64)
            for d, m in zip(st, means):
                d["dev"] = d["o_b"] - m
            fill(1.0)
            for d in st:
                bi, cum = d["bi"], d["cum"]
                vh = hi_s[d["rows"], :]
                qh = hq_s[d["rows"], :]
                s_hg = shg_c[bi]
                d["o_c"] = (_dot(d["pm"].astype(BF16), _stack_heads(vh, hm))
                            + _dot_t((qh * jnp.exp(cum)).astype(BF16), s_hg.astype(BF16)))
                k_st = (d["kc"] * jnp.exp(cum[c - 1:c, :] - cum)).astype(BF16)
                dec = jnp.exp(cum[c - 1:c, :])
                _update_state(shg_c, bi, s_hg, dec, bd_ref, vh.astype(BF16), k_st)
                fill(1.0)

            stats = _seg_means([d["dev"] * d["dev"] for d in st] + [d["o_c"] * d["o_c"] for d in st], e64)
            fill(1.0)
            for d, var_b, ms_c in zip(st, stats[:len(st)], stats[len(st):]):
                d["var_b"], d["ms_c"] = var_b, ms_c

        ys = []
        for d in st_all:
            rows, bi = d["rows"], d["bi"]
            h, h_last = _rg_scan(a[rows], b_in[rows], h_c[bi])
            h_c[bi] = h_last
            y_a = h * _silu(rgg_s[rows, :])
            y_b = d["dev"] * lax.rsqrt(d["var_b"] + LN_EPS) * gn_g * _silu(rgate_s[rows, :])
            y_c = d["o_c"] * lax.rsqrt(d["ms_c"] + LN_EPS) * hg_ng * _silu(hgate_s[rows, :])
            ys.append(jnp.concatenate([y_a.astype(BF16), y_b.astype(BF16), y_c.astype(BF16)], axis=1))
        y = ys[0] if n_ch == 1 else jnp.concatenate(ys, axis=0)

        assert abs(issued[0] - fill_total) < 1e-6 and issued[1] == len(filler)

        if bt == 1:
            xrows = _rows(base, blk)
            x_in = x_ref[0, xrows, :]
        elif all_layers:
            x_in = xs_s[blk_id]
        else:
            x_in = x_ref[...].reshape(n, D_MODEL)
        xn = dn_alpha * x_in + _dot(y, wout_ref[...])
        mu = jnp.mean(xn, axis=-1, keepdims=True)
        dv = xn - mu
        var = jnp.mean(dv * dv, axis=-1, keepdims=True)
        out = dv * lax.rsqrt(var + LN_EPS) * lng_ref[row, :] + lnb_ref[row, :]
        if bt == 1:
            xo_ref[0, xrows, :] = out
        else:
            xo_ref[...] = out.reshape(bt, tile, D_MODEL)
            if all_layers:
                xs_s[blk_id] = out

    def project(xblk, dst):
        for run in project_steps(xblk, dst):
            run()

    if not lookahead:
        project(xs_s[blk_id] if all_layers else x_ref[...].reshape(n, D_MODEL), sets[0])
        mix(sets[0], 0)
    else:
        @pl.when((pl.program_id(0) == 0) & (pl.program_id(1) == 0))
        def _first_block():
            project(x_ref[0, 0:blk, :], sets[0])

        def pair_body(k, carry):
            base0 = pl.multiple_of(2 * k * blk, 2 * blk)
            mix(sets[0], base0, project_steps(x_ref[0, _rows(base0 + blk, blk), :], sets[1]))
            in_tile = 2 * k + 2 < nb
            nxt = jnp.minimum(2 * k + 2, nb - 1) * blk
            ahead = jnp.where(in_tile, x_ref[0, _rows(nxt, blk), :], xn_ref[0])
            mix(sets[1], base0 + blk, project_steps(ahead, sets[0]))
            return carry

        lax.fori_loop(0, nb // 2, pair_body, 0)

    @pl.when(last_tile)
    def _store_state():
        conv_o[...] = conv_c[:, SUBLANES - (CONV_W - 1):SUBLANES, :]
        h_o[...] = h_c[:, 0:1, :]
        for bi in range(bt):
            sret_o[bi] = _compact_state(sret_c[bi])
            shg_o[bi] = _head_transpose(_compact_state(shg_c[bi]))


def _tiling(batch, seq):
    chunk = min(MAX_CHUNK, seq)
    tile = min(seq, MAX_TILE)
    bt = 1 if tile >= BLOCK_ROWS else min(batch, BLOCK_ROWS // tile)
    blk = min(BLOCK_ROWS, bt * tile)
    lookahead = bt == 1 and (tile // blk) >= 2
    assert seq % tile == 0 and batch % bt == 0 and tile % chunk == 0 and chunk % 16 == 0 and blk % chunk == 0
    assert lookahead or bt * tile == blk
    assert not lookahead or (tile // blk) % 2 == 0
    return bt, tile, chunk, blk, lookahead


def _scratch_shapes(bt, tile, blk):
    seg_rows = blk if bt == 1 else tile
    one_set = ([pltpu.VMEM((blk // seg_rows, SUBLANES + seg_rows, RG_WIDTH), F32), pltpu.VMEM((blk, RG_WIDTH), F32)]
               + [pltpu.VMEM((blk, HW), F32)] * N_NARROW + [pltpu.VMEM((blk, D_MODEL), BF16)])
    carries = [pltpu.VMEM((bt, SUBLANES, RG_WIDTH), F32)] * 2 + [pltpu.VMEM((bt, HW, HW), F32)] * 2
    gate_w = [pltpu.VMEM((2, 2, HW, HW), BF16)]
    return one_set * 2 + carries + gate_w


def _all_layers_call(kern, depth, x, rope, states, params, tabs, bt, tile, blk, grid):
    batch = x.shape[0]
    state_tails = ((CONV_W - 1, RG_WIDTH), (1, RG_WIDTH), (HW, HEAD_DIM), (HW, HEAD_DIM))

    def const(arr):
        zeros = (0,) * arr.ndim
        return pl.BlockSpec(arr.shape, lambda l, b: zeros)

    def per_layer(arr):
        tail = arr.shape[1:]
        return pl.BlockSpec((None,) + tail, lambda l, b: (l,) + (0,) * len(tail))

    def per_lb(tail):
        return pl.BlockSpec((None, bt) + tail, lambda l, b: (l, b) + (0,) * len(tail))

    x_spec = pl.BlockSpec((bt, tile, D_MODEL), lambda l, b: (b, 0, 0))
    rope_spec = pl.BlockSpec((tile, LANES), lambda l, b: (0, 0))
    args = [x] + list(rope) + list(states) + list(params) + list(tabs)
    specs = ([x_spec] + [rope_spec] * 2 + [per_lb(t) for t in state_tails]
             + [per_layer(p) if p.ndim > 2 else const(p) for p in params] + [const(t) for t in tabs])
    xo_spec = pl.BlockSpec((bt, tile, D_MODEL), lambda l, b: (jnp.where(l == depth - 1, b, 0), 0, 0))
    out_shape = ([jax.ShapeDtypeStruct(x.shape, F32)]
                 + [jax.ShapeDtypeStruct((depth, batch) + t, F32) for t in state_tails])
    out_specs = [xo_spec] + [per_lb(t) for t in state_tails]
    stream = [pltpu.VMEM((batch // bt, bt * tile, D_MODEL), F32)]
    return pl.pallas_call(
        kern, grid=grid, in_specs=specs, out_specs=out_specs, out_shape=out_shape,
        scratch_shapes=_scratch_shapes(bt, tile, blk) + stream,
        compiler_params=pltpu.CompilerParams(dimension_semantics=("arbitrary", "arbitrary"),
                                             vmem_limit_bytes=VMEM_LIMIT_BYTES),
        name=f"layers_t{tile}",
    )(*args)


def _layer_call(layer, depth, x, rope, states, params, tabs, layout, n_lv, tiling):
    bt, tile, chunk, blk, lookahead = tiling
    batch, seq, _ = x.shape
    n_tiles = seq // tile
    all_layers = layer is None
    assert not all_layers or (n_tiles == 1 and not lookahead and states is not None)
    grid = (depth, batch // bt) if all_layers else (batch // bt, n_tiles)
    n_steps = grid[0] * grid[1]
    zero_init = states is None
    kern = functools.partial(_layer_kernel, (layer, depth, bt, tile, chunk, n_lv, blk, lookahead, zero_init, layout))
    if all_layers:
        return _all_layers_call(kern, depth, x, rope, states, params, tabs, bt, tile, blk, grid)

    def layer_const(arr):
        tail = arr.shape[1:]
        return pl.BlockSpec((None,) + tail, lambda b, i: (layer,) + (0,) * len(tail))

    def const(arr):
        zeros = (0,) * arr.ndim
        return pl.BlockSpec(arr.shape, lambda b, i: zeros)

    def per_b(tail):
        return pl.BlockSpec((bt,) + tail, lambda b, i: (b,) + (0,) * len(tail))

    def per_lb(tail):
        return pl.BlockSpec((None, bt) + tail, lambda b, i: (layer, b) + (0,) * len(tail))

    def next_block(b, i):
        f = jnp.minimum(b * n_tiles + i + 1, n_steps - 1)
        return (f // n_tiles, (f % n_tiles) * (tile // blk), 0)

    state_tails = ((CONV_W - 1, RG_WIDTH), (1, RG_WIDTH), (HW, HEAD_DIM), (HW, HEAD_DIM))
    x_spec = pl.BlockSpec((bt, tile, D_MODEL), lambda b, i: (b, i, 0))
    rope_spec = pl.BlockSpec((tile, LANES), lambda b, i: (i, 0))
    args, specs = [x], [x_spec]
    if lookahead:
        args.append(x)
        specs.append(pl.BlockSpec((1, blk, D_MODEL), next_block))
    args += list(rope)
    specs += [rope_spec] * 2
    if not zero_init:
        args += list(states)
        specs += [per_lb(t) for t in state_tails]
    args += list(params)
    specs += [layer_const(p) if p.ndim > 2 else const(p) for p in params]
    args += list(tabs)
    specs += [const(t) for t in tabs]

    out_shape = [jax.ShapeDtypeStruct(x.shape, F32)] + [jax.ShapeDtypeStruct((batch,) + t, F32) for t in state_tails]
    out_specs = [x_spec] + [per_b(t) for t in state_tails]
    return pl.pallas_call(
        kern, grid=grid, in_specs=specs, out_specs=out_specs, out_shape=out_shape,
        scratch_shapes=_scratch_shapes(bt, tile, blk),
        compiler_params=pltpu.CompilerParams(dimension_semantics=("arbitrary", "arbitrary"),
                                             vmem_limit_bytes=VMEM_LIMIT_BYTES),
        name=f"layer{layer}_t{tile}",
    )(*args)


def kernel(x_prompt, x_sample, cache_conv, state_rglru, state_ret, state_hgrn, w_in, conv_w, conv_b, rg_wa, rg_ba,
           rg_wx, rg_bx, rg_lambda, ret_gn_g, hg_bf, hg_lb_logits, hg_norm_g, w_out, ln_g, ln_b):
    depth = w_in.shape[0]
    bp, lp, _ = x_prompt.shape
    bs, ls, _ = x_sample.shape

    streams = []
    for batch, seq, pos0 in ((bp, lp, 0), (bs, ls, PAST_LEN)):
        tiling = _tiling(batch, seq)
        tabs_np, n_lv = _mixer_tables(tiling[2])
        (tab_f32, tab_bf16), layout = _pack_tables(tabs_np)
        tabs = (jnp.asarray(tab_f32, F32), jnp.asarray(tab_bf16, BF16))
        rope = tuple(jnp.asarray(t, F32) for t in _rope_tables(pos0 + np.arange(seq)))
        streams.append((tiling, n_lv, tabs, layout, rope))

    halves = lambda w: w.reshape(depth, 2, HW, RG_BLOCK)
    params = (w_in.astype(BF16), halves(rg_wa), halves(rg_wx), w_out.astype(BF16), conv_w, conv_b, rg_ba, rg_bx,
              rg_lambda, ret_gn_g, hg_bf, hg_norm_g, hg_lb_logits, ln_g, ln_b)
    sample_states = (cache_conv, state_rglru[:, :, None, :], state_ret.reshape(depth, bs, HW, HEAD_DIM),
                     state_hgrn.reshape(depth, bs, HW, HEAD_DIM))

    xp = x_prompt
    outs_p = []
    tiling, n_lv, tabs, layout, rope = streams[0]
    for l in range(depth):
        res = _layer_call(l, depth, xp, rope, None, params, tabs, layout, n_lv, tiling)
        xp = res[0]
        outs_p.append(res[1:])
    conv_p = jnp.stack([o[0] for o in outs_p])
    h_p = jnp.stack([o[1][:, 0, :] for o in outs_p])
    ret_p = jnp.stack([o[2] for o in outs_p]).reshape(depth, bp, HEADS, HEAD_DIM, HEAD_DIM)
    hg_p = jnp.stack([o[3] for o in outs_p]).reshape(depth, bp, HEADS, HEAD_DIM, HEAD_DIM)

    tiling, n_lv, tabs, layout, rope = streams[1]
    xs, conv_s, h_s, ret_s, hg_s = _layer_call(None, depth, x_sample, rope, sample_states, params, tabs, layout, n_lv,
                                               tiling)
    shape_s = (depth, bs, HEADS, HEAD_DIM, HEAD_DIM)
    return (xp, xs, conv_p, h_p, ret_p, hg_p,
            conv_s, h_s.reshape(depth, bs, RG_WIDTH), ret_s.reshape(shape_s), hg_s.reshape(shape_s))
```

```python
import functools
import math

import jax
import jax.numpy as jnp
import numpy as np
from jax import lax
from jax.experimental import pallas as pl
from jax.experimental.pallas import tpu as pltpu

F32 = jnp.float32
BF16 = jnp.bfloat16

D_MODEL = 1024
RG_WIDTH = 512
RG_BLOCKS = 8
RG_BLOCK = RG_WIDTH // RG_BLOCKS
CONV_W = 4
RG_C = 8.0
HEADS = 4
HEAD_DIM = 64
HW = HEADS * HEAD_DIM
ROPE_BASE = 10000.0
LN_EPS = 1e-5
F_EPS = 1e-6
PAST_LEN = 1024
SPLITS = (RG_WIDTH, RG_WIDTH, HW, HW, HW, HW, HW, HW, HW, HW)
D_IN = sum(SPLITS)
SEG = tuple(int(v) for v in np.cumsum((0,) + SPLITS))
N_NARROW = 8

SUBLANES = 8
LANES = 128
VMEM_LIMIT_BYTES = 56 * 1024 * 1024

MAX_CHUNK = 64
BLOCK_ROWS = 256
PROJ_COLS = 256
MAX_TILE = 1024
LOCKSTEP = 1
LEVEL0_ROWS = SUBLANES

QUADS = ((slice(0, 128), slice(0, 128)), (slice(128, 256), slice(128, 256)))


def _mixer_tables(chunk):
    c = chunk
    t = np.arange(c)
    hid = np.repeat(np.arange(HEADS), HEAD_DIM)
    row_h = np.repeat(np.arange(HEADS), c)
    scale = HEAD_DIM ** -0.5

    log_g = np.log1p(-np.exp2(-5.0 - np.arange(HEADS)))
    rel = t[:, None] - t[None, :]
    dmat = np.where(rel >= 0, np.exp(np.maximum(rel, 0)[None] * log_g[:, None, None]), 0.0)
    bd = (hid[:, None] == hid[None, :]).astype(np.float64)
    tabs = {
        "dall": scale * dmat.transpose(1, 0, 2).reshape(c, HEADS * c),
        "gq": scale * np.repeat(np.exp((t[:, None] + 1.0) * log_g[None, :]), HEAD_DIM, 1),
        "gk": np.exp((c - 1.0 - t)[:, None] * log_g[None, :])[:, hid],
        "gs": np.exp(c * log_g)[hid][:, None] * np.ones((1, HW)),
        "bd": bd,
        "hm": (row_h[:, None] == hid[None, :]).astype(np.float64),
        "e64": bd / HEAD_DIM,
    }

    n_lv = int(round(math.log2(c // LEVEL0_ROWS)))
    tri = (t[None, :] <= t[:, None]).astype(np.float64)
    signs = []
    lvl = np.full((c, c), -1.0)
    causal = t[None, :] <= t[:, None]
    same0 = (t[:, None] // LEVEL0_ROWS) == (t[None, :] // LEVEL0_ROWS)
    lvl[causal & same0] = 0.0
    assigned = same0.copy()
    for lv in range(1, n_lv + 1):
        g = LEVEL0_ROWS * 2 ** lv
        upper = (t % g) >= g // 2
        signs.append(np.where(upper, 1.0, -1.0)[:, None] * np.ones((1, HW)))
        same = (t[:, None] // g) == (t[None, :] // g)
        lvl[causal & same & ~assigned] = float(lv)
        assigned |= same
    tabs["tri3"] = np.concatenate([tri, tri, tri], axis=1)
    tabs["lsgn"] = np.concatenate(signs, axis=0)
    tabs["lvl"] = np.tile(lvl, (1, HEADS))
    return tabs, n_lv


_F32_TABLES = ("dall", "gq", "gk", "lvl", "lsgn", "gs", "bd")
_BF16_TABLES = ("hm", "e64", "tri3")


def _pack_tables(tabs):
    packed, layout = [], {}
    for which, names in enumerate((_F32_TABLES, _BF16_TABLES)):
        row0, parts = 0, []
        for name in names:
            t = tabs[name]
            layout[name] = (which, row0, t.shape[0], t.shape[1])
            parts.append(np.pad(t, ((0, 0), (0, HW - t.shape[1]))))
            row0 += t.shape[0]
        packed.append(np.concatenate(parts, axis=0))
    return packed, tuple(sorted(layout.items()))


def _rope_tables(pos):
    half = HEAD_DIM // 2
    inv = ROPE_BASE ** (-np.arange(half, dtype=np.float64) / half)
    ang = pos.astype(np.float64)[:, None] * inv[None, :]
    cos = np.tile(np.cos(ang), (1, 2 * LANES // HEAD_DIM))
    sin = np.tile(np.concatenate([-np.sin(ang), np.sin(ang)], axis=1), (1, LANES // HEAD_DIM))
    return cos, sin


def _silu(x):
    return x * jax.nn.sigmoid(x)


def _dot(a, b):
    return jnp.dot(a, b, preferred_element_type=F32)


def _dot_t(a, b):
    return lax.dot_general(a, b, (((1,), (1,)), ((), ())), preferred_element_type=F32)


def _tdot(a, b):
    return lax.dot_general(a, b, (((0,), (0,)), ((), ())), preferred_element_type=F32)


def _seg_means(xs, e64):
    rows = xs[0].shape[0]
    x = xs[0] if len(xs) == 1 else jnp.concatenate(xs, axis=0)
    m = _dot(x.astype(BF16), e64)
    return [m[i * rows:(i + 1) * rows] for i in range(len(xs))]


def _stack_heads(x, mask):
    xb = x.astype(BF16)
    return jnp.concatenate([xb] * HEADS, axis=0) * mask


def _rotary(x, cos, sin, first_half):
    partner = jnp.where(first_half, pltpu.roll(x, HW - HEAD_DIM // 2, axis=1), pltpu.roll(x, HEAD_DIM // 2, axis=1))
    return x * cos + partner * sin


def _rg_scan(a, b, h_prev):
    c, w = a.shape
    g = c // SUBLANES
    a3 = a.reshape(g, SUBLANES, w)
    b3 = b.reshape(g, SUBLANES, w)
    sub = lax.broadcasted_iota(jnp.int32, a3.shape, 1)
    shift = 1
    while shift < SUBLANES:
        keep = sub >= shift
        a_sh = pltpu.roll(a3, shift, axis=1)
        b_sh = pltpu.roll(b3, shift, axis=1)
        b3 = jnp.where(keep, a3 * b_sh + b3, b3)
        a3 = jnp.where(keep, a3 * a_sh, a3)
        shift *= 2
    outs = []
    hb = h_prev
    for gi in range(g):
        hg = a3[gi] * hb + b3[gi]
        outs.append(hg)
        hb = jnp.broadcast_to(hg[SUBLANES - 1:SUBLANES, :], (SUBLANES, w))
    return jnp.concatenate(outs, axis=0), hb


def _group_row(x, g, r):
    return jnp.concatenate([jnp.broadcast_to(x[g0 + r:g0 + r + 1, :], (g, x.shape[1]))
                            for g0 in range(0, x.shape[0], g)], axis=0)


def _rows(start, size):
    return pl.ds(start if isinstance(start, int) else pl.multiple_of(start, size), size)


def _update_state(ref, bi, old, scale, mask_ref, row_op, col_op):
    for rs, cs in QUADS:
        sc = scale[rs, cs] if scale.shape[0] > 1 else scale[:, cs]
        ref[bi, rs, cs] = sc * old[rs, cs] + mask_ref[rs, cs] * _tdot(row_op[:, rs], col_op[:, cs])


def _head_transpose(s):
    return jnp.concatenate([s[h * HEAD_DIM:(h + 1) * HEAD_DIM, :].T for h in range(HEADS)], axis=0)


def _expand_state(s, bd):
    return jnp.concatenate([s] * HEADS, axis=1) * bd


def _compact_state(s):
    out = s[:, 0:HEAD_DIM]
    for h in range(1, HEADS):
        out = out + s[:, h * HEAD_DIM:(h + 1) * HEAD_DIM]
    return out


def _layer_kernel(cfg, *refs):
    layer, depth, bt, tile, c, n_lv, blk, lookahead, zero_init, layout = cfg
    n = bt * tile
    nb = n // blk
    it = iter(refs)
    x_ref = next(it)
    xn_ref = next(it) if lookahead else None
    cos_ref, sin_ref = next(it), next(it)
    conv_in, h_in, sret_in, shg_in = (None,) * 4 if zero_init else (next(it), next(it), next(it), next(it))
    (win_ref, wa_ref, wx_ref, wout_ref, convw_ref, convb_ref, rgba_ref, rgbx_ref, rglam_ref, gng_ref, hgbf_ref,
     hgng_ref, lbl_ref, lng_ref, lnb_ref) = [next(it) for _ in range(15)]
    packed = (next(it), next(it))
    tab = {name: packed[which].at[row0:row0 + rows, 0:lanes] for name, (which, row0, rows, lanes) in layout}
    dall_ref, gq_ref, gk_ref, gs_ref, bd_ref = tab["dall"], tab["gq"], tab["gk"], tab["gs"], tab["bd"]
    hm_ref, tri3_ref, lsgn_ref, lvl_ref, e64_ref = tab["hm"], tab["tri3"], tab["lsgn"], tab["lvl"], tab["e64"]
    xo_ref, conv_o, h_o, sret_o, shg_o = [next(it) for _ in range(5)]
    scr = list(it)
    sets = (scr[0:11], scr[11:22])
    conv_c, h_c, sret_c, shg_c = scr[22:26]
    wg_s = scr[26]
    seg_rows = blk if bt == 1 else tile
    dn_alpha = (2 * depth) ** 0.25
    all_layers = layer is None
    if all_layers:
        xs_s = scr[27]
        lyr, blk_id = pl.program_id(0), pl.program_id(1)
        first_tile = last_tile = True
        new_weights = blk_id == 0

        @pl.when(lyr == 0)
        def _take_input():
            xs_s[blk_id] = x_ref[...].reshape(n, D_MODEL)
    else:
        lyr = layer
        first_tile = pl.program_id(1) == 0
        last_tile = pl.program_id(1) == pl.num_programs(1) - 1
        new_weights = (pl.program_id(0) == 0) & (pl.program_id(1) == 0)

    @pl.when(first_tile)
    def _load_state():
        if zero_init:
            for ref in (conv_c, h_c, sret_c, shg_c):
                ref[...] = jnp.zeros(ref.shape, F32)
        else:
            conv_c[:, 0:SUBLANES - (CONV_W - 1), :] = jnp.zeros((bt, SUBLANES - (CONV_W - 1), RG_WIDTH), F32)
            conv_c[:, SUBLANES - (CONV_W - 1):SUBLANES, :] = conv_in[...]
            h_c[...] = jnp.broadcast_to(h_in[...], (bt, SUBLANES, RG_WIDTH))
            for bi in range(bt):
                sret_c[bi] = _expand_state(sret_in[bi], bd_ref[...])
                shg_c[bi] = _expand_state(_head_transpose(shg_in[bi]), bd_ref[...])

    @pl.when(new_weights)
    def _expand_gate_weights():
        for gi, ref in enumerate((wa_ref, wx_ref)):
            for hf in range(2):
                wg_s[gi, hf] = _expand_state(ref[hf], bd_ref[...]).astype(BF16)

    row = pl.ds(lyr, 1)
    conv_b = convb_ref[row, :]
    conv_w = [convw_ref[j:j + 1, :] for j in range(CONV_W)]
    rg_ba = rgba_ref[row, :]
    rg_bx = rgbx_ref[row, :]
    c_lam = RG_C * jax.nn.log_sigmoid(rglam_ref[row, :])
    gn_g = gng_ref[row, :]
    hg_bf = hgbf_ref[row, :]
    hg_ng = hgng_ref[row, :]
    logits = [lbl_ref[li:li + 1, :] for li in range(depth)]
    mx = functools.reduce(jnp.maximum, logits)
    ex = [jnp.exp(v - mx) for v in logits]
    den = functools.reduce(lambda s, v: s + v, ex)
    probs = [v / den for v in ex]
    lbs = [functools.reduce(lambda s, v: s + v, probs[:li + 1]) - probs[0] for li in range(depth)]
    if all_layers:
        lb = functools.reduce(lambda acc, li: jnp.where(lyr == li, lbs[li], acc), range(1, depth), lbs[0])
    else:
        lb = lbs[layer]
    one_m_lb = 1.0 - lb

    def project_steps(xblk, dst):
        xb_s = dst[10]

        def stage():
            xb_s[...] = xblk.astype(BF16)

        def step(lo):
            def run():
                val = _dot(xb_s[...], win_ref[:, lo:lo + PROJ_COLS])
                for gi in range(2 + N_NARROW):
                    g_lo, g_hi = max(SEG[gi], lo), min(SEG[gi + 1], lo + PROJ_COLS)
                    if g_lo >= g_hi:
                        continue
                    part = val[:, g_lo - lo:g_hi - lo]
                    if gi == 0:
                        for si in range(blk // seg_rows):
                            dst[0][si, SUBLANES:SUBLANES + seg_rows, g_lo:g_hi] = part[si * seg_rows:(si + 1) * seg_rows]
                    else:
                        dst[gi][:, g_lo - SEG[gi]:g_hi - SEG[gi]] = part
            return run

        return [stage] + [step(lo) for lo in range(0, D_IN, PROJ_COLS)]

    def mix(src, base, filler=()):
        filler = list(filler)
        n_ch = blk // c
        lockstep = n_ch if bt > 1 else min(n_ch, LOCKSTEP)
        n_grp = -(-n_ch // lockstep)
        fill_total = 2.0 + 4 * n_ch + 2 * n_grp
        issued = [0.0, 0]

        def fill(weight):
            issued[0] += weight
            due = math.ceil(len(filler) * min(issued[0] / fill_total, 1.0) - 1e-9)
            while issued[1] < due:
                filler[issued[1]]()
                issued[1] += 1

        rgx_s, rgg_s, q_s, k_s, v_s, rgate_s, hq_s, hf_s, hi_s, hgate_s = src[:10]
        rgx_s[:, 0:SUBLANES, :] = conv_c[...]
        half = RG_WIDTH // 2
        y_as = [[None, None] for _ in range(blk // c)]
        for hf in range(2):
            cs = slice(hf * half, (hf + 1) * half)
            us = []
            for si in range(blk // seg_rows):
                win = rgx_s[si, :, cs]
                u_seg = conv_b[:, cs] + win[SUBLANES:, :] * conv_w[CONV_W - 1][:, cs]
                for back in range(1, CONV_W):
                    u_seg = u_seg + pltpu.roll(win, back, axis=0)[SUBLANES:, :] * conv_w[CONV_W - 1 - back][:, cs]
                us.append(u_seg)
            u = us[0] if len(us) == 1 else jnp.concatenate(us, axis=0)
            ub = u.astype(BF16)
            log_a = c_lam[:, cs] * jax.nn.sigmoid(_dot(ub, wg_s[0, hf]) + rg_ba[:, cs])
            a = jnp.exp(log_a)
            b_in = jnp.sqrt(-jnp.tanh(log_a) * (a * a + 1.0)) * (jax.nn.sigmoid(_dot(ub, wg_s[1, hf]) + rg_bx[:, cs]) * u)
            for ci in range(blk // c):
                rows, bi = slice(ci * c, (ci + 1) * c), ci // (seg_rows // c)
                h, h_last = _rg_scan(a[rows], b_in[rows], h_c[bi, :, cs])
                h_c[bi, :, cs] = h_last
                y_as[ci][hf] = (h * _silu(rgg_s[rows, cs])).astype(BF16)
            fill(1.0)
        conv_c[...] = rgx_s[:, seg_rows:seg_rows + SUBLANES, :]
        e64 = e64_ref[...]
        hm = hm_ref[...]
        lvl = lvl_ref[...]
        first_half = (lax.broadcasted_iota(jnp.int32, (c, HW), 1) % HEAD_DIM) < HEAD_DIM // 2

        per_seq = seg_rows // c
        st_all = [dict(rows=slice(ci * c, (ci + 1) * c), bi=ci // per_seq,
                       trow=_rows(base + (ci % per_seq) * c, c)) for ci in range(n_ch)]

        for g0 in range(0, n_ch, lockstep):
            st = st_all[g0:g0 + lockstep]
            for d in st:
                rows, trow = d["rows"], d["trow"]
                cos = jnp.concatenate([cos_ref[trow, :]] * (HW // LANES), axis=1)
                sin = jnp.concatenate([sin_ref[trow, :]] * (HW // LANES), axis=1)
                d["kr"] = _rotary(k_s[rows, :], cos, sin, first_half)
                d["qb"] = _rotary(q_s[rows, :], cos, sin, first_half).astype(BF16)
                d["scores"] = _dot_t(d["qb"], _stack_heads(d["kr"], hm))
                z = hf_s[rows, :] + hg_bf
                ez = jnp.exp(-jnp.abs(z))
                inv = 1.0 / (1.0 + ez)
                pos = z >= 0.0
                sig_p = jnp.where(pos, inv, ez * inv)
                sig_n = jnp.where(pos, ez * inv, inv)
                log_f = jnp.log(jnp.maximum(lb + one_m_lb * sig_p, F_EPS))
                d["kc"] = one_m_lb * sig_n
                f_hi = log_f.astype(BF16)
                res = log_f - f_hi.astype(F32)
                f_mid = res.astype(BF16)
                f_lo = (res - f_mid.astype(F32)).astype(BF16)
                d["cum"] = _dot(tri3_ref[...], jnp.concatenate([f_hi, f_mid, f_lo], axis=0))
                fill(1.0)

            for d in st:
                bi, kr, qb = d["bi"], d["kr"], d["qb"]
                v = v_s[d["rows"], :]
                p = (d["scores"] * dall_ref[...]).astype(BF16)
                s_ret = sret_c[bi]
                d["o_b"] = _dot(p, _stack_heads(v, hm)) + _dot(qb, s_ret.astype(BF16)) * gq_ref[...]
                _update_state(sret_c, bi, s_ret, gs_ref, bd_ref, (kr * gk_ref[...]).astype(BF16), v.astype(BF16))
                fill(1.0)
                cum, kc = d["cum"], d["kc"]
                qh = hq_s[d["rows"], :]
                e0 = jnp.exp(cum - _group_row(cum, LEVEL0_ROWS, LEVEL0_ROWS // 2 - 1))
                s0 = _dot_t((qh * e0).astype(BF16), _stack_heads(kc / e0, hm))
                pm = jnp.where(lvl == 0.0, s0, 0.0)
                for lv in range(1, n_lv + 1):
                    g = LEVEL0_ROWS * 2 ** lv
                    el = jnp.exp(lsgn_ref[(lv - 1) * c:lv * c, :] * (cum - _group_row(cum, g, g // 2 - 1)))
                    sl_ = _dot_t((qh * el).astype(BF16), _stack_heads(kc * el, hm))
                    pm = jnp.where(lvl == float(lv), sl_, pm)
                d["pm"] = pm
                fill(1.0)

            means = _seg_means([d["o_b"] for d in st], e64)
            for d, m in zip(st, means):
                d["dev"] = d["o_b"] - m
            fill(1.0)
            for d in st:
                bi, cum = d["bi"], d["cum"]
                vh = hi_s[d["rows"], :]
                qh = hq_s[d["rows"], :]
                s_hg = shg_c[bi]
                d["o_c"] = (_dot(d["pm"].astype(BF16), _stack_heads(vh, hm))
                            + _dot_t((qh * jnp.exp(cum)).astype(BF16), s_hg.astype(BF16)))
                k_st = (d["kc"] * jnp.exp(cum[c - 1:c, :] - cum)).astype(BF16)
                dec = jnp.exp(cum[c - 1:c, :])
                _update_state(shg_c, bi, s_hg, dec, bd_ref, vh.astype(BF16), k_st)
                fill(1.0)

            stats = _seg_means([d["dev"] * d["dev"] for d in st] + [d["o_c"] * d["o_c"] for d in st], e64)
            fill(1.0)
            for d, var_b, ms_c in zip(st, stats[:len(st)], stats[len(st):]):
                d["var_b"], d["ms_c"] = var_b, ms_c

        ys = []
        for ci, d in enumerate(st_all):
            rows = d["rows"]
            y_b = d["dev"] * lax.rsqrt(d["var_b"] + LN_EPS) * gn_g * _silu(rgate_s[rows, :])
            y_c = d["o_c"] * lax.rsqrt(d["ms_c"] + LN_EPS) * hg_ng * _silu(hgate_s[rows, :])
            ys.append(jnp.concatenate(y_as[ci] + [y_b.astype(BF16), y_c.astype(BF16)], axis=1))
        y = ys[0] if n_ch == 1 else jnp.concatenate(ys, axis=0)

        assert abs(issued[0] - fill_total) < 1e-6 and issued[1] == len(filler)

        if bt == 1:
            xrows = _rows(base, blk)
            x_in = x_ref[0, xrows, :]
        elif all_layers:
            x_in = xs_s[blk_id]
        else:
            x_in = x_ref[...].reshape(n, D_MODEL)
        xn = dn_alpha * x_in + _dot(y, wout_ref[...])
        mu = jnp.mean(xn, axis=-1, keepdims=True)
        dv = xn - mu
        var = jnp.mean(dv * dv, axis=-1, keepdims=True)
        out = dv * lax.rsqrt(var + LN_EPS) * lng_ref[row, :] + lnb_ref[row, :]
        if bt == 1:
            xo_ref[0, xrows, :] = out
        else:
            xo_ref[...] = out.reshape(bt, tile, D_MODEL)
            if all_layers:
                xs_s[blk_id] = out

    def project(xblk, dst):
        for run in project_steps(xblk, dst):
            run()

    if not lookahead:
        project(xs_s[blk_id] if all_layers else x_ref[...].reshape(n, D_MODEL), sets[0])
        mix(sets[0], 0)
    else:
        @pl.when((pl.program_id(0) == 0) & (pl.program_id(1) == 0))
        def _first_block():
            project(x_ref[0, 0:blk, :], sets[0])

        def pair_body(k, carry):
            base0 = pl.multiple_of(2 * k * blk, 2 * blk)
            mix(sets[0], base0, project_steps(x_ref[0, _rows(base0 + blk, blk), :], sets[1]))
            in_tile = 2 * k + 2 < nb
            nxt = jnp.minimum(2 * k + 2, nb - 1) * blk
            ahead = jnp.where(in_tile, x_ref[0, _rows(nxt, blk), :], xn_ref[0])
            mix(sets[1], base0 + blk, project_steps(ahead, sets[0]))
            return carry

        lax.fori_loop(0, nb // 2, pair_body, 0)

    @pl.when(last_tile)
    def _store_state():
        conv_o[...] = conv_c[:, SUBLANES - (CONV_W - 1):SUBLANES, :]
        h_o[...] = h_c[:, 0:1, :]
        for bi in range(bt):
            sret_o[bi] = _compact_state(sret_c[bi])
            shg_o[bi] = _head_transpose(_compact_state(shg_c[bi]))


def _tiling(batch, seq):
    chunk = min(MAX_CHUNK, seq)
    tile = min(seq, MAX_TILE)
    bt = 1 if tile >= BLOCK_ROWS else min(batch, BLOCK_ROWS // tile)
    blk = min(BLOCK_ROWS, bt * tile)
    lookahead = bt == 1 and (tile // blk) >= 2
    assert seq % tile == 0 and batch % bt == 0 and tile % chunk == 0 and chunk % 16 == 0 and blk % chunk == 0
    assert lookahead or bt * tile == blk
    assert not lookahead or (tile // blk) % 2 == 0
    return bt, tile, chunk, blk, lookahead


def _scratch_shapes(bt, tile, blk):
    seg_rows = blk if bt == 1 else tile
    one_set = ([pltpu.VMEM((blk // seg_rows, SUBLANES + seg_rows, RG_WIDTH), F32), pltpu.VMEM((blk, RG_WIDTH), F32)]
               + [pltpu.VMEM((blk, HW), F32)] * N_NARROW + [pltpu.VMEM((blk, D_MODEL), BF16)])
    carries = [pltpu.VMEM((bt, SUBLANES, RG_WIDTH), F32)] * 2 + [pltpu.VMEM((bt, HW, HW), F32)] * 2
    gate_w = [pltpu.VMEM((2, 2, HW, HW), BF16)]
    return one_set * 2 + carries + gate_w


def _all_layers_call(kern, depth, x, rope, states, params, tabs, bt, tile, blk, grid):
    batch = x.shape[0]
    state_tails = ((CONV_W - 1, RG_WIDTH), (1, RG_WIDTH), (HW, HEAD_DIM), (HW, HEAD_DIM))

    def const(arr):
        zeros = (0,) * arr.ndim
        return pl.BlockSpec(arr.shape, lambda l, b: zeros)

    def per_layer(arr):
        tail = arr.shape[1:]
        return pl.BlockSpec((None,) + tail, lambda l, b: (l,) + (0,) * len(tail))

    def per_lb(tail):
        return pl.BlockSpec((None, bt) + tail, lambda l, b: (l, b) + (0,) * len(tail))

    x_spec = pl.BlockSpec((bt, tile, D_MODEL), lambda l, b: (b, 0, 0))
    rope_spec = pl.BlockSpec((tile, LANES), lambda l, b: (0, 0))
    args = [x] + list(rope) + list(states) + list(params) + list(tabs)
    specs = ([x_spec] + [rope_spec] * 2 + [per_lb(t) for t in state_tails]
             + [per_layer(p) if p.ndim > 2 else const(p) for p in params] + [const(t) for t in tabs])
    xo_spec = pl.BlockSpec((bt, tile, D_MODEL), lambda l, b: (jnp.where(l == depth - 1, b, 0), 0, 0))
    out_shape = ([jax.ShapeDtypeStruct(x.shape, F32)]
                 + [jax.ShapeDtypeStruct((depth, batch) + t, F32) for t in state_tails])
    out_specs = [xo_spec] + [per_lb(t) for t in state_tails]
    stream = [pltpu.VMEM((batch // bt, bt * tile, D_MODEL), F32)]
    return pl.pallas_call(
        kern, grid=grid, in_specs=specs, out_specs=out_specs, out_shape=out_shape,
        scratch_shapes=_scratch_shapes(bt, tile, blk) + stream,
        compiler_params=pltpu.CompilerParams(dimension_semantics=("arbitrary", "arbitrary"),
                                             vmem_limit_bytes=VMEM_LIMIT_BYTES),
        name=f"layers_t{tile}",
    )(*args)


def _layer_call(layer, depth, x, rope, states, params, tabs, layout, n_lv, tiling):
    bt, tile, chunk, blk, lookahead = tiling
    batch, seq, _ = x.shape
    n_tiles = seq // tile
    all_layers = layer is None
    assert not all_layers or (n_tiles == 1 and not lookahead and states is not None)
    grid = (depth, batch // bt) if all_layers else (batch // bt, n_tiles)
    n_steps = grid[0] * grid[1]
    zero_init = states is None
    kern = functools.partial(_layer_kernel, (layer, depth, bt, tile, chunk, n_lv, blk, lookahead, zero_init, layout))
    if all_layers:
        return _all_layers_call(kern, depth, x, rope, states, params, tabs, bt, tile, blk, grid)

    def layer_const(arr):
        tail = arr.shape[1:]
        return pl.BlockSpec((None,) + tail, lambda b, i: (layer,) + (0,) * len(tail))

    def const(arr):
        zeros = (0,) * arr.ndim
        return pl.BlockSpec(arr.shape, lambda b, i: zeros)

    def per_b(tail):
        return pl.BlockSpec((bt,) + tail, lambda b, i: (b,) + (0,) * len(tail))

    def per_lb(tail):
        return pl.BlockSpec((None, bt) + tail, lambda b, i: (layer, b) + (0,) * len(tail))

    def next_block(b, i):
        f = jnp.minimum(b * n_tiles + i + 1, n_steps - 1)
        return (f // n_tiles, (f % n_tiles) * (tile // blk), 0)

    state_tails = ((CONV_W - 1, RG_WIDTH), (1, RG_WIDTH), (HW, HEAD_DIM), (HW, HEAD_DIM))
    x_spec = pl.BlockSpec((bt, tile, D_MODEL), lambda b, i: (b, i, 0))
    rope_spec = pl.BlockSpec((tile, LANES), lambda b, i: (i, 0))
    args, specs = [x], [x_spec]
    if lookahead:
        args.append(x)
        specs.append(pl.BlockSpec((1, blk, D_MODEL), next_block))
    args += list(rope)
    specs += [rope_spec] * 2
    if not zero_init:
        args += list(states)
        specs += [per_lb(t) for t in state_tails]
    args += list(params)
    specs += [layer_const(p) if p.ndim > 2 else const(p) for p in params]
    args += list(tabs)
    specs += [const(t) for t in tabs]

    out_shape = [jax.ShapeDtypeStruct(x.shape, F32)] + [jax.ShapeDtypeStruct((batch,) + t, F32) for t in state_tails]
    out_specs = [x_spec] + [per_b(t) for t in state_tails]
    return pl.pallas_call(
        kern, grid=grid, in_specs=specs, out_specs=out_specs, out_shape=out_shape,
        scratch_shapes=_scratch_shapes(bt, tile, blk),
        compiler_params=pltpu.CompilerParams(dimension_semantics=("arbitrary", "arbitrary"),
                                             vmem_limit_bytes=VMEM_LIMIT_BYTES),
        name=f"layer{layer}_t{tile}",
    )(*args)


def kernel(x_prompt, x_sample, cache_conv, state_rglru, state_ret, state_hgrn, w_in, conv_w, conv_b, rg_wa, rg_ba,
           rg_wx, rg_bx, rg_lambda, ret_gn_g, hg_bf, hg_lb_logits, hg_norm_g, w_out, ln_g, ln_b):
    depth = w_in.shape[0]
    bp, lp, _ = x_prompt.shape
    bs, ls, _ = x_sample.shape

    streams = []
    for batch, seq, pos0 in ((bp, lp, 0), (bs, ls, PAST_LEN)):
        tiling = _tiling(batch, seq)
        tabs_np, n_lv = _mixer_tables(tiling[2])
        (tab_f32, tab_bf16), layout = _pack_tables(tabs_np)
        tabs = (jnp.asarray(tab_f32, F32), jnp.asarray(tab_bf16, BF16))
        rope = tuple(jnp.asarray(t, F32) for t in _rope_tables(pos0 + np.arange(seq)))
        streams.append((tiling, n_lv, tabs, layout, rope))

    halves = lambda w: w.reshape(depth, 2, HW, RG_BLOCK)
    params = (w_in.astype(BF16), halves(rg_wa), halves(rg_wx), w_out.astype(BF16), conv_w, conv_b, rg_ba, rg_bx,
              rg_lambda, ret_gn_g, hg_bf, hg_norm_g, hg_lb_logits, ln_g, ln_b)
    sample_states = (cache_conv, state_rglru[:, :, None, :], state_ret.reshape(depth, bs, HW, HEAD_DIM),
                     state_hgrn.reshape(depth, bs, HW, HEAD_DIM))

    xp = x_prompt
    outs_p = []
    tiling, n_lv, tabs, layout, rope = streams[0]
    for l in range(depth):
        res = _layer_call(l, depth, xp, rope, None, params, tabs, layout, n_lv, tiling)
        xp = res[0]
        outs_p.append(res[1:])
    conv_p = jnp.stack([o[0] for o in outs_p])
    h_p = jnp.stack([o[1][:, 0, :] for o in outs_p])
    ret_p = jnp.stack([o[2] for o in outs_p]).reshape(depth, bp, HEADS, HEAD_DIM, HEAD_DIM)
    hg_p = jnp.stack([o[3] for o in outs_p]).reshape(depth, bp, HEADS, HEAD_DIM, HEAD_DIM)

    tiling, n_lv, tabs, layout, rope = streams[1]
    xs, conv_s, h_s, ret_s, hg_s = _layer_call(None, depth, x_sample, rope, sample_states, params, tabs, layout, n_lv,
                                               tiling)
    shape_s = (depth, bs, HEADS, HEAD_DIM, HEAD_DIM)
    return (xp, xs, conv_p, h_p, ret_p, hg_p,
            conv_s, h_s.reshape(depth, bs, RG_WIDTH), ret_s.reshape(shape_s), hg_s.reshape(shape_s))
```

```python
import functools
import math

import jax
import jax.numpy as jnp
import numpy as np
from jax import lax
from jax.experimental import pallas as pl
from jax.experimental.pallas import tpu as pltpu

F32 = jnp.float32
BF16 = jnp.bfloat16

D_MODEL = 1024
RG_WIDTH = 512
RG_BLOCKS = 8
RG_BLOCK = RG_WIDTH // RG_BLOCKS
CONV_W = 4
RG_C = 8.0
HEADS = 4
HEAD_DIM = 64
HW = HEADS * HEAD_DIM
ROPE_BASE = 10000.0
LN_EPS = 1e-5
F_EPS = 1e-6
PAST_LEN = 1024
SPLITS = (RG_WIDTH, RG_WIDTH, HW, HW, HW, HW, HW, HW, HW, HW)
D_IN = sum(SPLITS)
SEG = tuple(int(v) for v in np.cumsum((0,) + SPLITS))
N_NARROW = 8

SUBLANES = 8
LANES = 128
VMEM_LIMIT_BYTES = 56 * 1024 * 1024

MAX_CHUNK = 64
BLOCK_ROWS = 256
PROJ_COLS = 256
MAX_TILE = 1024
LOCKSTEP = 1
LEVEL0_ROWS = SUBLANES

QUADS = ((slice(0, 128), slice(0, 128)), (slice(128, 256), slice(128, 256)))


def _mixer_tables(chunk):
    c = chunk
    t = np.arange(c)
    hid = np.repeat(np.arange(HEADS), HEAD_DIM)
    row_h = np.repeat(np.arange(HEADS), c)
    scale = HEAD_DIM ** -0.5

    log_g = np.log1p(-np.exp2(-5.0 - np.arange(HEADS)))
    rel = t[:, None] - t[None, :]
    dmat = np.where(rel >= 0, np.exp(np.maximum(rel, 0)[None] * log_g[:, None, None]), 0.0)
    bd = (hid[:, None] == hid[None, :]).astype(np.float64)
    tabs = {
        "dall": scale * dmat.transpose(1, 0, 2).reshape(c, HEADS * c),
        "gq": scale * np.repeat(np.exp((t[:, None] + 1.0) * log_g[None, :]), HEAD_DIM, 1),
        "gk": np.exp((c - 1.0 - t)[:, None] * log_g[None, :])[:, hid],
        "gs": np.exp(c * log_g)[hid][:, None] * np.ones((1, HW)),
        "bd": bd,
        "hm": (row_h[:, None] == hid[None, :]).astype(np.float64),
        "e64": bd / HEAD_DIM,
    }

    n_lv = int(round(math.log2(c // LEVEL0_ROWS)))
    tri = (t[None, :] <= t[:, None]).astype(np.float64)
    signs = []
    lvl = np.full((c, c), -1.0)
    causal = t[None, :] <= t[:, None]
    same0 = (t[:, None] // LEVEL0_ROWS) == (t[None, :] // LEVEL0_ROWS)
    lvl[causal & same0] = 0.0
    assigned = same0.copy()
    for lv in range(1, n_lv + 1):
        g = LEVEL0_ROWS * 2 ** lv
        upper = (t % g) >= g // 2
        signs.append(np.where(upper, 1.0, -1.0)[:, None] * np.ones((1, HW)))
        same = (t[:, None] // g) == (t[None, :] // g)
        lvl[causal & same & ~assigned] = float(lv)
        assigned |= same
    tabs["tri3"] = np.concatenate([tri, tri, tri], axis=1)
    tabs["lsgn"] = np.concatenate(signs, axis=0)
    tabs["lvl"] = np.tile(lvl, (1, HEADS))
    return tabs, n_lv


_F32_TABLES = ("dall", "gq", "gk", "lvl", "lsgn", "gs", "bd")
_BF16_TABLES = ("hm", "e64", "tri3")


def _pack_tables(tabs):
    packed, layout = [], {}
    for which, names in enumerate((_F32_TABLES, _BF16_TABLES)):
        row0, parts = 0, []
        for name in names:
            t = tabs[name]
            layout[name] = (which, row0, t.shape[0], t.shape[1])
            parts.append(np.pad(t, ((0, 0), (0, HW - t.shape[1]))))
            row0 += t.shape[0]
        packed.append(np.concatenate(parts, axis=0))
    return packed, tuple(sorted(layout.items()))


def _rope_tables(pos):
    half = HEAD_DIM // 2
    inv = ROPE_BASE ** (-np.arange(half, dtype=np.float64) / half)
    ang = pos.astype(np.float64)[:, None] * inv[None, :]
    cos = np.tile(np.cos(ang), (1, 2 * LANES // HEAD_DIM))
    sin = np.tile(np.concatenate([-np.sin(ang), np.sin(ang)], axis=1), (1, LANES // HEAD_DIM))
    return cos, sin


def _silu(x):
    return x * jax.nn.sigmoid(x)


def _dot(a, b):
    return jnp.dot(a, b, preferred_element_type=F32)


def _dot_t(a, b):
    return lax.dot_general(a, b, (((1,), (1,)), ((), ())), preferred_element_type=F32)


def _tdot(a, b):
    return lax.dot_general(a, b, (((0,), (0,)), ((), ())), preferred_element_type=F32)


def _seg_means(xs, e64):
    rows = xs[0].shape[0]
    x = xs[0] if len(xs) == 1 else jnp.concatenate(xs, axis=0)
    m = _dot(x.astype(BF16), e64)
    return [m[i * rows:(i + 1) * rows] for i in range(len(xs))]


def _stack_heads(x, mask):
    xb = x.astype(BF16)
    return jnp.concatenate([xb] * HEADS, axis=0) * mask


def _rotary(x, cos, sin, first_half):
    partner = jnp.where(first_half, pltpu.roll(x, HW - HEAD_DIM // 2, axis=1), pltpu.roll(x, HEAD_DIM // 2, axis=1))
    return x * cos + partner * sin


def _rg_scan(a, b, h_prev):
    c, w = a.shape
    g = c // SUBLANES
    a3 = a.reshape(g, SUBLANES, w)
    b3 = b.reshape(g, SUBLANES, w)
    sub = lax.broadcasted_iota(jnp.int32, a3.shape, 1)
    shift = 1
    while shift < SUBLANES:
        keep = sub >= shift
        a_sh = pltpu.roll(a3, shift, axis=1)
        b_sh = pltpu.roll(b3, shift, axis=1)
        b3 = jnp.where(keep, a3 * b_sh + b3, b3)
        a3 = jnp.where(keep, a3 * a_sh, a3)
        shift *= 2
    outs = []
    hb = h_prev
    for gi in range(g):
        hg = a3[gi] * hb + b3[gi]
        outs.append(hg)
        hb = jnp.broadcast_to(hg[SUBLANES - 1:SUBLANES, :], (SUBLANES, w))
    return jnp.concatenate(outs, axis=0), hb


def _group_row(x, g, r):
    return jnp.concatenate([jnp.broadcast_to(x[g0 + r:g0 + r + 1, :], (g, x.shape[1]))
                            for g0 in range(0, x.shape[0], g)], axis=0)


def _rows(start, size):
    return pl.ds(start if isinstance(start, int) else pl.multiple_of(start, size), size)


def _update_state(ref, bi, old, scale, mask_ref, row_op, col_op):
    for rs, cs in QUADS:
        sc = scale[rs, cs] if scale.shape[0] > 1 else scale[:, cs]
        ref[bi, rs, cs] = sc * old[rs, cs] + mask_ref[rs, cs] * _tdot(row_op[:, rs], col_op[:, cs])


def _head_transpose(s):
    return jnp.concatenate([s[h * HEAD_DIM:(h + 1) * HEAD_DIM, :].T for h in range(HEADS)], axis=0)


def _expand_state(s, bd):
    return jnp.concatenate([s] * HEADS, axis=1) * bd


def _compact_state(s):
    out = s[:, 0:HEAD_DIM]
    for h in range(1, HEADS):
        out = out + s[:, h * HEAD_DIM:(h + 1) * HEAD_DIM]
    return out


def _layer_kernel(cfg, *refs):
    layer, depth, bt, tile, c, n_lv, blk, lookahead, zero_init, layout = cfg
    n = bt * tile
    nb = n // blk
    it = iter(refs)
    x_ref = next(it)
    xn_ref = next(it) if lookahead else None
    cos_ref, sin_ref = next(it), next(it)
    conv_in, h_in, sret_in, shg_in = (None,) * 4 if zero_init else (next(it), next(it), next(it), next(it))
    (win_ref, wa_ref, wx_ref, wout_ref, convw_ref, convb_ref, rgba_ref, rgbx_ref, rglam_ref, gng_ref, hgbf_ref,
     hgng_ref, lbl_ref, lng_ref, lnb_ref) = [next(it) for _ in range(15)]
    packed = (next(it), next(it))
    tab = {name: packed[which].at[row0:row0 + rows, 0:lanes] for name, (which, row0, rows, lanes) in layout}
    dall_ref, gq_ref, gk_ref, gs_ref, bd_ref = tab["dall"], tab["gq"], tab["gk"], tab["gs"], tab["bd"]
    hm_ref, tri3_ref, lsgn_ref, lvl_ref, e64_ref = tab["hm"], tab["tri3"], tab["lsgn"], tab["lvl"], tab["e64"]
    xo_ref, conv_o, h_o, sret_o, shg_o = [next(it) for _ in range(5)]
    scr = list(it)
    sets = (scr[0:11], scr[11:22])
    conv_c, h_c, sret_c, shg_c = scr[22:26]
    wg_s = scr[26]
    seg_rows = blk if bt == 1 else tile
    dn_alpha = (2 * depth) ** 0.25
    all_layers = layer is None
    if all_layers:
        xs_s = scr[27]
        lyr, blk_id = pl.program_id(0), pl.program_id(1)
        first_tile = last_tile = True
        new_weights = blk_id == 0

        @pl.when(lyr == 0)
        def _take_input():
            xs_s[blk_id] = x_ref[...].reshape(n, D_MODEL)
    else:
        lyr = layer
        first_tile = pl.program_id(1) == 0
        last_tile = pl.program_id(1) == pl.num_programs(1) - 1
        new_weights = (pl.program_id(0) == 0) & (pl.program_id(1) == 0)

    @pl.when(first_tile)
    def _load_state():
        if zero_init:
            for ref in (conv_c, h_c, sret_c, shg_c):
                ref[...] = jnp.zeros(ref.shape, F32)
        else:
            conv_c[:, 0:SUBLANES - (CONV_W - 1), :] = jnp.zeros((bt, SUBLANES - (CONV_W - 1), RG_WIDTH), F32)
            conv_c[:, SUBLANES - (CONV_W - 1):SUBLANES, :] = conv_in[...]
            h_c[...] = jnp.broadcast_to(h_in[...], (bt, SUBLANES, RG_WIDTH))
            for bi in range(bt):
                sret_c[bi] = _expand_state(sret_in[bi], bd_ref[...])
                shg_c[bi] = _expand_state(_head_transpose(shg_in[bi]), bd_ref[...])

    @pl.when(new_weights)
    def _expand_gate_weights():
        for gi, ref in enumerate((wa_ref, wx_ref)):
            for hf in range(2):
                wg_s[gi, hf] = _expand_state(ref[hf], bd_ref[...]).astype(BF16)

    row = pl.ds(lyr, 1)
    conv_b = convb_ref[row, :]
    conv_w = [convw_ref[j:j + 1, :] for j in range(CONV_W)]
    rg_ba = rgba_ref[row, :]
    rg_bx = rgbx_ref[row, :]
    c_lam = RG_C * jax.nn.log_sigmoid(rglam_ref[row, :])
    gn_g = gng_ref[row, :]
    hg_bf = hgbf_ref[row, :]
    hg_ng = hgng_ref[row, :]
    logits = [lbl_ref[li:li + 1, :] for li in range(depth)]
    mx = functools.reduce(jnp.maximum, logits)
    ex = [jnp.exp(v - mx) for v in logits]
    den = functools.reduce(lambda s, v: s + v, ex)
    probs = [v / den for v in ex]
    lbs = [functools.reduce(lambda s, v: s + v, probs[:li + 1]) - probs[0] for li in range(depth)]
    if all_layers:
        lb = functools.reduce(lambda acc, li: jnp.where(lyr == li, lbs[li], acc), range(1, depth), lbs[0])
    else:
        lb = lbs[layer]
    one_m_lb = 1.0 - lb

    def project_steps(xblk, dst):
        xb_s = dst[10]

        def stage():
            xb_s[...] = xblk.astype(BF16)

        def step(lo):
            def run():
                val = _dot(xb_s[...], win_ref[:, lo:lo + PROJ_COLS])
                for gi in range(2 + N_NARROW):
                    g_lo, g_hi = max(SEG[gi], lo), min(SEG[gi + 1], lo + PROJ_COLS)
                    if g_lo >= g_hi:
                        continue
                    part = val[:, g_lo - lo:g_hi - lo]
                    if gi == 0:
                        for si in range(blk // seg_rows):
                            dst[0][si, SUBLANES:SUBLANES + seg_rows, g_lo:g_hi] = part[si * seg_rows:(si + 1) * seg_rows]
                    else:
                        dst[gi][:, g_lo - SEG[gi]:g_hi - SEG[gi]] = part
            return run

        return [stage] + [step(lo) for lo in range(0, D_IN, PROJ_COLS)]

    def mix(src, base, filler=()):
        filler = list(filler)
        n_ch = blk // c
        lockstep = n_ch if bt > 1 else min(n_ch, LOCKSTEP)
        n_grp = -(-n_ch // lockstep)
        fill_total = 2.0 + 4 * n_ch + 2 * n_grp
        issued = [0.0, 0]

        def fill(weight):
            issued[0] += weight
            due = math.ceil(len(filler) * min(issued[0] / fill_total, 1.0) - 1e-9)
            while issued[1] < due:
                filler[issued[1]]()
                issued[1] += 1

        rgx_s, rgg_s, q_s, k_s, v_s, rgate_s, hq_s, hf_s, hi_s, hgate_s = src[:10]
        rgx_s[:, 0:SUBLANES, :] = conv_c[...]
        half = RG_WIDTH // 2
        a_hs, b_hs = [], []
        for hf in range(2):
            cs = slice(hf * half, (hf + 1) * half)
            us = []
            for si in range(blk // seg_rows):
                win = rgx_s[si, :, cs]
                u_seg = conv_b[:, cs] + win[SUBLANES:, :] * conv_w[CONV_W - 1][:, cs]
                for back in range(1, CONV_W):
                    u_seg = u_seg + pltpu.roll(win, back, axis=0)[SUBLANES:, :] * conv_w[CONV_W - 1 - back][:, cs]
                us.append(u_seg)
            u = us[0] if len(us) == 1 else jnp.concatenate(us, axis=0)
            ub = u.astype(BF16)
            log_a = c_lam[:, cs] * jax.nn.sigmoid(_dot(ub, wg_s[0, hf]) + rg_ba[:, cs])
            a = jnp.exp(log_a)
            b_in = jnp.sqrt(-jnp.tanh(log_a) * (a * a + 1.0)) * (jax.nn.sigmoid(_dot(ub, wg_s[1, hf]) + rg_bx[:, cs]) * u)
            a_hs.append(a)
            b_hs.append(b_in)
            fill(1.0)
        conv_c[...] = rgx_s[:, seg_rows:seg_rows + SUBLANES, :]
        e64 = e64_ref[...]
        hm = hm_ref[...]
        lvl = lvl_ref[...]
        first_half = (lax.broadcasted_iota(jnp.int32, (c, HW), 1) % HEAD_DIM) < HEAD_DIM // 2

        per_seq = seg_rows // c
        st_all = [dict(rows=slice(ci * c, (ci + 1) * c), bi=ci // per_seq,
                       trow=_rows(base + (ci % per_seq) * c, c)) for ci in range(n_ch)]

        for g0 in range(0, n_ch, lockstep):
            st = st_all[g0:g0 + lockstep]
            for d in st:
                rows, trow = d["rows"], d["trow"]
                cos = jnp.concatenate([cos_ref[trow, :]] * (HW // LANES), axis=1)
                sin = jnp.concatenate([sin_ref[trow, :]] * (HW // LANES), axis=1)
                d["kr"] = _rotary(k_s[rows, :], cos, sin, first_half)
                d["qb"] = _rotary(q_s[rows, :], cos, sin, first_half).astype(BF16)
                d["scores"] = _dot_t(d["qb"], _stack_heads(d["kr"], hm))
                z = hf_s[rows, :] + hg_bf
                ez = jnp.exp(-jnp.abs(z))
                inv = 1.0 / (1.0 + ez)
                pos = z >= 0.0
                sig_p = jnp.where(pos, inv, ez * inv)
                sig_n = jnp.where(pos, ez * inv, inv)
                log_f = jnp.log(jnp.maximum(lb + one_m_lb * sig_p, F_EPS))
                d["kc"] = one_m_lb * sig_n
                f_hi = log_f.astype(BF16)
                res = log_f - f_hi.astype(F32)
                f_mid = res.astype(BF16)
                f_lo = (res - f_mid.astype(F32)).astype(BF16)
                d["cum"] = _dot(tri3_ref[...], jnp.concatenate([f_hi, f_mid, f_lo], axis=0))
                fill(1.0)

            for d in st:
                bi, kr, qb = d["bi"], d["kr"], d["qb"]
                v = v_s[d["rows"], :]
                p = (d["scores"] * dall_ref[...]).astype(BF16)
                s_ret = sret_c[bi]
                d["o_b"] = _dot(p, _stack_heads(v, hm)) + _dot(qb, s_ret.astype(BF16)) * gq_ref[...]
                _update_state(sret_c, bi, s_ret, gs_ref, bd_ref, (kr * gk_ref[...]).astype(BF16), v.astype(BF16))
                fill(1.0)
                cum, kc = d["cum"], d["kc"]
                qh = hq_s[d["rows"], :]
                e0 = jnp.exp(cum - _group_row(cum, LEVEL0_ROWS, LEVEL0_ROWS // 2 - 1))
                s0 = _dot_t((qh * e0).astype(BF16), _stack_heads(kc / e0, hm))
                pm = jnp.where(lvl == 0.0, s0, 0.0)
                for lv in range(1, n_lv + 1):
                    g = LEVEL0_ROWS * 2 ** lv
                    el = jnp.exp(lsgn_ref[(lv - 1) * c:lv * c, :] * (cum - _group_row(cum, g, g // 2 - 1)))
                    sl_ = _dot_t((qh * el).astype(BF16), _stack_heads(kc * el, hm))
                    pm = jnp.where(lvl == float(lv), sl_, pm)
                d["pm"] = pm
                fill(1.0)

            means = _seg_means([d["o_b"] for d in st], e64)
            for d, m in zip(st, means):
                d["dev"] = d["o_b"] - m
            fill(1.0)
            for d in st:
                bi, cum = d["bi"], d["cum"]
                vh = hi_s[d["rows"], :]
                qh = hq_s[d["rows"], :]
                s_hg = shg_c[bi]
                d["o_c"] = (_dot(d["pm"].astype(BF16), _stack_heads(vh, hm))
                            + _dot_t((qh * jnp.exp(cum)).astype(BF16), s_hg.astype(BF16)))
                k_st = (d["kc"] * jnp.exp(cum[c - 1:c, :] - cum)).astype(BF16)
                dec = jnp.exp(cum[c - 1:c, :])
                _update_state(shg_c, bi, s_hg, dec, bd_ref, vh.astype(BF16), k_st)
                fill(1.0)

            stats = _seg_means([d["dev"] * d["dev"] for d in st] + [d["o_c"] * d["o_c"] for d in st], e64)
            fill(1.0)
            for d, var_b, ms_c in zip(st, stats[:len(st)], stats[len(st):]):
                d["var_b"], d["ms_c"] = var_b, ms_c

        ys = []
        for d in st_all:
            rows, bi = d["rows"], d["bi"]
            y_a = []
            for hf in range(2):
                cs = slice(hf * half, (hf + 1) * half)
                h, h_last = _rg_scan(a_hs[hf][rows], b_hs[hf][rows], h_c[bi, :, cs])
                h_c[bi, :, cs] = h_last
                y_a.append((h * _silu(rgg_s[rows, cs])).astype(BF16))
            y_b = d["dev"] * lax.rsqrt(d["var_b"] + LN_EPS) * gn_g * _silu(rgate_s[rows, :])
            y_c = d["o_c"] * lax.rsqrt(d["ms_c"] + LN_EPS) * hg_ng * _silu(hgate_s[rows, :])
            ys.append(jnp.concatenate(y_a + [y_b.astype(BF16), y_c.astype(BF16)], axis=1))
        y = ys[0] if n_ch == 1 else jnp.concatenate(ys, axis=0)

        assert abs(issued[0] - fill_total) < 1e-6 and issued[1] == len(filler)

        if bt == 1:
            xrows = _rows(base, blk)
            x_in = x_ref[0, xrows, :]
        elif all_layers:
            x_in = xs_s[blk_id]
        else:
            x_in = x_ref[...].reshape(n, D_MODEL)
        xn = dn_alpha * x_in + _dot(y, wout_ref[...])
        mu = jnp.mean(xn, axis=-1, keepdims=True)
        dv = xn - mu
        var = jnp.mean(dv * dv, axis=-1, keepdims=True)
        out = dv * lax.rsqrt(var + LN_EPS) * lng_ref[row, :] + lnb_ref[row, :]
        if bt == 1:
            xo_ref[0, xrows, :] = out
        else:
            xo_ref[...] = out.reshape(bt, tile, D_MODEL)
            if all_layers:
                xs_s[blk_id] = out

    def project(xblk, dst):
        for run in project_steps(xblk, dst):
            run()

    if not lookahead:
        project(xs_s[blk_id] if all_layers else x_ref[...].reshape(n, D_MODEL), sets[0])
        mix(sets[0], 0)
    else:
        @pl.when((pl.program_id(0) == 0) & (pl.program_id(1) == 0))
        def _first_block():
            project(x_ref[0, 0:blk, :], sets[0])

        def pair_body(k, carry):
            base0 = pl.multiple_of(2 * k * blk, 2 * blk)
            mix(sets[0], base0, project_steps(x_ref[0, _rows(base0 + blk, blk), :], sets[1]))
            in_tile = 2 * k + 2 < nb
            nxt = jnp.minimum(2 * k + 2, nb - 1) * blk
            ahead = jnp.where(in_tile, x_ref[0, _rows(nxt, blk), :], xn_ref[0])
            mix(sets[1], base0 + blk, project_steps(ahead, sets[0]))
            return carry

        lax.fori_loop(0, nb // 2, pair_body, 0)

    @pl.when(last_tile)
    def _store_state():
        conv_o[...] = conv_c[:, SUBLANES - (CONV_W - 1):SUBLANES, :]
        h_o[...] = h_c[:, 0:1, :]
        for bi in range(bt):
            sret_o[bi] = _compact_state(sret_c[bi])
            shg_o[bi] = _head_transpose(_compact_state(shg_c[bi]))


def _tiling(batch, seq):
    chunk = min(MAX_CHUNK, seq)
    tile = min(seq, MAX_TILE)
    bt = 1 if tile >= BLOCK_ROWS else min(batch, BLOCK_ROWS // tile)
    blk = min(BLOCK_ROWS, bt * tile)
    lookahead = bt == 1 and (tile // blk) >= 2
    assert seq % tile == 0 and batch % bt == 0 and tile % chunk == 0 and chunk % 16 == 0 and blk % chunk == 0
    assert lookahead or bt * tile == blk
    assert not lookahead or (tile // blk) % 2 == 0
    return bt, tile, chunk, blk, lookahead


def _scratch_shapes(bt, tile, blk):
    seg_rows = blk if bt == 1 else tile
    one_set = ([pltpu.VMEM((blk // seg_rows, SUBLANES + seg_rows, RG_WIDTH), F32), pltpu.VMEM((blk, RG_WIDTH), F32)]
               + [pltpu.VMEM((blk, HW), F32)] * N_NARROW + [pltpu.VMEM((blk, D_MODEL), BF16)])
    carries = [pltpu.VMEM((bt, SUBLANES, RG_WIDTH), F32)] * 2 + [pltpu.VMEM((bt, HW, HW), F32)] * 2
    gate_w = [pltpu.VMEM((2, 2, HW, HW), BF16)]
    return one_set * 2 + carries + gate_w


def _all_layers_call(kern, depth, x, rope, states, params, tabs, bt, tile, blk, grid):
    batch = x.shape[0]
    state_tails = ((CONV_W - 1, RG_WIDTH), (1, RG_WIDTH), (HW, HEAD_DIM), (HW, HEAD_DIM))

    def const(arr):
        zeros = (0,) * arr.ndim
        return pl.BlockSpec(arr.shape, lambda l, b: zeros)

    def per_layer(arr):
        tail = arr.shape[1:]
        return pl.BlockSpec((None,) + tail, lambda l, b: (l,) + (0,) * len(tail))

    def per_lb(tail):
        return pl.BlockSpec((None, bt) + tail, lambda l, b: (l, b) + (0,) * len(tail))

    x_spec = pl.BlockSpec((bt, tile, D_MODEL), lambda l, b: (b, 0, 0))
    rope_spec = pl.BlockSpec((tile, LANES), lambda l, b: (0, 0))
    args = [x] + list(rope) + list(states) + list(params) + list(tabs)
    specs = ([x_spec] + [rope_spec] * 2 + [per_lb(t) for t in state_tails]
             + [per_layer(p) if p.ndim > 2 else const(p) for p in params] + [const(t) for t in tabs])
    xo_spec = pl.BlockSpec((bt, tile, D_MODEL), lambda l, b: (jnp.where(l == depth - 1, b, 0), 0, 0))
    out_shape = ([jax.ShapeDtypeStruct(x.shape, F32)]
                 + [jax.ShapeDtypeStruct((depth, batch) + t, F32) for t in state_tails])
    out_specs = [xo_spec] + [per_lb(t) for t in state_tails]
    stream = [pltpu.VMEM((batch // bt, bt * tile, D_MODEL), F32)]
    return pl.pallas_call(
        kern, grid=grid, in_specs=specs, out_specs=out_specs, out_shape=out_shape,
        scratch_shapes=_scratch_shapes(bt, tile, blk) + stream,
        compiler_params=pltpu.CompilerParams(dimension_semantics=("arbitrary", "arbitrary"),
                                             vmem_limit_bytes=VMEM_LIMIT_BYTES),
        name=f"layers_t{tile}",
    )(*args)


def _layer_call(layer, depth, x, rope, states, params, tabs, layout, n_lv, tiling):
    bt, tile, chunk, blk, lookahead = tiling
    batch, seq, _ = x.shape
    n_tiles = seq // tile
    all_layers = layer is None
    assert not all_layers or (n_tiles == 1 and not lookahead and states is not None)
    grid = (depth, batch // bt) if all_layers else (batch // bt, n_tiles)
    n_steps = grid[0] * grid[1]
    zero_init = states is None
    kern = functools.partial(_layer_kernel, (layer, depth, bt, tile, chunk, n_lv, blk, lookahead, zero_init, layout))
    if all_layers:
        return _all_layers_call(kern, depth, x, rope, states, params, tabs, bt, tile, blk, grid)

    def layer_const(arr):
        tail = arr.shape[1:]
        return pl.BlockSpec((None,) + tail, lambda b, i: (layer,) + (0,) * len(tail))

    def const(arr):
        zeros = (0,) * arr.ndim
        return pl.BlockSpec(arr.shape, lambda b, i: zeros)

    def per_b(tail):
        return pl.BlockSpec((bt,) + tail, lambda b, i: (b,) + (0,) * len(tail))

    def per_lb(tail):
        return pl.BlockSpec((None, bt) + tail, lambda b, i: (layer, b) + (0,) * len(tail))

    def next_block(b, i):
        f = jnp.minimum(b * n_tiles + i + 1, n_steps - 1)
        return (f // n_tiles, (f % n_tiles) * (tile // blk), 0)

    state_tails = ((CONV_W - 1, RG_WIDTH), (1, RG_WIDTH), (HW, HEAD_DIM), (HW, HEAD_DIM))
    x_spec = pl.BlockSpec((bt, tile, D_MODEL), lambda b, i: (b, i, 0))
    rope_spec = pl.BlockSpec((tile, LANES), lambda b, i: (i, 0))
    args, specs = [x], [x_spec]
    if lookahead:
        args.append(x)
        specs.append(pl.BlockSpec((1, blk, D_MODEL), next_block))
    args += list(rope)
    specs += [rope_spec] * 2
    if not zero_init:
        args += list(states)
        specs += [per_lb(t) for t in state_tails]
    args += list(params)
    specs += [layer_const(p) if p.ndim > 2 else const(p) for p in params]
    args += list(tabs)
    specs += [const(t) for t in tabs]

    out_shape = [jax.ShapeDtypeStruct(x.shape, F32)] + [jax.ShapeDtypeStruct((batch,) + t, F32) for t in state_tails]
    out_specs = [x_spec] + [per_b(t) for t in state_tails]
    return pl.pallas_call(
        kern, grid=grid, in_specs=specs, out_specs=out_specs, out_shape=out_shape,
        scratch_shapes=_scratch_shapes(bt, tile, blk),
        compiler_params=pltpu.CompilerParams(dimension_semantics=("arbitrary", "arbitrary"),
                                             vmem_limit_bytes=VMEM_LIMIT_BYTES),
        name=f"layer{layer}_t{tile}",
    )(*args)


def kernel(x_prompt, x_sample, cache_conv, state_rglru, state_ret, state_hgrn, w_in, conv_w, conv_b, rg_wa, rg_ba,
           rg_wx, rg_bx, rg_lambda, ret_gn_g, hg_bf, hg_lb_logits, hg_norm_g, w_out, ln_g, ln_b):
    depth = w_in.shape[0]
    bp, lp, _ = x_prompt.shape
    bs, ls, _ = x_sample.shape

    streams = []
    for batch, seq, pos0 in ((bp, lp, 0), (bs, ls, PAST_LEN)):
        tiling = _tiling(batch, seq)
        tabs_np, n_lv = _mixer_tables(tiling[2])
        (tab_f32, tab_bf16), layout = _pack_tables(tabs_np)
        tabs = (jnp.asarray(tab_f32, F32), jnp.asarray(tab_bf16, BF16))
        rope = tuple(jnp.asarray(t, F32) for t in _rope_tables(pos0 + np.arange(seq)))
        streams.append((tiling, n_lv, tabs, layout, rope))

    halves = lambda w: w.reshape(depth, 2, HW, RG_BLOCK)
    params = (w_in.astype(BF16), halves(rg_wa), halves(rg_wx), w_out.astype(BF16), conv_w, conv_b, rg_ba, rg_bx,
              rg_lambda, ret_gn_g, hg_bf, hg_norm_g, hg_lb_logits, ln_g, ln_b)
    sample_states = (cache_conv, state_rglru[:, :, None, :], state_ret.reshape(depth, bs, HW, HEAD_DIM),
                     state_hgrn.reshape(depth, bs, HW, HEAD_DIM))

    xp = x_prompt
    outs_p = []
    tiling, n_lv, tabs, layout, rope = streams[0]
    for l in range(depth):
        res = _layer_call(l, depth, xp, rope, None, params, tabs, layout, n_lv, tiling)
        xp = res[0]
        outs_p.append(res[1:])
    conv_p = jnp.stack([o[0] for o in outs_p])
    h_p = jnp.stack([o[1][:, 0, :] for o in outs_p])
    ret_p = jnp.stack([o[2] for o in outs_p]).reshape(depth, bp, HEADS, HEAD_DIM, HEAD_DIM)
    hg_p = jnp.stack([o[3] for o in outs_p]).reshape(depth, bp, HEADS, HEAD_DIM, HEAD_DIM)

    tiling, n_lv, tabs, layout, rope = streams[1]
    xs, conv_s, h_s, ret_s, hg_s = _layer_call(None, depth, x_sample, rope, sample_states, params, tabs, layout, n_lv,
                                               tiling)
    shape_s = (depth, bs, HEADS, HEAD_DIM, HEAD_DIM)
    return (xp, xs, conv_p, h_p, ret_p, hg_p,
            conv_s, h_s.reshape(depth, bs, RG_WIDTH), ret_s.reshape(shape_s), hg_s.reshape(shape_s))
```
